```python
import math
import jax
import jax.numpy as jnp
from jax import lax
import numpy as np

D_MODEL = 2048
BATCH = 1
SEQ = 8192
DEPTH = 2
DEC_BATCH = 4
DEC_SEQ = 2048
PAST_LEN = 128

D_MIX = D_MODEL
N_GROUPS = 4
GROUP_W = D_MIX // N_GROUPS
GDN_HEADS = 4
GDN_DK = 128
GDN_DV = GROUP_W // GDN_HEADS
GDN_KW = GDN_HEADS * GDN_DK
GDN_VW = GDN_HEADS * GDN_DV
LRU_W = GROUP_W
LRU_BLOCKS = 4
LRU_BW = LRU_W // LRU_BLOCKS
LRU_C = 8.0
SSD_DI = GROUP_W
SSD_HEADDIM = 64
SSD_HEADS = SSD_DI // SSD_HEADDIM
SSD_GROUPS = 2
SSD_STATE = 128
SSD_XBC = SSD_DI + 2 * SSD_GROUPS * SSD_STATE
HGRN_HEADS = 4
HGRN_DK = 128
HGRN_DV = GROUP_W // HGRN_HEADS
HGRN_KW = HGRN_HEADS * HGRN_DK
HGRN_VW = HGRN_HEADS * HGRN_DV
CONV_W = 4
FFN_CONV_W = 3
D_FF = 5632
CHUNK = 64
HGRN_CHUNK = 16
EPS = 1e-6
IN_SIZES = (2 * GDN_KW + GDN_VW, GDN_VW, 2 * GDN_HEADS, 2 * GDN_HEADS,
            LRU_W, LRU_W,
            SSD_DI, SSD_XBC, 2 * SSD_HEADS,
            HGRN_KW, 2 * HGRN_KW, HGRN_VW, HGRN_VW)
D_IN = sum(IN_SIZES)

kernel_name = 'hybrid_bidir_head_group_encoder'

F32 = jnp.float32


def _split(u, sizes):
    idx, acc = [], 0
    for s in sizes[:-1]:
        acc += s
        idx.append(acc)
    return jnp.split(u, idx, axis=-1)


def _rmsnorm(x, w):
    xf = x.astype(F32)
    y = xf * lax.rsqrt(jnp.mean(xf * xf, axis=-1, keepdims=True) + EPS)
    return (y * w.astype(F32)).astype(x.dtype)


def _l2norm(a):
    return a * lax.rsqrt(jnp.sum(a * a, axis=-1, keepdims=True) + EPS)


def _rev(a):
    return jnp.flip(a, axis=1)


def _dwconv(x, w, b=None):
    k = w.shape[0]
    y = lax.conv_general_dilated(
        x, w[:, None, :].astype(x.dtype), window_strides=(1,),
        padding=((k // 2, k - 1 - k // 2),),
        dimension_numbers=('NWC', 'WIO', 'NWC'),
        feature_group_count=x.shape[-1])
    if b is not None:
        y = y + b.astype(y.dtype)
    return y


def _gdn_chunked(q, k, v, g, beta):
    bsz, t, h, dk = q.shape
    dv = v.shape[-1]
    n = t // CHUNK

    def blocks(a):
        return jnp.moveaxis(a.reshape(bsz, n, CHUNK, h, -1), 3, 1)

    q, k, v = blocks(q), blocks(k), blocks(v)
    g = blocks(g[..., None])[..., 0]
    beta = blocks(beta[..., None])
    gc = jnp.cumsum(g, axis=-1)
    tri = jnp.tril(jnp.ones((CHUNK, CHUNK), dtype=bool))
    decay = jnp.exp(jnp.where(tri, gc[..., :, None] - gc[..., None, :], -jnp.inf))
    kb = k * beta
    a_low = jnp.tril(jnp.einsum('bhncd,bhnsd->bhncs', kb, k) * decay, -1)
    w = lax.linalg.triangular_solve(a_low, kb * jnp.exp(gc)[..., None],
                                    left_side=True, lower=True, unit_diagonal=True)
    u = lax.linalg.triangular_solve(a_low, v * beta,
                                    left_side=True, lower=True, unit_diagonal=True)
    qk = jnp.einsum('bhncd,bhnsd->bhncs', q, k) * decay
    q_dec = q * jnp.exp(gc)[..., None]
    k_dec = k * jnp.exp(gc[..., -1:] - gc)[..., None]
    g_last = jnp.exp(gc[..., -1])

    def step(state, inp):
        w_c, u_c, qk_c, qd_c, kd_c, gl_c = inp
        v_new = u_c - w_c @ state
        out = qd_c @ state + qk_c @ v_new
        state = state * gl_c[..., None, None] + jnp.swapaxes(kd_c, -1, -2) @ v_new
        return state, out

    xs = tuple(jnp.moveaxis(a, 2, 0) for a in (w, u, qk, q_dec, k_dec, g_last))
    s0 = jnp.zeros((bsz, h, dk, dv), F32)
    _, o = lax.scan(step, s0, xs)
    return jnp.transpose(o, (1, 0, 3, 2, 4)).reshape(bsz, t, h, dv)


def _gdn_mixer(qkv, z, beta_logit, alpha_logit, conv_w, a_log, dt_bias, norm_w):
    bsz, t, _ = qkv.shape
    qkv = jax.nn.silu(_dwconv(qkv, conv_w)).astype(F32)
    q, k, v = _split(qkv, (GDN_KW, GDN_KW, GDN_VW))
    q = _l2norm(q.reshape(bsz, t, GDN_HEADS, GDN_DK)) * (GDN_DK ** -0.5)
    k = _l2norm(k.reshape(bsz, t, GDN_HEADS, GDN_DK))
    v = v.reshape(bsz, t, GDN_HEADS, GDN_DV)
    beta = jax.nn.sigmoid(beta_logit.astype(F32)).reshape(bsz, t, 2, GDN_HEADS)
    g = -jnp.exp(a_log.astype(F32)) * jax.nn.softplus(
        alpha_logit.astype(F32).reshape(bsz, t, 2, GDN_HEADS) + dt_bias.astype(F32))
    o = (_gdn_chunked(q, k, v, g[:, :, 0], beta[:, :, 0])
         + _rev(_gdn_chunked(_rev(q), _rev(k), _rev(v), _rev(g[:, :, 1]), _rev(beta[:, :, 1]))))
    o = _rmsnorm(o, norm_w) * jax.nn.silu(z.astype(F32).reshape(bsz, t, GDN_HEADS, GDN_DV))
    return o.reshape(bsz, t, GDN_VW).astype(z.dtype)


def _lru_scan(log_a, gx):
    a = jnp.exp(log_a)
    mult = jnp.sqrt(-jnp.expm1(2.0 * log_a))
    mult = mult.at[:, 0].set(1.0)

    def combine(left, right):
        return left[0] * right[0], right[0] * left[1] + right[1]

    _, h = lax.associative_scan(combine, (a, mult * gx), axis=1)
    return h


def _rglru_mixer(xb, gate, conv_w, conv_b, wa, ba, wi, bi, lam):
    bsz, t, _ = xb.shape
    xc = _dwconv(xb, conv_w, conv_b).astype(F32)
    xblk = xc.reshape(bsz, t, LRU_BLOCKS, LRU_BW)
    r = jax.nn.sigmoid(jnp.einsum('btni,rnij->btrnj', xblk, wa.astype(F32)).reshape(bsz, t, 2, LRU_W)
                       + ba.astype(F32))
    ig = jax.nn.sigmoid(jnp.einsum('btni,rnij->btrnj', xblk, wi.astype(F32)).reshape(bsz, t, 2, LRU_W)
                        + bi.astype(F32))
    log_a = -LRU_C * r * jax.nn.softplus(-lam.astype(F32))
    gx = ig * xc[:, :, None]
    h = (_lru_scan(log_a[:, :, 0], gx[:, :, 0])
         + _rev(_lru_scan(_rev(log_a[:, :, 1]), _rev(gx[:, :, 1]))))
    return (h * jax.nn.gelu(gate.astype(F32))).astype(xb.dtype)


def _ssd_chunked(x, dt, a, bm, cm):
    bsz, t, h, p = x.shape
    n = t // CHUNK
    xdt = (x * dt[..., None]).reshape(bsz, n, CHUNK, h, p)
    bc = bm.reshape(bsz, n, CHUNK, h, -1)
    cc = cm.reshape(bsz, n, CHUNK, h, -1)
    acs = jnp.cumsum(jnp.moveaxis((dt * a).reshape(bsz, n, CHUNK, h), 3, 1), axis=-1)
    tri = jnp.tril(jnp.ones((CHUNK, CHUNK), dtype=bool))
    lmat = jnp.exp(jnp.where(tri, acs[..., :, None] - acs[..., None, :], -jnp.inf))
    scores = jnp.einsum('bnlhs,bnjhs->bhnlj', cc, bc) * lmat
    y_diag = jnp.einsum('bhnlj,bnjhp->bnlhp', scores, xdt)
    states = jnp.einsum('bnjhs,bhnj,bnjhp->bnhps', bc, jnp.exp(acs[..., -1:] - acs), xdt)

    def step(state, inp):
        st, dec = inp
        return state * dec[..., None, None] + st, state

    s0 = jnp.zeros((bsz, h, p, bc.shape[-1]), F32)
    _, s_in = lax.scan(step, s0, (jnp.moveaxis(states, 1, 0),
                                  jnp.moveaxis(jnp.exp(acs[..., -1]), 2, 0)))
    y_off = jnp.einsum('bnlhs,nbhps,bhnl->bnlhp', cc, s_in, jnp.exp(acs))
    return (y_diag + y_off).reshape(bsz, t, h, p)


def _ssd_mixer(z, xbc, dt_raw, conv_w, conv_b, a_log, dt_bias, d_skip, norm_w):
    bsz, t, _ = z.shape
    xbc = jax.nn.silu(_dwconv(xbc, conv_w, conv_b)).astype(F32)
    xs, bm, cm = _split(xbc, (SSD_DI, SSD_GROUPS * SSD_STATE, SSD_GROUPS * SSD_STATE))
    xs = xs.reshape(bsz, t, SSD_HEADS, SSD_HEADDIM)
    rep = SSD_HEADS // SSD_GROUPS
    bm = jnp.repeat(bm.reshape(bsz, t, SSD_GROUPS, SSD_STATE), rep, axis=2)
    cm = jnp.repeat(cm.reshape(bsz, t, SSD_GROUPS, SSD_STATE), rep, axis=2)
    dt = jax.nn.softplus(dt_raw.astype(F32).reshape(bsz, t, 2, SSD_HEADS) + dt_bias.astype(F32))
    a = -jnp.exp(a_log.astype(F32))
    y = (_ssd_chunked(xs, dt[:, :, 0], a[0], bm, cm)
         + _rev(_ssd_chunked(_rev(xs), _rev(dt[:, :, 1]), a[1], _rev(bm), _rev(cm))))
    y = y + d_skip.astype(F32)[:, None] * xs
    y = y.reshape(bsz, t, SSD_DI) * jax.nn.silu(z.astype(F32))
    y = _rmsnorm(y.reshape(bsz, t, SSD_GROUPS, SSD_DI // SSD_GROUPS),
                 norm_w.reshape(SSD_GROUPS, SSD_DI // SSD_GROUPS))
    return y.reshape(bsz, t, SSD_DI).astype(z.dtype)


def _hgrn_chunked(q, k, v, logf):
    bsz, t, h, dk = q.shape
    dv = v.shape[-1]
    n = t // HGRN_CHUNK
    q = q.reshape(bsz, n, HGRN_CHUNK, h, dk)
    k = k.reshape(bsz, n, HGRN_CHUNK, h, dk)
    v = v.reshape(bsz, n, HGRN_CHUNK, h, dv)
    b = jnp.cumsum(logf.reshape(bsz, n, HGRN_CHUNK, h, dk), axis=2)
    tri = jnp.tril(jnp.ones((HGRN_CHUNK, HGRN_CHUNK), dtype=bool))[:, :, None, None]
    dec = jnp.exp(jnp.where(tri, b[:, :, :, None] - b[:, :, None, :], -jnp.inf))
    scores = jnp.einsum('bnthk,bnshk,bntshk->bnhts', q, k, dec)
    y_intra = jnp.einsum('bnhts,bnshv->bnthv', scores, v)
    q_dec = q * jnp.exp(b)
    k_dec = k * jnp.exp(b[:, :, -1:] - b)
    g_last = jnp.exp(b[:, :, -1])

    def step(state, inp):
        qd, kd, gl, vc = inp
        out = jnp.einsum('bthk,bhkv->bthv', qd, state)
        state = state * gl[..., None] + jnp.einsum('bthk,bthv->bhkv', kd, vc)
        return state, out

    xs = tuple(jnp.moveaxis(a, 1, 0) for a in (q_dec, k_dec, g_last, v))
    s0 = jnp.zeros((bsz, h, dk, dv), F32)
    _, y_inter = lax.scan(step, s0, xs)
    return (y_intra + jnp.moveaxis(y_inter, 0, 1)).reshape(bsz, t, h, dv)


def _hgrn_mixer(q, f_logit, i, g, lb_param, layer, norm_w):
    bsz, t, _ = q.shape
    p = jax.nn.softmax(lb_param.astype(F32), axis=1)
    lb = (jnp.cumsum(p, axis=1) - p[:, :1])[:, layer]
    f = lb + (1.0 - lb) * jax.nn.sigmoid(f_logit.astype(F32).reshape(bsz, t, 2, HGRN_KW))
    logf = jnp.log(f).reshape(bsz, t, 2, HGRN_HEADS, HGRN_DK)
    kk = (1.0 - f).reshape(bsz, t, 2, HGRN_HEADS, HGRN_DK)
    qq = jax.nn.silu(q.astype(F32)).reshape(bsz, t, HGRN_HEADS, HGRN_DK)
    vv = i.astype(F32).reshape(bsz, t, HGRN_HEADS, HGRN_DV)
    o = (_hgrn_chunked(qq, kk[:, :, 0], vv, logf[:, :, 0])
         + _rev(_hgrn_chunked(_rev(qq), _rev(kk[:, :, 1]), _rev(vv), _rev(logf[:, :, 1]))))
    o = _rmsnorm(o, norm_w) * jax.nn.silu(g.astype(F32).reshape(bsz, t, HGRN_HEADS, HGRN_DV))
    return o.reshape(bsz, t, HGRN_VW).astype(q.dtype)


def setup_inputs(seed: int = 0) -> dict:
    key = jax.random.key(seed)
    ks = iter(jax.random.split(key, 40))

    def normal(shape, scale):
        return scale * jax.random.normal(next(ks), shape, F32)

    def gain(shape):
        return 1.0 + normal(shape, 0.02)

    def uniform(shape, lo, hi):
        return jax.random.uniform(next(ks), shape, F32, lo, hi)

    def dt_bias(shape):
        dt = jnp.exp(uniform(shape, math.log(1e-3), math.log(1e-1)))
        return dt + jnp.log(-jnp.expm1(-dt))

    s = jnp.power(uniform((DEPTH, 2, LRU_W), 0.9, 0.999), 1.0 / LRU_C)
    lam = jnp.log(s) - jnp.log1p(-s)
    return {
        'x_prompt': normal((BATCH, SEQ, D_MODEL), 1.0),
        'x_sample': normal((DEC_BATCH, DEC_SEQ, D_MODEL), 1.0),
        'ln1': gain((DEPTH, D_MODEL)),
        'w_in': normal((DEPTH, D_MODEL, D_IN), D_MODEL ** -0.5),
        'gdn_conv_w': normal((DEPTH, CONV_W, 2 * GDN_KW + GDN_VW), CONV_W ** -0.5),
        'gdn_a_log': jnp.log(uniform((DEPTH, 2, GDN_HEADS), 1.0, 16.0)),
        'gdn_dt_bias': dt_bias((DEPTH, 2, GDN_HEADS)),
        'gdn_norm_w': gain((DEPTH, GDN_DV)),
        'lru_conv_w': normal((DEPTH, CONV_W, LRU_W), CONV_W ** -0.5),
        'lru_conv_b': normal((DEPTH, LRU_W), 0.01),
        'lru_wa': normal((DEPTH, 2, LRU_BLOCKS, LRU_BW, LRU_BW), LRU_BW ** -0.5),
        'lru_ba': normal((DEPTH, 2, LRU_W), 0.01),
        'lru_wi': normal((DEPTH, 2, LRU_BLOCKS, LRU_BW, LRU_BW), LRU_BW ** -0.5),
        'lru_bi': normal((DEPTH, 2, LRU_W), 0.01),
        'lru_lambda': lam,
        'ssd_conv_w': normal((DEPTH, CONV_W, SSD_XBC), CONV_W ** -0.5),
        'ssd_conv_b': normal((DEPTH, SSD_XBC), 0.01),
        'ssd_a_log': jnp.log(uniform((DEPTH, 2, SSD_HEADS), 1.0, 16.0)),
        'ssd_dt_bias': dt_bias((DEPTH, 2, SSD_HEADS)),
        'ssd_d': 1.0 + normal((DEPTH, SSD_HEADS), 0.1),
        'ssd_norm_w': gain((DEPTH, SSD_DI)),
        'hgrn_lb': normal((2, DEPTH, HGRN_KW), 0.5),
        'hgrn_norm_w': gain((DEPTH, HGRN_DV)),
        'group_norm_w': gain((DEPTH, N_GROUPS, GROUP_W)),
        'w_out': normal((DEPTH, D_MIX, D_MODEL), D_MIX ** -0.5),
        'ln2': gain((DEPTH, D_MODEL)),
        'w_up': normal((DEPTH, D_MODEL, 2 * D_FF), D_MODEL ** -0.5),
        'ffn_conv_w': normal((DEPTH, FFN_CONV_W, 2 * D_FF), FFN_CONV_W ** -0.5),
        'ffn_conv_b': normal((DEPTH, 2 * D_FF), 0.01),
        'w_down': normal((DEPTH, D_FF, D_MODEL), D_FF ** -0.5),
        'final_norm': gain((D_MODEL,)),
    }


def reference(x_prompt, x_sample, ln1, w_in, gdn_conv_w, gdn_a_log, gdn_dt_bias, gdn_norm_w,
              lru_conv_w, lru_conv_b, lru_wa, lru_ba, lru_wi, lru_bi, lru_lambda,
              ssd_conv_w, ssd_conv_b, ssd_a_log, ssd_dt_bias, ssd_d, ssd_norm_w,
              hgrn_lb, hgrn_norm_w, group_norm_w, w_out, ln2, w_up, ffn_conv_w, ffn_conv_b,
              w_down, final_norm):
    def run(x):
        for l in range(DEPTH):
            h = _rmsnorm(x, ln1[l])
            (gdn_qkv, gdn_z, gdn_beta, gdn_alpha, lru_x, lru_gate, ssd_z, ssd_xbc, ssd_dt,
             hg_q, hg_f, hg_i, hg_g) = _split(h @ w_in[l], IN_SIZES)
            ya = _gdn_mixer(gdn_qkv, gdn_z, gdn_beta, gdn_alpha, gdn_conv_w[l], gdn_a_log[l],
                            gdn_dt_bias[l], gdn_norm_w[l])
            yb = _rglru_mixer(lru_x, lru_gate, lru_conv_w[l], lru_conv_b[l], lru_wa[l], lru_ba[l],
                              lru_wi[l], lru_bi[l], lru_lambda[l])
            yc = _ssd_mixer(ssd_z, ssd_xbc, ssd_dt, ssd_conv_w[l], ssd_conv_b[l], ssd_a_log[l],
                            ssd_dt_bias[l], ssd_d[l], ssd_norm_w[l])
            yd = _hgrn_mixer(hg_q, hg_f, hg_i, hg_g, hgrn_lb, l, hgrn_norm_w[l])
            gn = group_norm_w[l]
            mix = jnp.concatenate([_rmsnorm(ya, gn[0]), _rmsnorm(yb, gn[1]),
                                   _rmsnorm(yc, gn[2]), _rmsnorm(yd, gn[3])], axis=-1)
            x = x + mix @ w_out[l]
            h = _rmsnorm(x, ln2[l])
            up = _dwconv(h @ w_up[l], ffn_conv_w[l], ffn_conv_b[l])
            gate, val = jnp.split(up, 2, axis=-1)
            x = x + (jax.nn.silu(gate) * val) @ w_down[l]
        return _rmsnorm(x, final_norm)

    y_prompt = run(x_prompt)
    y_sample = run(x_sample)
    return (y_prompt, y_sample)
```

```python
import functools
import math

import jax
import jax.numpy as jnp
from jax import lax
from jax.experimental import pallas as pl
from jax.experimental.pallas import tpu as pltpu

F32 = jnp.float32
BF16 = jnp.bfloat16

D_MODEL = 2048
DEPTH = 2
GROUP_W = 512
GDN_HEADS = 4
GDN_DK = 128
LRU_BLOCKS = 4
LRU_BW = 128
LRU_C = 8.0
SSD_HEADS = 8
SSD_HEADDIM = 64
SSD_GROUPS = 2
SSD_STATE = 128
HGRN_HEADS = 4
HGRN_DK = 128
D_FF = 5632
CHUNK = 64
HGRN_CHUNK = 16
EPS = 1e-6

COL_GDN_QKV = 0
COL_GDN_Z = 1536
COL_SSD_XBC = 2048
COL_LRU_X = 3072
COL_LRU_GATE = 3584
COL_SSD_Z = 4096
COL_HG_Q = 4608
COL_HG_F = 5120
COL_HG_I = 6144
COL_HG_G = 6656
COL_SMALL = 7168
D_IN_P = 7296
SM_BETA = 0
SM_ALPHA = 8
SM_DT = 16

SUBLANES = 8
HALO = 8
FHALO = 16
VMEM_LIMIT = 56 * 1024 * 1024


def _sig(x):
    return 1.0 / (1.0 + jnp.exp(-x))


def _silu(x):
    return x * _sig(x)


def _softplus(x):
    return jnp.maximum(x, 0.0) + jnp.log1p(jnp.exp(-jnp.abs(x)))


def _gelu_tanh(x):
    c = math.sqrt(2.0 / math.pi)
    return 0.5 * x * (1.0 + jnp.tanh(c * (x + 0.044715 * (x * x * x))))


def _bf(x):
    return x.astype(BF16)


def _dot(a, b):
    return jnp.dot(a, b, preferred_element_type=F32)


def _dot1(a, b):
    return _dot(_bf(a), _bf(b))


def _dot1_nt(a, b):
    return lax.dot_general(_bf(a), _bf(b), (((1,), (1,)), ((), ())),
                           preferred_element_type=F32)


def _dot1_tn(a, b):
    return lax.dot_general(_bf(a), _bf(b), (((0,), (0,)), ((), ())),
                           preferred_element_type=F32)


def _split(x):
    hi = _bf(x)
    lo = _bf(x - hi.astype(F32))
    return hi, lo


def _dot3(a, b):
    ah, al = _split(a)
    bh, bl = _split(b)
    return _dot(ah, bh) + (_dot(ah, bl) + _dot(al, bh))


def _dot_mask_l(m, b):
    b1 = _bf(b)
    r1 = b - b1.astype(F32)
    b2 = _bf(r1)
    b3 = _bf(r1 - b2.astype(F32))
    return _dot(m, b1) + (_dot(m, b2) + _dot(m, b3))


def _dot_mask_r(a, m):
    a1 = _bf(a)
    r1 = a - a1.astype(F32)
    a2 = _bf(r1)
    a3 = _bf(r1 - a2.astype(F32))
    return _dot(a1, m) + (_dot(a2, m) + _dot(a3, m))


def _rms(x, w):
    return x * lax.rsqrt(jnp.mean(x * x, axis=-1, keepdims=True) + EPS) * w


def _seq_start(row, seq):
    p_rows, s_len, ds_len = seq
    return jnp.where(row < p_rows, lax.rem(row, s_len) == 0,
                     lax.rem(row - p_rows, ds_len) == 0)


def _block_pos(nb, tb, rev, seq):
    i = pl.program_id(0)
    j = (nb - 1 - i) if rev else i
    row0 = j * tb
    first = _seq_start(row0, seq)
    last = _seq_start(row0 + tb, seq)
    return first, last


def _fill_pad(xp_ref, blk_ref, prev_ref, next_ref, first, last, tb):
    keep_p = jnp.where(first, 0.0, 1.0)
    keep_n = jnp.where(last, 0.0, 1.0)
    xp_ref[0:HALO, :] = prev_ref[...] * keep_p
    xp_ref[HALO:HALO + tb, :] = blk_ref[...]
    xp_ref[HALO + tb:2 * HALO + tb, :] = next_ref[...] * keep_n


def _conv4(xp_ref, w_ref, tb):
    acc = xp_ref[HALO - 2:HALO - 2 + tb, :] * w_ref[0:1, :]
    acc = acc + xp_ref[HALO - 1:HALO - 1 + tb, :] * w_ref[1:2, :]
    acc = acc + xp_ref[HALO:HALO + tb, :] * w_ref[2:3, :]
    acc = acc + xp_ref[HALO + 1:HALO + 1 + tb, :] * w_ref[3:4, :]
    return acc


def _iota2(shape):
    return (lax.broadcasted_iota(jnp.int32, shape, 0),
            lax.broadcasted_iota(jnp.int32, shape, 1))


def _tri_masks(c, rev):
    r, cc = _iota2((c, c))
    incl = (cc >= r) if rev else (cc <= r)
    strict = (cc > r) if rev else (cc < r)
    return r, cc, incl, strict


def _unit_tri_inverse(a, r, c):
    n = a.shape[0]
    eye = jnp.where(r == c, 1.0, 0.0)
    d = jnp.where((r >> 3) == (c >> 3), a, 0.0)
    d2 = _dot3(d, d)
    x = eye - d
    x = x + _dot3(x, d2)
    d4 = _dot3(d2, d2)
    x = x + _dot3(x, d4)
    b = 16
    sh = 4
    while b <= n:
        e = jnp.where(((r >> sh) == (c >> sh)) & ((r >> (sh - 1)) != (c >> (sh - 1))), a, 0.0)
        x = x - _dot3(_dot3(x, e), x)
        b *= 2
        sh += 1
    return x


def _inproj_kernel(x_ref, ln_ref, w_ref, o_ref):
    h = _rms(x_ref[...], ln_ref[...])
    o_ref[...] = _dot(_bf(h), w_ref[...])


def _inproj(x, ln, w):
    t, d = x.shape
    n = w.shape[1]
    tm = 256
    tn = n // 3
    return pl.pallas_call(
        _inproj_kernel,
        grid=(n // tn, t // tm),
        in_specs=[pl.BlockSpec((tm, d), lambda j, i: (i, 0)),
                  pl.BlockSpec((1, d), lambda j, i: (0, 0)),
                  pl.BlockSpec((d, tn), lambda j, i: (0, j))],
        out_specs=pl.BlockSpec((tm, tn), lambda j, i: (i, j)),
        out_shape=jax.ShapeDtypeStruct((t, n), F32),
        compiler_params=pltpu.CompilerParams(
            dimension_semantics=("arbitrary", "arbitrary"),
            vmem_limit_bytes=VMEM_LIMIT),
        name="inproj",
    )(x, ln.reshape(1, d), w)


def _bidx(nb, rev):
    if rev:
        return lambda i: nb - 1 - i
    return lambda i: i


def _main_spec(tb, w, col, nb, rev):
    assert col % w == 0
    f = _bidx(nb, rev)
    return pl.BlockSpec((tb, w), lambda i: (f(i), col // w))


def _halo_specs(tb, w, col, nb, rev, t):
    assert col % w == 0
    f = _bidx(nb, rev)
    r = tb // HALO
    prev = pl.BlockSpec((HALO, w), lambda i: (jnp.maximum(f(i) * r - 1, 0), col // w))
    nxt = pl.BlockSpec((HALO, w), lambda i: (jnp.minimum((f(i) + 1) * r, t // HALO - 1), col // w))
    return prev, nxt


def _const_spec(shape):
    nd = len(shape)
    return pl.BlockSpec(shape, lambda i: (0,) * nd)


def _scan_params():
    return pltpu.CompilerParams(dimension_semantics=("arbitrary",),
                                vmem_limit_bytes=VMEM_LIMIT)


def _gdn_kernel(rev, final, tb, nb, seq, *refs):
    if final:
        (qkv_ref, prev_ref, next_ref, small_ref, cw_ref, arow_ref, brow_ref,
         z_ref, ofwd_ref, nw_ref, gn_ref, out_ref, xp_ref, qkv_s, gate_s, st_ref) = refs
    else:
        (qkv_ref, prev_ref, next_ref, small_ref, cw_ref, arow_ref, brow_ref,
         out_ref, xp_ref, qkv_s, gate_s, st_ref) = refs
    d = 1 if rev else 0
    c = CHUNK
    nc = tb // c
    hw = GDN_DK
    kw = GDN_HEADS * GDN_DK
    first, last = _block_pos(nb, tb, rev, seq)

    _fill_pad(xp_ref, qkv_ref, prev_ref, next_ref, first, last, tb)
    y = _silu(_conv4(xp_ref, cw_ref, tb))
    for h in range(GDN_HEADS):
        qh = y[:, h * hw:(h + 1) * hw]
        qh = qh * lax.rsqrt(jnp.sum(qh * qh, axis=-1, keepdims=True) + EPS) * (GDN_DK ** -0.5)
        qkv_s[:, h * hw:(h + 1) * hw] = qh
        kh = y[:, kw + h * hw:kw + (h + 1) * hw]
        kh = kh * lax.rsqrt(jnp.sum(kh * kh, axis=-1, keepdims=True) + EPS)
        qkv_s[:, kw + h * hw:kw + (h + 1) * hw] = kh
    qkv_s[:, 2 * kw:] = y[:, 2 * kw:]
    sm = small_ref[...]
    gate_s[:, 0:128] = _sig(sm)
    gate_s[:, 128:256] = -jnp.exp(arow_ref[...]) * _softplus(sm + brow_ref[...])

    @pl.when(last if rev else first)
    def _():
        st_ref[...] = jnp.zeros_like(st_ref)

    r, cc, incl, strict = _tri_masks(c, rev)
    m_incl = jnp.where(incl, 1.0, 0.0).astype(BF16)
    m_ones = jnp.ones((c, c), BF16)

    def chunk(ci, carry):
        cj = (nc - 1 - ci) if rev else ci
        r0 = pl.multiple_of(cj * c, c)
        beta_a = gate_s[pl.ds(r0, c), 0:128]
        g_a = gate_s[pl.ds(r0, c), 128:256]
        gcum = _dot_mask_l(m_incl, g_a)
        gcum_t = gcum.T
        gtot = _dot_mask_l(m_ones, g_a)
        for h in range(GDN_HEADS):
            cb = SM_BETA + d * GDN_HEADS + h
            cg = SM_ALPHA + d * GDN_HEADS + h
            q = qkv_s[pl.ds(r0, c), h * hw:(h + 1) * hw]
            k = qkv_s[pl.ds(r0, c), kw + h * hw:kw + (h + 1) * hw]
            v = qkv_s[pl.ds(r0, c), 2 * kw + h * hw:2 * kw + (h + 1) * hw]
            bcol = beta_a[:, cb:cb + 1]
            gi = gcum[:, cg:cg + 1]
            gj = gcum_t[cg:cg + 1, :]
            gt = gtot[:, cg:cg + 1]
            decay = jnp.where(incl, jnp.exp(gi - gj), 0.0)
            kb = k * bcol
            a = jnp.where(strict, _dot1_nt(kb, k) * decay, 0.0)
            tm = _unit_tri_inverse(a, r, cc)
            eg = jnp.exp(gi)
            w = _dot3(tm, kb * eg)
            u = _dot3(tm, v * bcol)
            qk = _dot1_nt(q, k) * decay
            s_h = st_ref[h]
            v_new = u - _dot1(w, s_h)
            o = _dot1(q * eg, s_h) + _dot1(qk, v_new)
            kd = k * jnp.exp(gt - gi)
            st_ref[h] = s_h * jnp.exp(gt[0:1, :]) + _dot1_tn(kd, v_new)
            out_ref[pl.ds(r0, c), h * hw:(h + 1) * hw] = o
        return carry

    lax.fori_loop(0, nc, chunk, 0)

    if final:
        o = out_ref[...] + ofwd_ref[...]
        z = z_ref[...]
        for h in range(GDN_HEADS):
            oh = _rms(o[:, h * hw:(h + 1) * hw], nw_ref[...])
            out_ref[:, h * hw:(h + 1) * hw] = oh * _silu(z[:, h * hw:(h + 1) * hw])
        out_ref[...] = _rms(out_ref[...], gn_ref[...])


def _gdn(u, conv_w, a_log, dt_bias, norm_w, gn_w, tb, seq):
    t = u.shape[0]
    nb = t // tb
    w3 = 3 * GROUP_W
    arow = jnp.zeros((1, 128), F32).at[0, SM_ALPHA:SM_ALPHA + 8].set(a_log.reshape(8))
    brow = jnp.zeros((1, 128), F32).at[0, SM_ALPHA:SM_ALPHA + 8].set(dt_bias.reshape(8))
    scratch = [pltpu.VMEM((tb + 2 * HALO, w3), F32), pltpu.VMEM((tb, w3), F32),
               pltpu.VMEM((tb, 256), F32), pltpu.VMEM((GDN_HEADS, GDN_DK, GDN_DK), F32)]

    def call(rev, final, extra_in, extra_specs):
        prev, nxt = _halo_specs(tb, w3, COL_GDN_QKV, nb, rev, t)
        in_specs = [_main_spec(tb, w3, COL_GDN_QKV, nb, rev), prev, nxt,
                    _main_spec(tb, 128, COL_SMALL, nb, rev),
                    _const_spec((4, w3)), _const_spec((1, 128)), _const_spec((1, 128))] + extra_specs
        return pl.pallas_call(
            functools.partial(_gdn_kernel, rev, final, tb, nb, seq),
            grid=(nb,),
            in_specs=in_specs,
            out_specs=_main_spec(tb, GROUP_W, 0, nb, rev),
            out_shape=jax.ShapeDtypeStruct((t, GROUP_W), F32),
            scratch_shapes=scratch,
            compiler_params=_scan_params(),
            name="gdn_bwd" if rev else "gdn_fwd",
        )(u, u, u, u, conv_w, arow, brow, *extra_in)

    o_fwd = call(False, False, [], [])
    return call(True, True,
                [u, o_fwd, norm_w.reshape(1, -1), gn_w.reshape(1, -1)],
                [_main_spec(tb, GROUP_W, COL_GDN_Z, nb, True),
                 _main_spec(tb, GROUP_W, 0, nb, True),
                 _const_spec((1, GDN_DK)), _const_spec((1, GROUP_W))])


def _lru_kernel(rev, final, tb, nb, seq, *refs):
    if final:
        (x_ref, prev_ref, next_ref, cw_ref, cb_ref, w_ref, ba_ref, bi_ref, lam_ref,
         gate_ref, ofwd_ref, gn_ref, out_ref, xp_ref, h_ref) = refs
    else:
        (x_ref, prev_ref, next_ref, cw_ref, cb_ref, w_ref, ba_ref, bi_ref, lam_ref,
         out_ref, xp_ref, h_ref) = refs
    first, last = _block_pos(nb, tb, rev, seq)
    start = last if rev else first
    bw = LRU_BW

    _fill_pad(xp_ref, x_ref, prev_ref, next_ref, first, last, tb)
    xc = _conv4(xp_ref, cw_ref, tb) + cb_ref[...]

    @pl.when(start)
    def _():
        h_ref[...] = jnp.zeros_like(h_ref)

    rows = lax.broadcasted_iota(jnp.int32, (tb, bw), 0)
    start_row = (tb - 1) if rev else 0
    end_row = 0 if rev else (tb - 1)
    is_start_row = (rows == start_row) & start
    sp = _softplus(-lam_ref[...])
    for n in range(LRU_BLOCKS):
        xb = xc[:, n * bw:(n + 1) * bw]
        pre = _dot1(xb, w_ref[n])
        rg = _sig(pre[:, :bw] + ba_ref[:, n * bw:(n + 1) * bw])
        ig = _sig(pre[:, bw:] + bi_ref[:, n * bw:(n + 1) * bw])
        log_a = (-LRU_C) * rg * sp[:, n * bw:(n + 1) * bw]
        a = jnp.exp(log_a)
        mult = jnp.sqrt((1.0 + a * a) * jnp.tanh(-log_a))
        mult = jnp.where(is_start_row, 1.0, mult)
        b = mult * (ig * xb)
        s = 1
        while s < tb:
            if rev:
                a_sh = pltpu.roll(a, tb - s, 0)
                b_sh = pltpu.roll(b, tb - s, 0)
                ok = rows < tb - s
            else:
                a_sh = pltpu.roll(a, s, 0)
                b_sh = pltpu.roll(b, s, 0)
                ok = rows >= s
            b = jnp.where(ok, a * b_sh, 0.0) + b
            a = jnp.where(ok, a * a_sh, a)
            s *= 2
        hv = a * h_ref[:, n * bw:(n + 1) * bw] + b
        out_ref[:, n * bw:(n + 1) * bw] = hv
        h_ref[:, n * bw:(n + 1) * bw] = hv[end_row:end_row + 1, :]

    if final:
        o = (out_ref[...] + ofwd_ref[...]) * _gelu_tanh(gate_ref[...])
        out_ref[...] = _rms(o, gn_ref[...])


def _lru(u, conv_w, conv_b, wa, ba, wi, bi, lam, gn_w, tb, seq):
    t = u.shape[0]
    nb = t // tb
    w = GROUP_W
    wcat = jnp.concatenate([wa, wi], axis=-1).astype(BF16)
    scratch = [pltpu.VMEM((tb + 2 * HALO, w), F32), pltpu.VMEM((1, w), F32)]

    def call(rev, final, extra_in, extra_specs):
        d = 1 if rev else 0
        prev, nxt = _halo_specs(tb, w, COL_LRU_X, nb, rev, t)
        in_specs = [_main_spec(tb, w, COL_LRU_X, nb, rev), prev, nxt,
                    _const_spec((4, w)), _const_spec((1, w)),
                    _const_spec((LRU_BLOCKS, LRU_BW, 2 * LRU_BW)),
                    _const_spec((1, w)), _const_spec((1, w)), _const_spec((1, w))] + extra_specs
        return pl.pallas_call(
            functools.partial(_lru_kernel, rev, final, tb, nb, seq),
            grid=(nb,),
            in_specs=in_specs,
            out_specs=_main_spec(tb, w, 0, nb, rev),
            out_shape=jax.ShapeDtypeStruct((t, w), F32),
            scratch_shapes=scratch,
            compiler_params=_scan_params(),
            name="lru_bwd" if rev else "lru_fwd",
        )(u, u, u, conv_w, conv_b.reshape(1, w), wcat[d], ba[d].reshape(1, w),
          bi[d].reshape(1, w), lam[d].reshape(1, w), *extra_in)

    o_fwd = call(False, False, [], [])
    return call(True, True, [u, o_fwd, gn_w.reshape(1, w)],
                [_main_spec(tb, w, COL_LRU_GATE, nb, True), _main_spec(tb, w, 0, nb, True),
                 _const_spec((1, w))])


def _ssd_kernel(rev, final, tb, nb, seq, *refs):
    if final:
        (xbc_ref, prev_ref, next_ref, small_ref, cw_ref, cb_ref, arow_ref, brow_ref,
         z_ref, ofwd_ref, dskip_ref, nw_ref, gn_ref, out_ref, xp_ref, xbc_s, dt_s, st_ref) = refs
    else:
        (xbc_ref, prev_ref, next_ref, small_ref, cw_ref, cb_ref, arow_ref, brow_ref,
         out_ref, xp_ref, xbc_s, dt_s, st_ref) = refs
    d = 1 if rev else 0
    c = CHUNK
    nc = tb // c
    di = GROUP_W
    p = SSD_HEADDIM
    gw = di // SSD_GROUPS
    first, last = _block_pos(nb, tb, rev, seq)

    _fill_pad(xp_ref, xbc_ref, prev_ref, next_ref, first, last, tb)
    xbc_s[...] = _silu(_conv4(xp_ref, cw_ref, tb) + cb_ref[...])
    dt = _softplus(small_ref[...] + brow_ref[...])
    dt_s[:, 0:128] = dt
    dt_s[:, 128:256] = dt * (-jnp.exp(arow_ref[...]))

    @pl.when(last if rev else first)
    def _():
        st_ref[...] = jnp.zeros_like(st_ref)

    r, cc, incl, _ = _tri_masks(c, rev)
    m_incl = jnp.where(incl, 1.0, 0.0).astype(BF16)
    m_ones = jnp.ones((c, c), BF16)
    sr, sc = _iota2((128, di))
    sel = jnp.where(sr == SM_DT + d * SSD_HEADS + (sc >> 6), 1.0, 0.0).astype(BF16)

    def chunk(ci, carry):
        cj = (nc - 1 - ci) if rev else ci
        r0 = pl.multiple_of(cj * c, c)
        dt_a = dt_s[pl.ds(r0, c), 0:128]
        da_a = dt_s[pl.ds(r0, c), 128:256]
        acs = _dot_mask_l(m_incl, da_a)
        acs_t = acs.T
        tot = _dot_mask_l(m_ones, da_a)
        acs_x = _dot_mask_r(acs, sel)
        tot_x = _dot_mask_r(tot, sel)
        dt_x = _dot_mask_r(dt_a, sel)
        xs = xbc_s[pl.ds(r0, c), 0:di]
        xdt = xs * dt_x
        xdt_e = xdt * jnp.exp(tot_x - acs_x)
        e_acs = jnp.exp(acs_x)
        e_tot = jnp.exp(tot_x[0:1, :])
        for g in range(SSD_GROUPS):
            bm = xbc_s[pl.ds(r0, c), di + g * SSD_STATE:di + (g + 1) * SSD_STATE]
            cm = xbc_s[pl.ds(r0, c), di + (SSD_GROUPS + g) * SSD_STATE:di + (SSD_GROUPS + g + 1) * SSD_STATE]
            cbm = _dot1_nt(cm, bm)
            s_g = st_ref[g]
            y_off = _dot1(cm, s_g) * e_acs[:, g * gw:(g + 1) * gw]
            st_ref[g] = s_g * e_tot[:, g * gw:(g + 1) * gw] + _dot1_tn(bm, xdt_e[:, g * gw:(g + 1) * gw])
            for hh in range(SSD_HEADS // SSD_GROUPS):
                h = g * (SSD_HEADS // SSD_GROUPS) + hh
                col = SM_DT + d * SSD_HEADS + h
                ai = acs[:, col:col + 1]
                aj = acs_t[col:col + 1, :]
                lmat = jnp.where(incl, jnp.exp(ai - aj), 0.0)
                y_d = _dot1(cbm * lmat, xdt[:, h * p:(h + 1) * p])
                out_ref[pl.ds(r0, c), h * p:(h + 1) * p] = y_d + y_off[:, hh * p:(hh + 1) * p]
        return carry

    lax.fori_loop(0, nc, chunk, 0)

    if final:
        y = out_ref[...] + ofwd_ref[...] + dskip_ref[...] * xbc_s[:, 0:di]
        y = y * _silu(z_ref[...])
        for g in range(SSD_GROUPS):
            out_ref[:, g * gw:(g + 1) * gw] = _rms(y[:, g * gw:(g + 1) * gw], nw_ref[:, g * gw:(g + 1) * gw])
        out_ref[...] = _rms(out_ref[...], gn_ref[...])


def _ssd(u, conv_w, conv_b, a_log, dt_bias, d_skip, norm_w, gn_w, tb, seq):
    t = u.shape[0]
    nb = t // tb
    wx = GROUP_W + 2 * SSD_GROUPS * SSD_STATE
    arow = jnp.zeros((1, 128), F32).at[0, SM_DT:SM_DT + 16].set(a_log.reshape(16))
    brow = jnp.zeros((1, 128), F32).at[0, SM_DT:SM_DT + 16].set(dt_bias.reshape(16))
    dsk = jnp.repeat(d_skip, SSD_HEADDIM).reshape(1, GROUP_W)
    scratch = [pltpu.VMEM((tb + 2 * HALO, wx), F32), pltpu.VMEM((tb, wx), F32),
               pltpu.VMEM((tb, 256), F32),
               pltpu.VMEM((SSD_GROUPS, SSD_STATE, GROUP_W // SSD_GROUPS), F32)]

    def call(rev, final, extra_in, extra_specs):
        prev, nxt = _halo_specs(tb, wx, COL_SSD_XBC, nb, rev, t)
        in_specs = [_main_spec(tb, wx, COL_SSD_XBC, nb, rev), prev, nxt,
                    _main_spec(tb, 128, COL_SMALL, nb, rev),
                    _const_spec((4, wx)), _const_spec((1, wx)),
                    _const_spec((1, 128)), _const_spec((1, 128))] + extra_specs
        return pl.pallas_call(
            functools.partial(_ssd_kernel, rev, final, tb, nb, seq),
            grid=(nb,),
            in_specs=in_specs,
            out_specs=_main_spec(tb, GROUP_W, 0, nb, rev),
            out_shape=jax.ShapeDtypeStruct((t, GROUP_W), F32),
            scratch_shapes=scratch,
            compiler_params=_scan_params(),
            name="ssd_bwd" if rev else "ssd_fwd",
        )(u, u, u, u, conv_w, conv_b.reshape(1, wx), arow, brow, *extra_in)

    o_fwd = call(False, False, [], [])
    w = GROUP_W
    return call(True, True,
                [u, o_fwd, dsk, norm_w.reshape(1, w), gn_w.reshape(1, w)],
                [_main_spec(tb, w, COL_SSD_Z, nb, True), _main_spec(tb, w, 0, nb, True),
                 _const_spec((1, w)), _const_spec((1, w)), _const_spec((1, w))])


def _hgrn_kernel(rev, final, layer, tb, nb, seq, *refs):
    if final:
        (q_ref, f_ref, i_ref, lb_ref, g_ref, ofwd_ref, nw_ref, gn_ref, out_ref,
         k_s, b_s, v_s, qd_s, kd_s, gl_s, st_ref) = refs
    else:
        (q_ref, f_ref, i_ref, lb_ref, out_ref,
         k_s, b_s, v_s, qd_s, kd_s, gl_s, st_ref) = refs
    c = HGRN_CHUNK
    nc = tb // c
    w = GROUP_W
    hw = HGRN_DK
    pad = c
    first, last = _block_pos(nb, tb, rev, seq)

    lbp = lb_ref[...]
    mx = jnp.max(lbp, axis=0, keepdims=True)
    ex = jnp.exp(lbp - mx)
    den = jnp.sum(ex, axis=0, keepdims=True)
    lb = jnp.zeros((1, w), F32)
    for m in range(1, layer + 1):
        lb = lb + ex[m:m + 1, :] / den

    f = lb + (1.0 - lb) * _sig(f_ref[...])
    logf = jnp.log(f)
    kk = 1.0 - f
    qq = _silu(q_ref[...])
    vv = i_ref[...]

    r, cc = _iota2((tb, tb))
    same = (r >> 4) == (cc >> 4)
    incl = same & ((cc >= r) if rev else (cc <= r))
    m_incl = jnp.where(incl, 1.0, 0.0).astype(BF16)
    m_same = jnp.where(same, 1.0, 0.0).astype(BF16)
    b = _dot_mask_l(m_incl, logf)
    btot = _dot_mask_l(m_same, logf)

    zpad = jnp.zeros((pad, w), F32)
    for ref, val in ((k_s, kk), (b_s, b), (v_s, vv)):
        ref[0:pad, :] = zpad
        ref[pad:pad + tb, :] = val
        ref[pad + tb:2 * pad + tb, :] = zpad
    qd_s[...] = qq * jnp.exp(b)
    kd_s[...] = kk * jnp.exp(btot - b)
    gl_s[...] = jnp.exp(btot)

    pos = lax.broadcasted_iota(jnp.int32, (tb, 1), 0) & (c - 1)
    y = [jnp.zeros((tb, hw), F32) for _ in range(HGRN_HEADS)]
    for dl in range(c):
        off = pad + dl if rev else pad - dl
        ks = k_s[off:off + tb, :]
        bs = b_s[off:off + tb, :]
        vs = v_s[off:off + tb, :]
        ok = (pos <= c - 1 - dl) if rev else (pos >= dl)
        pr = qq * ks * jnp.exp(b - bs)
        for h in range(HGRN_HEADS):
            sc = jnp.sum(pr[:, h * hw:(h + 1) * hw], axis=-1, keepdims=True)
            sc = jnp.where(ok, sc, 0.0)
            y[h] = y[h] + sc * vs[:, h * hw:(h + 1) * hw]
    for h in range(HGRN_HEADS):
        out_ref[:, h * hw:(h + 1) * hw] = y[h]

    @pl.when(last if rev else first)
    def _():
        st_ref[...] = jnp.zeros_like(st_ref)

    def chunk(ci, carry):
        cj = (nc - 1 - ci) if rev else ci
        r0 = pl.multiple_of(cj * c, c)
        for h in range(HGRN_HEADS):
            sl = slice(h * hw, (h + 1) * hw)
            s_h = st_ref[h]
            qd = qd_s[pl.ds(r0, c), sl]
            kd = kd_s[pl.ds(r0, c), sl]
            vc = v_s[pl.ds(pad + r0, c), sl]
            gl = gl_s[pl.ds(r0, 1), sl]
            out_ref[pl.ds(r0, c), sl] += _dot1_nt(qd, s_h)
            st_ref[h] = s_h * gl + _dot1_tn(vc, kd)
        return carry

    lax.fori_loop(0, nc, chunk, 0)

    if final:
        o = out_ref[...] + ofwd_ref[...]
        gg = g_ref[...]
        for h in range(HGRN_HEADS):
            sl = slice(h * hw, (h + 1) * hw)
            out_ref[:, sl] = _rms(o[:, sl], nw_ref[...]) * _silu(gg[:, sl])
        out_ref[...] = _rms(out_ref[...], gn_ref[...])


def _hgrn(u, lb_param, layer, norm_w, gn_w, tb, seq):
    t = u.shape[0]
    nb = t // tb
    w = GROUP_W
    pad = HGRN_CHUNK
    scratch = [pltpu.VMEM((tb + 2 * pad, w), F32), pltpu.VMEM((tb + 2 * pad, w), F32),
               pltpu.VMEM((tb + 2 * pad, w), F32), pltpu.VMEM((tb, w), F32),
               pltpu.VMEM((tb, w), F32), pltpu.VMEM((tb, w), F32),
               pltpu.VMEM((HGRN_HEADS, HGRN_DK, HGRN_DK), F32)]

    def call(rev, final, extra_in, extra_specs):
        d = 1 if rev else 0
        in_specs = [_main_spec(tb, w, COL_HG_Q, nb, rev),
                    _main_spec(tb, w, COL_HG_F + d * w, nb, rev),
                    _main_spec(tb, w, COL_HG_I, nb, rev),
                    _const_spec((DEPTH, w))] + extra_specs
        return pl.pallas_call(
            functools.partial(_hgrn_kernel, rev, final, layer, tb, nb, seq),
            grid=(nb,),
            in_specs=in_specs,
            out_specs=_main_spec(tb, w, 0, nb, rev),
            out_shape=jax.ShapeDtypeStruct((t, w), F32),
            scratch_shapes=scratch,
            compiler_params=_scan_params(),
            name="hgrn_bwd" if rev else "hgrn_fwd",
        )(u, u, u, lb_param[d], *extra_in)

    o_fwd = call(False, False, [], [])
    return call(True, True, [u, o_fwd, norm_w.reshape(1, -1), gn_w.reshape(1, w)],
                [_main_spec(tb, w, COL_HG_G, nb, True), _main_spec(tb, w, 0, nb, True),
                 _const_spec((1, HGRN_DK)), _const_spec((1, w))])


def _outproj_kernel(x_ref, ya_ref, yb_ref, yc_ref, yd_ref, w_ref, o_ref):
    acc = x_ref[...]
    for n, y_ref in enumerate((ya_ref, yb_ref, yc_ref, yd_ref)):
        acc = acc + _dot(_bf(y_ref[...]), w_ref[n * GROUP_W:(n + 1) * GROUP_W, :])
    o_ref[...] = acc


def _outproj(x, ys, w):
    t, d = x.shape
    tm = 512
    yspec = pl.BlockSpec((tm, GROUP_W), lambda i: (i, 0))
    return pl.pallas_call(
        _outproj_kernel,
        grid=(t // tm,),
        in_specs=[pl.BlockSpec((tm, d), lambda i: (i, 0)), yspec, yspec, yspec, yspec,
                  pl.BlockSpec(w.shape, lambda i: (0, 0))],
        out_specs=pl.BlockSpec((tm, d), lambda i: (i, 0)),
        out_shape=jax.ShapeDtypeStruct((t, d), F32),
        compiler_params=pltpu.CompilerParams(dimension_semantics=("arbitrary",),
                                             vmem_limit_bytes=VMEM_LIMIT),
        name="outproj",
    )(x, *ys, w)


def _ffn_kernel(apply_final, tm, nrb, nf, seq, *refs):
    (x_ref, prev_ref, next_ref, ln_ref, wg_ref, wv_ref, cwg_ref, cwv_ref, cbg_ref, cbv_ref,
     wd_ref, fn_ref, o_ref, h_s, g_s, v_s, acc_s) = refs
    j = pl.program_id(1)
    i = pl.program_id(0)
    row0 = i * tm
    first = _seq_start(row0, seq)
    last = _seq_start(row0 + tm, seq)

    @pl.when(j == 0)
    def _():
        keep_p = jnp.where(first, 0.0, 1.0)
        keep_n = jnp.where(last, 0.0, 1.0)
        h_s[0:FHALO, :] = _bf(_rms(prev_ref[...], ln_ref[...]) * keep_p)
        h_s[FHALO:FHALO + tm, :] = _bf(_rms(x_ref[...], ln_ref[...]))
        h_s[FHALO + tm:2 * FHALO + tm, :] = _bf(_rms(next_ref[...], ln_ref[...]) * keep_n)
        acc_s[...] = jnp.zeros_like(acc_s)

    hb = h_s[...]
    g_s[...] = _dot(hb, wg_ref[...])
    v_s[...] = _dot(hb, wv_ref[...])

    def conv3(s_ref, cw_ref, cb_ref):
        acc = s_ref[FHALO - 1:FHALO - 1 + tm, :] * cw_ref[0:1, :]
        acc = acc + s_ref[FHALO:FHALO + tm, :] * cw_ref[1:2, :]
        acc = acc + s_ref[FHALO + 1:FHALO + 1 + tm, :] * cw_ref[2:3, :]
        return acc + cb_ref[...]

    act = _silu(conv3(g_s, cwg_ref, cbg_ref)) * conv3(v_s, cwv_ref, cbv_ref)
    acc_s[...] += _dot(_bf(act), wd_ref[...])

    @pl.when(j == nf - 1)
    def _():
        y = x_ref[...] + acc_s[...]
        if apply_final:
            y = _rms(y, fn_ref[...])
        o_ref[...] = y


def _ffn(x, ln, w_up, conv_w, conv_b, w_down, final_w, apply_final, seq):
    t, d = x.shape
    tm = 512
    tf = 512
    nf = D_FF // tf
    nrb = t // tm
    r = tm // FHALO
    cb = conv_b.reshape(1, 2 * D_FF)
    return pl.pallas_call(
        functools.partial(_ffn_kernel, apply_final, tm, nrb, nf, seq),
        grid=(nrb, nf),
        in_specs=[pl.BlockSpec((tm, d), lambda i, j: (i, 0)),
                  pl.BlockSpec((FHALO, d), lambda i, j: (jnp.maximum(i * r - 1, 0), 0)),
                  pl.BlockSpec((FHALO, d), lambda i, j: (jnp.minimum((i + 1) * r, t // FHALO - 1), 0)),
                  pl.BlockSpec((1, d), lambda i, j: (0, 0)),
                  pl.BlockSpec((d, tf), lambda i, j: (0, j)),
                  pl.BlockSpec((d, tf), lambda i, j: (0, j + nf)),
                  pl.BlockSpec((3, tf), lambda i, j: (0, j)),
                  pl.BlockSpec((3, tf), lambda i, j: (0, j + nf)),
                  pl.BlockSpec((1, tf), lambda i, j: (0, j)),
                  pl.BlockSpec((1, tf), lambda i, j: (0, j + nf)),
                  pl.BlockSpec((tf, d), lambda i, j: (j, 0)),
                  pl.BlockSpec((1, d), lambda i, j: (0, 0))],
        out_specs=pl.BlockSpec((tm, d), lambda i, j: (i, 0)),
        out_shape=jax.ShapeDtypeStruct((t, d), F32),
        scratch_shapes=[pltpu.VMEM((tm + 2 * FHALO, d), BF16),
                        pltpu.VMEM((tm + 2 * FHALO, tf), F32),
                        pltpu.VMEM((tm + 2 * FHALO, tf), F32),
                        pltpu.VMEM((tm, d), F32)],
        compiler_params=pltpu.CompilerParams(dimension_semantics=("arbitrary", "arbitrary"),
                                             vmem_limit_bytes=VMEM_LIMIT),
        name="ffn",
    )(x, x, x, ln.reshape(1, d), w_up, w_up, conv_w, conv_w, cb, cb, w_down, final_w.reshape(1, d))


def _permute_w_in(w):
    sizes = (1536, 512, 8, 8, 512, 512, 512, 1024, 16, 512, 1024, 512, 512)
    offs = [0]
    for s in sizes:
        offs.append(offs[-1] + s)
    (qkv, z, beta, alpha, lx, lg, sz, sxbc, sdt, hq, hf, hi, hg) = [
        w[:, offs[n]:offs[n + 1]] for n in range(len(sizes))]
    padw = jnp.zeros((w.shape[0], D_IN_P - COL_SMALL - 32), w.dtype)
    return jnp.concatenate([qkv, z, sxbc, lx, lg, sz, hq, hf, hi, hg, beta, alpha, sdt, padw], axis=1)


def kernel(x_prompt, x_sample, ln1, w_in, gdn_conv_w, gdn_a_log, gdn_dt_bias, gdn_norm_w, lru_conv_w, lru_conv_b, lru_wa, lru_ba, lru_wi, lru_bi, lru_lambda, ssd_conv_w, ssd_conv_b, ssd_a_log, ssd_dt_bias, ssd_d, ssd_norm_w, hgrn_lb, hgrn_norm_w, group_norm_w, w_out, ln2, w_up, ffn_conv_w, ffn_conv_b, w_down, final_norm):
    b, s, dm = x_prompt.shape
    db, ds, _ = x_sample.shape
    seq = (b * s, s, ds)
    tb = 256
    x = jnp.concatenate([x_prompt.reshape(b * s, dm), x_sample.reshape(db * ds, dm)], axis=0)
    depth = w_in.shape[0]
    for l in range(depth):
        u = _inproj(x, ln1[l], _permute_w_in(w_in[l]).astype(BF16))
        gn = group_norm_w[l]
        ya = _gdn(u, gdn_conv_w[l], gdn_a_log[l], gdn_dt_bias[l], gdn_norm_w[l], gn[0], tb, seq)
        yb = _lru(u, lru_conv_w[l], lru_conv_b[l], lru_wa[l], lru_ba[l], lru_wi[l], lru_bi[l],
                  lru_lambda[l], gn[1], tb, seq)
        yc = _ssd(u, ssd_conv_w[l], ssd_conv_b[l], ssd_a_log[l], ssd_dt_bias[l], ssd_d[l],
                  ssd_norm_w[l], gn[2], tb, seq)
        yd = _hgrn(u, hgrn_lb, l, hgrn_norm_w[l], gn[3], tb, seq)
        x = _outproj(x, (ya, yb, yc, yd), w_out[l].astype(BF16))
        x = _ffn(x, ln2[l], w_up[l].astype(BF16), ffn_conv_w[l], ffn_conv_b[l],
                 w_down[l].astype(BF16), final_norm, l == depth - 1, seq)
    return (x[:b * s].reshape(b, s, dm), x[b * s:].reshape(db, ds, dm))
```

```python
import functools
import math

import jax
import jax.numpy as jnp
from jax import lax
from jax.experimental import pallas as pl
from jax.experimental.pallas import tpu as pltpu

F32 = jnp.float32
BF16 = jnp.bfloat16

D_MODEL = 2048
DEPTH = 2
GROUP_W = 512
GDN_HEADS = 4
GDN_DK = 128
LRU_BLOCKS = 4
LRU_BW = 128
LRU_C = 8.0
SSD_HEADS = 8
SSD_HEADDIM = 64
SSD_GROUPS = 2
SSD_STATE = 128
HGRN_HEADS = 4
HGRN_DK = 128
D_FF = 5632
CHUNK = 64
HGRN_CHUNK = 16
EPS = 1e-6

COL_GDN_QKV = 0
COL_GDN_Z = 1536
COL_SSD_XBC = 2048
COL_LRU_X = 3072
COL_LRU_GATE = 3584
COL_SSD_Z = 4096
COL_HG_Q = 4608
COL_HG_F = 5120
COL_HG_I = 6144
COL_HG_G = 6656
COL_SMALL = 7168
D_IN_P = 7296
SM_BETA = 0
SM_ALPHA = 8
SM_DT = 16

SUBLANES = 8
HALO = 8
FHALO = 16
VMEM_LIMIT = 56 * 1024 * 1024


def _sig(x):
    return 1.0 / (1.0 + jnp.exp(-x))


def _silu(x):
    return x * _sig(x)


def _softplus(x):
    return jnp.maximum(x, 0.0) + jnp.log1p(jnp.exp(-jnp.abs(x)))


def _gelu_tanh(x):
    c = math.sqrt(2.0 / math.pi)
    return 0.5 * x * (1.0 + jnp.tanh(c * (x + 0.044715 * (x * x * x))))


def _bf(x):
    return x.astype(BF16)


def _dot(a, b):
    return jnp.dot(a, b, preferred_element_type=F32)


def _dot1(a, b):
    return _dot(_bf(a), _bf(b))


def _dot1_nt(a, b):
    return lax.dot_general(_bf(a), _bf(b), (((1,), (1,)), ((), ())),
                           preferred_element_type=F32)


def _dot1_tn(a, b):
    return lax.dot_general(_bf(a), _bf(b), (((0,), (0,)), ((), ())),
                           preferred_element_type=F32)


def _split(x):
    hi = _bf(x)
    lo = _bf(x - hi.astype(F32))
    return hi, lo


def _dot3(a, b):
    ah, al = _split(a)
    bh, bl = _split(b)
    return _dot(ah, bh) + (_dot(ah, bl) + _dot(al, bh))


def _dot_mask_l(m, b):
    b1 = _bf(b)
    r1 = b - b1.astype(F32)
    b2 = _bf(r1)
    b3 = _bf(r1 - b2.astype(F32))
    return _dot(m, b1) + (_dot(m, b2) + _dot(m, b3))


def _dot_mask_r(a, m):
    a1 = _bf(a)
    r1 = a - a1.astype(F32)
    a2 = _bf(r1)
    a3 = _bf(r1 - a2.astype(F32))
    return _dot(a1, m) + (_dot(a2, m) + _dot(a3, m))


def _rms(x, w):
    return x * lax.rsqrt(jnp.mean(x * x, axis=-1, keepdims=True) + EPS) * w


def _seq_start(row, seq):
    p_rows, s_len, ds_len = seq
    return jnp.where(row < p_rows, lax.rem(row, s_len) == 0,
                     lax.rem(row - p_rows, ds_len) == 0)


def _block_pos(nb, tb, rev, seq):
    i = pl.program_id(0)
    j = (nb - 1 - i) if rev else i
    row0 = j * tb
    first = _seq_start(row0, seq)
    last = _seq_start(row0 + tb, seq)
    return first, last


def _fill_pad(xp_ref, blk_ref, prev_ref, next_ref, first, last, tb):
    keep_p = jnp.where(first, 0.0, 1.0)
    keep_n = jnp.where(last, 0.0, 1.0)
    xp_ref[0:HALO, :] = prev_ref[...] * keep_p
    xp_ref[HALO:HALO + tb, :] = blk_ref[...]
    xp_ref[HALO + tb:2 * HALO + tb, :] = next_ref[...] * keep_n


def _conv4(xp_ref, w_ref, tb):
    acc = xp_ref[HALO - 2:HALO - 2 + tb, :] * w_ref[0:1, :]
    acc = acc + xp_ref[HALO - 1:HALO - 1 + tb, :] * w_ref[1:2, :]
    acc = acc + xp_ref[HALO:HALO + tb, :] * w_ref[2:3, :]
    acc = acc + xp_ref[HALO + 1:HALO + 1 + tb, :] * w_ref[3:4, :]
    return acc


def _iota2(shape):
    return (lax.broadcasted_iota(jnp.int32, shape, 0),
            lax.broadcasted_iota(jnp.int32, shape, 1))


def _tri_masks(c, rev):
    r, cc = _iota2((c, c))
    incl = (cc >= r) if rev else (cc <= r)
    strict = (cc > r) if rev else (cc < r)
    return r, cc, incl, strict


def _bdot(a, b):
    return lax.dot_general(a, b, (((2,), (1,)), ((0,), (0,))), preferred_element_type=F32)


def _bdot1_nt(a, b):
    return lax.dot_general(_bf(a), _bf(b), (((2,), (2,)), ((0,), (0,))),
                           preferred_element_type=F32)


def _bdot3(a, b):
    ah, al = _split(a)
    bh, bl = _split(b)
    return _bdot(ah, bh) + (_bdot(ah, bl) + _bdot(al, bh))


def _unit_tri_inverse(a, r, c):
    n = a.shape[-1]
    eye = jnp.where(r == c, 1.0, 0.0)[None]
    d = jnp.where(((r >> 3) == (c >> 3))[None], a, 0.0)
    d2 = _bdot3(d, d)
    x = eye - d
    x = x + _bdot3(x, d2)
    d4 = _bdot3(d2, d2)
    x = x + _bdot3(x, d4)
    b = 16
    sh = 4
    while b <= n:
        joined = ((r >> sh) == (c >> sh)) & ((r >> (sh - 1)) != (c >> (sh - 1)))
        e = jnp.where(joined[None], a, 0.0)
        x = x - _bdot3(_bdot3(x, e), x)
        b *= 2
        sh += 1
    return x


def _inproj_kernel(x_ref, ln_ref, w_ref, o_ref):
    h = _rms(x_ref[...], ln_ref[...])
    o_ref[...] = _dot(_bf(h), w_ref[...])


def _inproj(x, ln, w):
    t, d = x.shape
    n = w.shape[1]
    tm = 256
    tn = n // 3
    return pl.pallas_call(
        _inproj_kernel,
        grid=(n // tn, t // tm),
        in_specs=[pl.BlockSpec((tm, d), lambda j, i: (i, 0)),
                  pl.BlockSpec((1, d), lambda j, i: (0, 0)),
                  pl.BlockSpec((d, tn), lambda j, i: (0, j))],
        out_specs=pl.BlockSpec((tm, tn), lambda j, i: (i, j)),
        out_shape=jax.ShapeDtypeStruct((t, n), F32),
        compiler_params=pltpu.CompilerParams(
            dimension_semantics=("arbitrary", "arbitrary"),
            vmem_limit_bytes=VMEM_LIMIT),
        name="inproj",
    )(x, ln.reshape(1, d), w)


def _bidx(nb, rev):
    if rev:
        return lambda i: nb - 1 - i
    return lambda i: i


def _main_spec(tb, w, col, nb, rev):
    assert col % w == 0
    f = _bidx(nb, rev)
    return pl.BlockSpec((tb, w), lambda i: (f(i), col // w))


def _halo_specs(tb, w, col, nb, rev, t):
    assert col % w == 0
    f = _bidx(nb, rev)
    r = tb // HALO
    prev = pl.BlockSpec((HALO, w), lambda i: (jnp.maximum(f(i) * r - 1, 0), col // w))
    nxt = pl.BlockSpec((HALO, w), lambda i: (jnp.minimum((f(i) + 1) * r, t // HALO - 1), col // w))
    return prev, nxt


def _const_spec(shape):
    nd = len(shape)
    return pl.BlockSpec(shape, lambda i: (0,) * nd)


def _scan_params():
    return pltpu.CompilerParams(dimension_semantics=("arbitrary",),
                                vmem_limit_bytes=VMEM_LIMIT)


def _gdn_kernel(rev, final, tb, nb, seq, *refs):
    if final:
        (qkv_ref, prev_ref, next_ref, small_ref, cw_ref, arow_ref, brow_ref,
         z_ref, ofwd_ref, nw_ref, gn_ref, out_ref,
         xp_ref, q3, k3, v3, bet3, gi3, gt3, gj3, w3, u3, qk3, st_ref) = refs
    else:
        (qkv_ref, prev_ref, next_ref, small_ref, cw_ref, arow_ref, brow_ref, out_ref,
         xp_ref, q3, k3, v3, bet3, gi3, gt3, gj3, w3, u3, qk3, st_ref) = refs
    d = 1 if rev else 0
    c = CHUNK
    nc = tb // c
    nh = GDN_HEADS
    hw = GDN_DK
    kw = GDN_HEADS * GDN_DK
    first, last = _block_pos(nb, tb, rev, seq)

    _fill_pad(xp_ref, qkv_ref, prev_ref, next_ref, first, last, tb)
    y = _silu(_conv4(xp_ref, cw_ref, tb))
    for h in range(nh):
        qh = y[:, h * hw:(h + 1) * hw]
        qh = qh * lax.rsqrt(jnp.sum(qh * qh, axis=-1, keepdims=True) + EPS) * (GDN_DK ** -0.5)
        kh = y[:, kw + h * hw:kw + (h + 1) * hw]
        kh = kh * lax.rsqrt(jnp.sum(kh * kh, axis=-1, keepdims=True) + EPS)
        vh = y[:, 2 * kw + h * hw:2 * kw + (h + 1) * hw]
        for ci in range(nc):
            q3[ci * nh + h] = qh[ci * c:(ci + 1) * c, :]
            k3[ci * nh + h] = kh[ci * c:(ci + 1) * c, :]
            v3[ci * nh + h] = vh[ci * c:(ci + 1) * c, :]

    sm = small_ref[...]
    beta_all = _sig(sm)
    g_all = -jnp.exp(arow_ref[...]) * _softplus(sm + brow_ref[...])
    rb, cb_ = _iota2((tb, tb))
    same = (rb >> 6) == (cb_ >> 6)
    incl_b = same & ((cb_ >= rb) if rev else (cb_ <= rb))
    gcum = _dot_mask_l(jnp.where(incl_b, 1.0, 0.0).astype(BF16), g_all)
    gtot = _dot_mask_l(jnp.where(same, 1.0, 0.0).astype(BF16), g_all)
    gcum_t = gcum.T
    for ci in range(nc):
        rows = slice(ci * c, (ci + 1) * c)
        for h in range(nh):
            cb = SM_BETA + d * nh + h
            cg = SM_ALPHA + d * nh + h
            b = ci * nh + h
            bet3[b] = jnp.broadcast_to(beta_all[rows, cb:cb + 1], (c, hw))
            gi3[b] = jnp.broadcast_to(gcum[rows, cg:cg + 1], (c, hw))
            gt3[b] = jnp.broadcast_to(gtot[rows, cg:cg + 1], (c, hw))
            gj3[b] = jnp.broadcast_to(gcum_t[cg:cg + 1, rows], (c, c))

    @pl.when(last if rev else first)
    def _():
        st_ref[...] = jnp.zeros_like(st_ref)

    r, cc, incl, strict = _tri_masks(c, rev)
    q = q3[...]
    k = k3[...]
    bet = bet3[...]
    gi = gi3[...]
    decay = jnp.where(incl[None], jnp.exp(gi[:, :, 0:c] - gj3[...]), 0.0)
    kb = k * bet
    a = jnp.where(strict[None], _bdot1_nt(kb, k) * decay, 0.0)
    tm = _unit_tri_inverse(a, r, cc)
    eg = jnp.exp(gi)
    w3[...] = _bdot3(tm, kb * eg)
    u3[...] = _bdot3(tm, v3[...] * bet)
    qk3[...] = _bdot1_nt(q, k) * decay
    q3[...] = q * eg
    k3[...] = k * jnp.exp(gt3[...] - gi)

    def chunk(ci, carry):
        cj = (nc - 1 - ci) if rev else ci
        r0 = pl.multiple_of(cj * c, c)
        for h in range(nh):
            b = cj * nh + h
            s_h = st_ref[h]
            v_new = u3[b] - _dot1(w3[b], s_h)
            o = _dot1(q3[b], s_h) + _dot1(qk3[b], v_new)
            st_ref[h] = s_h * jnp.exp(gt3[b][0:1, :]) + _dot1_tn(k3[b], v_new)
            out_ref[pl.ds(r0, c), h * hw:(h + 1) * hw] = o
        return carry

    lax.fori_loop(0, nc, chunk, 0)

    if final:
        o = out_ref[...] + ofwd_ref[...]
        z = z_ref[...]
        for h in range(GDN_HEADS):
            oh = _rms(o[:, h * hw:(h + 1) * hw], nw_ref[...])
            out_ref[:, h * hw:(h + 1) * hw] = oh * _silu(z[:, h * hw:(h + 1) * hw])
        out_ref[...] = _rms(out_ref[...], gn_ref[...])


def _gdn(u, conv_w, a_log, dt_bias, norm_w, gn_w, tb, seq):
    t = u.shape[0]
    nb = t // tb
    w3 = 3 * GROUP_W
    arow = jnp.zeros((1, 128), F32).at[0, SM_ALPHA:SM_ALPHA + 8].set(a_log.reshape(8))
    brow = jnp.zeros((1, 128), F32).at[0, SM_ALPHA:SM_ALPHA + 8].set(dt_bias.reshape(8))
    nbat = (tb // CHUNK) * GDN_HEADS
    wide = pltpu.VMEM((nbat, CHUNK, GDN_DK), F32)
    square = pltpu.VMEM((nbat, CHUNK, CHUNK), F32)
    scratch = [pltpu.VMEM((tb + 2 * HALO, w3), F32),
               wide, wide, wide, wide, wide, wide, square, wide, wide, square,
               pltpu.VMEM((GDN_HEADS, GDN_DK, GDN_DK), F32)]

    def call(rev, final, extra_in, extra_specs):
        prev, nxt = _halo_specs(tb, w3, COL_GDN_QKV, nb, rev, t)
        in_specs = [_main_spec(tb, w3, COL_GDN_QKV, nb, rev), prev, nxt,
                    _main_spec(tb, 128, COL_SMALL, nb, rev),
                    _const_spec((4, w3)), _const_spec((1, 128)), _const_spec((1, 128))] + extra_specs
        return pl.pallas_call(
            functools.partial(_gdn_kernel, rev, final, tb, nb, seq),
            grid=(nb,),
            in_specs=in_specs,
            out_specs=_main_spec(tb, GROUP_W, 0, nb, rev),
            out_shape=jax.ShapeDtypeStruct((t, GROUP_W), F32),
            scratch_shapes=scratch,
            compiler_params=_scan_params(),
            name="gdn_bwd" if rev else "gdn_fwd",
        )(u, u, u, u, conv_w, arow, brow, *extra_in)

    o_fwd = call(False, False, [], [])
    return call(True, True,
                [u, o_fwd, norm_w.reshape(1, -1), gn_w.reshape(1, -1)],
                [_main_spec(tb, GROUP_W, COL_GDN_Z, nb, True),
                 _main_spec(tb, GROUP_W, 0, nb, True),
                 _const_spec((1, GDN_DK)), _const_spec((1, GROUP_W))])


def _lru_kernel(rev, final, tb, nb, seq, *refs):
    if final:
        (x_ref, prev_ref, next_ref, cw_ref, cb_ref, w_ref, ba_ref, bi_ref, lam_ref,
         gate_ref, ofwd_ref, gn_ref, out_ref, xp_ref, h_ref) = refs
    else:
        (x_ref, prev_ref, next_ref, cw_ref, cb_ref, w_ref, ba_ref, bi_ref, lam_ref,
         out_ref, xp_ref, h_ref) = refs
    first, last = _block_pos(nb, tb, rev, seq)
    start = last if rev else first
    bw = LRU_BW

    _fill_pad(xp_ref, x_ref, prev_ref, next_ref, first, last, tb)
    xc = _conv4(xp_ref, cw_ref, tb) + cb_ref[...]

    @pl.when(start)
    def _():
        h_ref[...] = jnp.zeros_like(h_ref)

    rows = lax.broadcasted_iota(jnp.int32, (tb, bw), 0)
    start_row = (tb - 1) if rev else 0
    end_row = 0 if rev else (tb - 1)
    is_start_row = (rows == start_row) & start
    sp = _softplus(-lam_ref[...])
    for n in range(LRU_BLOCKS):
        xb = xc[:, n * bw:(n + 1) * bw]
        pre = _dot1(xb, w_ref[n])
        rg = _sig(pre[:, :bw] + ba_ref[:, n * bw:(n + 1) * bw])
        ig = _sig(pre[:, bw:] + bi_ref[:, n * bw:(n + 1) * bw])
        log_a = (-LRU_C) * rg * sp[:, n * bw:(n + 1) * bw]
        a = jnp.exp(log_a)
        mult = jnp.sqrt((1.0 + a * a) * jnp.tanh(-log_a))
        mult = jnp.where(is_start_row, 1.0, mult)
        b = mult * (ig * xb)
        s = 1
        while s < tb:
            if rev:
                a_sh = pltpu.roll(a, tb - s, 0)
                b_sh = pltpu.roll(b, tb - s, 0)
                ok = rows < tb - s
            else:
                a_sh = pltpu.roll(a, s, 0)
                b_sh = pltpu.roll(b, s, 0)
                ok = rows >= s
            b = jnp.where(ok, a * b_sh, 0.0) + b
            a = jnp.where(ok, a * a_sh, a)
            s *= 2
        hv = a * h_ref[:, n * bw:(n + 1) * bw] + b
        out_ref[:, n * bw:(n + 1) * bw] = hv
        h_ref[:, n * bw:(n + 1) * bw] = hv[end_row:end_row + 1, :]

    if final:
        o = (out_ref[...] + ofwd_ref[...]) * _gelu_tanh(gate_ref[...])
        out_ref[...] = _rms(o, gn_ref[...])


def _lru(u, conv_w, conv_b, wa, ba, wi, bi, lam, gn_w, tb, seq):
    t = u.shape[0]
    nb = t // tb
    w = GROUP_W
    wcat = jnp.concatenate([wa, wi], axis=-1).astype(BF16)
    scratch = [pltpu.VMEM((tb + 2 * HALO, w), F32), pltpu.VMEM((1, w), F32)]

    def call(rev, final, extra_in, extra_specs):
        d = 1 if rev else 0
        prev, nxt = _halo_specs(tb, w, COL_LRU_X, nb, rev, t)
        in_specs = [_main_spec(tb, w, COL_LRU_X, nb, rev), prev, nxt,
                    _const_spec((4, w)), _const_spec((1, w)),
                    _const_spec((LRU_BLOCKS, LRU_BW, 2 * LRU_BW)),
                    _const_spec((1, w)), _const_spec((1, w)), _const_spec((1, w))] + extra_specs
        return pl.pallas_call(
            functools.partial(_lru_kernel, rev, final, tb, nb, seq),
            grid=(nb,),
            in_specs=in_specs,
            out_specs=_main_spec(tb, w, 0, nb, rev),
            out_shape=jax.ShapeDtypeStruct((t, w), F32),
            scratch_shapes=scratch,
            compiler_params=_scan_params(),
            name="lru_bwd" if rev else "lru_fwd",
        )(u, u, u, conv_w, conv_b.reshape(1, w), wcat[d], ba[d].reshape(1, w),
          bi[d].reshape(1, w), lam[d].reshape(1, w), *extra_in)

    o_fwd = call(False, False, [], [])
    return call(True, True, [u, o_fwd, gn_w.reshape(1, w)],
                [_main_spec(tb, w, COL_LRU_GATE, nb, True), _main_spec(tb, w, 0, nb, True),
                 _const_spec((1, w))])


def _ssd_kernel(rev, final, tb, nb, seq, *refs):
    if final:
        (xbc_ref, prev_ref, next_ref, small_ref, cw_ref, cb_ref, arow_ref, brow_ref,
         z_ref, ofwd_ref, dskip_ref, nw_ref, gn_ref, out_ref, xp_ref, xbc_s, dt_s, st_ref) = refs
    else:
        (xbc_ref, prev_ref, next_ref, small_ref, cw_ref, cb_ref, arow_ref, brow_ref,
         out_ref, xp_ref, xbc_s, dt_s, st_ref) = refs
    d = 1 if rev else 0
    c = CHUNK
    nc = tb // c
    di = GROUP_W
    p = SSD_HEADDIM
    gw = di // SSD_GROUPS
    first, last = _block_pos(nb, tb, rev, seq)

    _fill_pad(xp_ref, xbc_ref, prev_ref, next_ref, first, last, tb)
    xbc_s[...] = _silu(_conv4(xp_ref, cw_ref, tb) + cb_ref[...])
    dt = _softplus(small_ref[...] + brow_ref[...])
    dt_s[:, 0:128] = dt
    dt_s[:, 128:256] = dt * (-jnp.exp(arow_ref[...]))

    @pl.when(last if rev else first)
    def _():
        st_ref[...] = jnp.zeros_like(st_ref)

    r, cc, incl, _ = _tri_masks(c, rev)
    m_incl = jnp.where(incl, 1.0, 0.0).astype(BF16)
    m_ones = jnp.ones((c, c), BF16)
    sr, sc = _iota2((128, di))
    sel = jnp.where(sr == SM_DT + d * SSD_HEADS + (sc >> 6), 1.0, 0.0).astype(BF16)

    def chunk(ci, carry):
        cj = (nc - 1 - ci) if rev else ci
        r0 = pl.multiple_of(cj * c, c)
        dt_a = dt_s[pl.ds(r0, c), 0:128]
        da_a = dt_s[pl.ds(r0, c), 128:256]
        acs = _dot_mask_l(m_incl, da_a)
        acs_t = acs.T
        tot = _dot_mask_l(m_ones, da_a)
        acs_x = _dot_mask_r(acs, sel)
        tot_x = _dot_mask_r(tot, sel)
        dt_x = _dot_mask_r(dt_a, sel)
        xs = xbc_s[pl.ds(r0, c), 0:di]
        xdt = xs * dt_x
        xdt_e = xdt * jnp.exp(tot_x - acs_x)
        e_acs = jnp.exp(acs_x)
        e_tot = jnp.exp(tot_x[0:1, :])
        for g in range(SSD_GROUPS):
            bm = xbc_s[pl.ds(r0, c), di + g * SSD_STATE:di + (g + 1) * SSD_STATE]
            cm = xbc_s[pl.ds(r0, c), di + (SSD_GROUPS + g) * SSD_STATE:di + (SSD_GROUPS + g + 1) * SSD_STATE]
            cbm = _dot1_nt(cm, bm)
            s_g = st_ref[g]
            y_off = _dot1(cm, s_g) * e_acs[:, g * gw:(g + 1) * gw]
            st_ref[g] = s_g * e_tot[:, g * gw:(g + 1) * gw] + _dot1_tn(bm, xdt_e[:, g * gw:(g + 1) * gw])
            for hh in range(SSD_HEADS // SSD_GROUPS):
                h = g * (SSD_HEADS // SSD_GROUPS) + hh
                col = SM_DT + d * SSD_HEADS + h
                ai = acs[:, col:col + 1]
                aj = acs_t[col:col + 1, :]
                lmat = jnp.where(incl, jnp.exp(ai - aj), 0.0)
                y_d = _dot1(cbm * lmat, xdt[:, h * p:(h + 1) * p])
                out_ref[pl.ds(r0, c), h * p:(h + 1) * p] = y_d + y_off[:, hh * p:(hh + 1) * p]
        return carry

    lax.fori_loop(0, nc, chunk, 0)

    if final:
        y = out_ref[...] + ofwd_ref[...] + dskip_ref[...] * xbc_s[:, 0:di]
        y = y * _silu(z_ref[...])
        for g in range(SSD_GROUPS):
            out_ref[:, g * gw:(g + 1) * gw] = _rms(y[:, g * gw:(g + 1) * gw], nw_ref[:, g * gw:(g + 1) * gw])
        out_ref[...] = _rms(out_ref[...], gn_ref[...])


def _ssd(u, conv_w, conv_b, a_log, dt_bias, d_skip, norm_w, gn_w, tb, seq):
    t = u.shape[0]
    nb = t // tb
    wx = GROUP_W + 2 * SSD_GROUPS * SSD_STATE
    arow = jnp.zeros((1, 128), F32).at[0, SM_DT:SM_DT + 16].set(a_log.reshape(16))
    brow = jnp.zeros((1, 128), F32).at[0, SM_DT:SM_DT + 16].set(dt_bias.reshape(16))
    dsk = jnp.repeat(d_skip, SSD_HEADDIM).reshape(1, GROUP_W)
    scratch = [pltpu.VMEM((tb + 2 * HALO, wx), F32), pltpu.VMEM((tb, wx), F32),
               pltpu.VMEM((tb, 256), F32),
               pltpu.VMEM((SSD_GROUPS, SSD_STATE, GROUP_W // SSD_GROUPS), F32)]

    def call(rev, final, extra_in, extra_specs):
        prev, nxt = _halo_specs(tb, wx, COL_SSD_XBC, nb, rev, t)
        in_specs = [_main_spec(tb, wx, COL_SSD_XBC, nb, rev), prev, nxt,
                    _main_spec(tb, 128, COL_SMALL, nb, rev),
                    _const_spec((4, wx)), _const_spec((1, wx)),
                    _const_spec((1, 128)), _const_spec((1, 128))] + extra_specs
        return pl.pallas_call(
            functools.partial(_ssd_kernel, rev, final, tb, nb, seq),
            grid=(nb,),
            in_specs=in_specs,
            out_specs=_main_spec(tb, GROUP_W, 0, nb, rev),
            out_shape=jax.ShapeDtypeStruct((t, GROUP_W), F32),
            scratch_shapes=scratch,
            compiler_params=_scan_params(),
            name="ssd_bwd" if rev else "ssd_fwd",
        )(u, u, u, u, conv_w, conv_b.reshape(1, wx), arow, brow, *extra_in)

    o_fwd = call(False, False, [], [])
    w = GROUP_W
    return call(True, True,
                [u, o_fwd, dsk, norm_w.reshape(1, w), gn_w.reshape(1, w)],
                [_main_spec(tb, w, COL_SSD_Z, nb, True), _main_spec(tb, w, 0, nb, True),
                 _const_spec((1, w)), _const_spec((1, w)), _const_spec((1, w))])


def _hgrn_kernel(rev, final, layer, tb, nb, seq, *refs):
    if final:
        (q_ref, f_ref, i_ref, lb_ref, g_ref, ofwd_ref, nw_ref, gn_ref, out_ref,
         k_s, b_s, v_s, qd_s, kd_s, gl_s, st_ref) = refs
    else:
        (q_ref, f_ref, i_ref, lb_ref, out_ref,
         k_s, b_s, v_s, qd_s, kd_s, gl_s, st_ref) = refs
    c = HGRN_CHUNK
    nc = tb // c
    w = GROUP_W
    hw = HGRN_DK
    pad = c
    first, last = _block_pos(nb, tb, rev, seq)

    lbp = lb_ref[...]
    mx = jnp.max(lbp, axis=0, keepdims=True)
    ex = jnp.exp(lbp - mx)
    den = jnp.sum(ex, axis=0, keepdims=True)
    lb = jnp.zeros((1, w), F32)
    for m in range(1, layer + 1):
        lb = lb + ex[m:m + 1, :] / den

    f = lb + (1.0 - lb) * _sig(f_ref[...])
    logf = jnp.log(f)
    kk = 1.0 - f
    qq = _silu(q_ref[...])
    vv = i_ref[...]

    r, cc = _iota2((tb, tb))
    same = (r >> 4) == (cc >> 4)
    incl = same & ((cc >= r) if rev else (cc <= r))
    m_incl = jnp.where(incl, 1.0, 0.0).astype(BF16)
    m_same = jnp.where(same, 1.0, 0.0).astype(BF16)
    b = _dot_mask_l(m_incl, logf)
    btot = _dot_mask_l(m_same, logf)

    zpad = jnp.zeros((pad, w), F32)
    for ref, val in ((k_s, kk), (b_s, b), (v_s, vv)):
        ref[0:pad, :] = zpad
        ref[pad:pad + tb, :] = val
        ref[pad + tb:2 * pad + tb, :] = zpad
    qd_s[...] = qq * jnp.exp(b)
    kd_s[...] = kk * jnp.exp(btot - b)
    gl_s[...] = jnp.exp(btot)

    pos = lax.broadcasted_iota(jnp.int32, (tb, 1), 0) & (c - 1)
    y = [jnp.zeros((tb, hw), F32) for _ in range(HGRN_HEADS)]
    for dl in range(c):
        off = pad + dl if rev else pad - dl
        ks = k_s[off:off + tb, :]
        bs = b_s[off:off + tb, :]
        vs = v_s[off:off + tb, :]
        ok = (pos <= c - 1 - dl) if rev else (pos >= dl)
        pr = qq * ks * jnp.exp(b - bs)
        for h in range(HGRN_HEADS):
            sc = jnp.sum(pr[:, h * hw:(h + 1) * hw], axis=-1, keepdims=True)
            sc = jnp.where(ok, sc, 0.0)
            y[h] = y[h] + sc * vs[:, h * hw:(h + 1) * hw]
    for h in range(HGRN_HEADS):
        out_ref[:, h * hw:(h + 1) * hw] = y[h]

    @pl.when(last if rev else first)
    def _():
        st_ref[...] = jnp.zeros_like(st_ref)

    def chunk(ci, carry):
        cj = (nc - 1 - ci) if rev else ci
        r0 = pl.multiple_of(cj * c, c)
        for h in range(HGRN_HEADS):
            sl = slice(h * hw, (h + 1) * hw)
            s_h = st_ref[h]
            qd = qd_s[pl.ds(r0, c), sl]
            kd = kd_s[pl.ds(r0, c), sl]
            vc = v_s[pl.ds(pad + r0, c), sl]
            gl = gl_s[pl.ds(r0, 1), sl]
            out_ref[pl.ds(r0, c), sl] += _dot1_nt(qd, s_h)
            st_ref[h] = s_h * gl + _dot1_tn(vc, kd)
        return carry

    lax.fori_loop(0, nc, chunk, 0)

    if final:
        o = out_ref[...] + ofwd_ref[...]
        gg = g_ref[...]
        for h in range(HGRN_HEADS):
            sl = slice(h * hw, (h + 1) * hw)
            out_ref[:, sl] = _rms(o[:, sl], nw_ref[...]) * _silu(gg[:, sl])
        out_ref[...] = _rms(out_ref[...], gn_ref[...])


def _hgrn(u, lb_param, layer, norm_w, gn_w, tb, seq):
    t = u.shape[0]
    nb = t // tb
    w = GROUP_W
    pad = HGRN_CHUNK
    scratch = [pltpu.VMEM((tb + 2 * pad, w), F32), pltpu.VMEM((tb + 2 * pad, w), F32),
               pltpu.VMEM((tb + 2 * pad, w), F32), pltpu.VMEM((tb, w), F32),
               pltpu.VMEM((tb, w), F32), pltpu.VMEM((tb, w), F32),
               pltpu.VMEM((HGRN_HEADS, HGRN_DK, HGRN_DK), F32)]

    def call(rev, final, extra_in, extra_specs):
        d = 1 if rev else 0
        in_specs = [_main_spec(tb, w, COL_HG_Q, nb, rev),
                    _main_spec(tb, w, COL_HG_F + d * w, nb, rev),
                    _main_spec(tb, w, COL_HG_I, nb, rev),
                    _const_spec((DEPTH, w))] + extra_specs
        return pl.pallas_call(
            functools.partial(_hgrn_kernel, rev, final, layer, tb, nb, seq),
            grid=(nb,),
            in_specs=in_specs,
            out_specs=_main_spec(tb, w, 0, nb, rev),
            out_shape=jax.ShapeDtypeStruct((t, w), F32),
            scratch_shapes=scratch,
            compiler_params=_scan_params(),
            name="hgrn_bwd" if rev else "hgrn_fwd",
        )(u, u, u, lb_param[d], *extra_in)

    o_fwd = call(False, False, [], [])
    return call(True, True, [u, o_fwd, norm_w.reshape(1, -1), gn_w.reshape(1, w)],
                [_main_spec(tb, w, COL_HG_G, nb, True), _main_spec(tb, w, 0, nb, True),
                 _const_spec((1, HGRN_DK)), _const_spec((1, w))])


def _outproj_kernel(x_ref, ya_ref, yb_ref, yc_ref, yd_ref, w_ref, o_ref):
    acc = x_ref[...]
    for n, y_ref in enumerate((ya_ref, yb_ref, yc_ref, yd_ref)):
        acc = acc + _dot(_bf(y_ref[...]), w_ref[n * GROUP_W:(n + 1) * GROUP_W, :])
    o_ref[...] = acc


def _outproj(x, ys, w):
    t, d = x.shape
    tm = 512
    yspec = pl.BlockSpec((tm, GROUP_W), lambda i: (i, 0))
    return pl.pallas_call(
        _outproj_kernel,
        grid=(t // tm,),
        in_specs=[pl.BlockSpec((tm, d), lambda i: (i, 0)), yspec, yspec, yspec, yspec,
                  pl.BlockSpec(w.shape, lambda i: (0, 0))],
        out_specs=pl.BlockSpec((tm, d), lambda i: (i, 0)),
        out_shape=jax.ShapeDtypeStruct((t, d), F32),
        compiler_params=pltpu.CompilerParams(dimension_semantics=("arbitrary",),
                                             vmem_limit_bytes=VMEM_LIMIT),
        name="outproj",
    )(x, *ys, w)


def _ffn_kernel(apply_final, tm, nrb, nf, seq, *refs):
    (x_ref, prev_ref, next_ref, ln_ref, wg_ref, wv_ref, cwg_ref, cwv_ref, cbg_ref, cbv_ref,
     wd_ref, fn_ref, o_ref, h_s, g_s, v_s, acc_s) = refs
    j = pl.program_id(1)
    i = pl.program_id(0)
    row0 = i * tm
    first = _seq_start(row0, seq)
    last = _seq_start(row0 + tm, seq)

    @pl.when(j == 0)
    def _():
        keep_p = jnp.where(first, 0.0, 1.0)
        keep_n = jnp.where(last, 0.0, 1.0)
        h_s[0:FHALO, :] = _bf(_rms(prev_ref[...], ln_ref[...]) * keep_p)
        h_s[FHALO:FHALO + tm, :] = _bf(_rms(x_ref[...], ln_ref[...]))
        h_s[FHALO + tm:2 * FHALO + tm, :] = _bf(_rms(next_ref[...], ln_ref[...]) * keep_n)
        acc_s[...] = jnp.zeros_like(acc_s)

    hb = h_s[...]
    g_s[...] = _dot(hb, wg_ref[...])
    v_s[...] = _dot(hb, wv_ref[...])

    def conv3(s_ref, cw_ref, cb_ref):
        acc = s_ref[FHALO - 1:FHALO - 1 + tm, :] * cw_ref[0:1, :]
        acc = acc + s_ref[FHALO:FHALO + tm, :] * cw_ref[1:2, :]
        acc = acc + s_ref[FHALO + 1:FHALO + 1 + tm, :] * cw_ref[2:3, :]
        return acc + cb_ref[...]

    act = _silu(conv3(g_s, cwg_ref, cbg_ref)) * conv3(v_s, cwv_ref, cbv_ref)
    acc_s[...] += _dot(_bf(act), wd_ref[...])

    @pl.when(j == nf - 1)
    def _():
        y = x_ref[...] + acc_s[...]
        if apply_final:
            y = _rms(y, fn_ref[...])
        o_ref[...] = y


def _ffn(x, ln, w_up, conv_w, conv_b, w_down, final_w, apply_final, seq):
    t, d = x.shape
    tm = 512
    tf = 512
    nf = D_FF // tf
    nrb = t // tm
    r = tm // FHALO
    cb = conv_b.reshape(1, 2 * D_FF)
    return pl.pallas_call(
        functools.partial(_ffn_kernel, apply_final, tm, nrb, nf, seq),
        grid=(nrb, nf),
        in_specs=[pl.BlockSpec((tm, d), lambda i, j: (i, 0)),
                  pl.BlockSpec((FHALO, d), lambda i, j: (jnp.maximum(i * r - 1, 0), 0)),
                  pl.BlockSpec((FHALO, d), lambda i, j: (jnp.minimum((i + 1) * r, t // FHALO - 1), 0)),
                  pl.BlockSpec((1, d), lambda i, j: (0, 0)),
                  pl.BlockSpec((d, tf), lambda i, j: (0, j)),
                  pl.BlockSpec((d, tf), lambda i, j: (0, j + nf)),
                  pl.BlockSpec((3, tf), lambda i, j: (0, j)),
                  pl.BlockSpec((3, tf), lambda i, j: (0, j + nf)),
                  pl.BlockSpec((1, tf), lambda i, j: (0, j)),
                  pl.BlockSpec((1, tf), lambda i, j: (0, j + nf)),
                  pl.BlockSpec((tf, d), lambda i, j: (j, 0)),
                  pl.BlockSpec((1, d), lambda i, j: (0, 0))],
        out_specs=pl.BlockSpec((tm, d), lambda i, j: (i, 0)),
        out_shape=jax.ShapeDtypeStruct((t, d), F32),
        scratch_shapes=[pltpu.VMEM((tm + 2 * FHALO, d), BF16),
                        pltpu.VMEM((tm + 2 * FHALO, tf), F32),
                        pltpu.VMEM((tm + 2 * FHALO, tf), F32),
                        pltpu.VMEM((tm, d), F32)],
        compiler_params=pltpu.CompilerParams(dimension_semantics=("arbitrary", "arbitrary"),
                                             vmem_limit_bytes=VMEM_LIMIT),
        name="ffn",
    )(x, x, x, ln.reshape(1, d), w_up, w_up, conv_w, conv_w, cb, cb, w_down, final_w.reshape(1, d))


def _permute_w_in(w):
    sizes = (1536, 512, 8, 8, 512, 512, 512, 1024, 16, 512, 1024, 512, 512)
    offs = [0]
    for s in sizes:
        offs.append(offs[-1] + s)
    (qkv, z, beta, alpha, lx, lg, sz, sxbc, sdt, hq, hf, hi, hg) = [
        w[:, offs[n]:offs[n + 1]] for n in range(len(sizes))]
    padw = jnp.zeros((w.shape[0], D_IN_P - COL_SMALL - 32), w.dtype)
    return jnp.concatenate([qkv, z, sxbc, lx, lg, sz, hq, hf, hi, hg, beta, alpha, sdt, padw], axis=1)


def kernel(x_prompt, x_sample, ln1, w_in, gdn_conv_w, gdn_a_log, gdn_dt_bias, gdn_norm_w, lru_conv_w, lru_conv_b, lru_wa, lru_ba, lru_wi, lru_bi, lru_lambda, ssd_conv_w, ssd_conv_b, ssd_a_log, ssd_dt_bias, ssd_d, ssd_norm_w, hgrn_lb, hgrn_norm_w, group_norm_w, w_out, ln2, w_up, ffn_conv_w, ffn_conv_b, w_down, final_norm):
    b, s, dm = x_prompt.shape
    db, ds, _ = x_sample.shape
    seq = (b * s, s, ds)
    tb = 256
    x = jnp.concatenate([x_prompt.reshape(b * s, dm), x_sample.reshape(db * ds, dm)], axis=0)
    depth = w_in.shape[0]
    for l in range(depth):
        u = _inproj(x, ln1[l], _permute_w_in(w_in[l]).astype(BF16))
        gn = group_norm_w[l]
        ya = _gdn(u, gdn_conv_w[l], gdn_a_log[l], gdn_dt_bias[l], gdn_norm_w[l], gn[0], tb, seq)
        yb = _lru(u, lru_conv_w[l], lru_conv_b[l], lru_wa[l], lru_ba[l], lru_wi[l], lru_bi[l],
                  lru_lambda[l], gn[1], tb, seq)
        yc = _ssd(u, ssd_conv_w[l], ssd_conv_b[l], ssd_a_log[l], ssd_dt_bias[l], ssd_d[l],
                  ssd_norm_w[l], gn[2], tb, seq)
        yd = _hgrn(u, hgrn_lb, l, hgrn_norm_w[l], gn[3], tb, seq)
        x = _outproj(x, (ya, yb, yc, yd), w_out[l].astype(BF16))
        x = _ffn(x, ln2[l], w_up[l].astype(BF16), ffn_conv_w[l], ffn_conv_b[l],
                 w_down[l].astype(BF16), final_norm, l == depth - 1, seq)
    return (x[:b * s].reshape(b, s, dm), x[b * s:].reshape(db, ds, dm))
```

```python
import functools
import math

import jax
import jax.numpy as jnp
import numpy as np
from jax import lax
from jax.experimental import pallas as pl
from jax.experimental.pallas import tpu as pltpu

F32 = jnp.float32
BF16 = jnp.bfloat16

D_MODEL = 2048
DEPTH = 2
GROUP_W = 512
GDN_HEADS = 4
GDN_DK = 128
LRU_BLOCKS = 4
LRU_BW = 128
LRU_C = 8.0
SSD_HEADS = 8
SSD_HEADDIM = 64
SSD_GROUPS = 2
SSD_STATE = 128
HGRN_HEADS = 4
HGRN_DK = 128
D_FF = 5632
CHUNK = 64
HGRN_LEVELS = 6
HGRN_STATE_CHUNK = 1 << HGRN_LEVELS
EPS = 1e-6

COL_GDN_QKV = 0
COL_GDN_Z = 1536
COL_SSD_XBC = 2048
COL_LRU_X = 3072
COL_LRU_GATE = 3584
COL_SSD_Z = 4096
COL_HG_Q = 4608
COL_HG_F = 5120
COL_HG_I = 6144
COL_HG_G = 6656
COL_SMALL = 7168
D_IN_P = 7296
SM_BETA = 0
SM_ALPHA = 8
SM_DT = 16

SUBLANES = 8
HALO = 8
FHALO = 16
VMEM_LIMIT = 56 * 1024 * 1024


def _sig(x):
    return 1.0 / (1.0 + jnp.exp(-x))


def _silu(x):
    return x * _sig(x)


def _softplus(x):
    return jnp.maximum(x, 0.0) + jnp.log1p(jnp.exp(-jnp.abs(x)))


def _gelu_tanh(x):
    c = math.sqrt(2.0 / math.pi)
    return 0.5 * x * (1.0 + jnp.tanh(c * (x + 0.044715 * (x * x * x))))


def _bf(x):
    return x.astype(BF16)


def _dot(a, b):
    return jnp.dot(a, b, preferred_element_type=F32)


def _dot1(a, b):
    return _dot(_bf(a), _bf(b))


def _dot1_nt(a, b):
    return lax.dot_general(_bf(a), _bf(b), (((1,), (1,)), ((), ())),
                           preferred_element_type=F32)


def _dot1_tn(a, b):
    return lax.dot_general(_bf(a), _bf(b), (((0,), (0,)), ((), ())),
                           preferred_element_type=F32)


def _dot_mask_l(m, b):
    b1 = _bf(b)
    r1 = b - b1.astype(F32)
    b2 = _bf(r1)
    b3 = _bf(r1 - b2.astype(F32))
    return _dot(m, b1) + (_dot(m, b2) + _dot(m, b3))


def _dot_mask_r(a, m):
    a1 = _bf(a)
    r1 = a - a1.astype(F32)
    a2 = _bf(r1)
    a3 = _bf(r1 - a2.astype(F32))
    return _dot(a1, m) + (_dot(a2, m) + _dot(a3, m))


def _rms(x, w):
    return x * lax.rsqrt(jnp.mean(x * x, axis=-1, keepdims=True) + EPS) * w


def _seq_start(row, seq):
    p_rows, s_len, ds_len = seq
    return jnp.where(row < p_rows, lax.rem(row, s_len) == 0,
                     lax.rem(row - p_rows, ds_len) == 0)


def _block_pos(nb, tb, rev, seq):
    i = pl.program_id(0)
    j = (nb - 1 - i) if rev else i
    row0 = j * tb
    first = _seq_start(row0, seq)
    last = _seq_start(row0 + tb, seq)
    return first, last


def _fill_pad(xp_ref, blk_ref, prev_ref, next_ref, first, last, tb):
    keep_p = jnp.where(first, 0.0, 1.0)
    keep_n = jnp.where(last, 0.0, 1.0)
    xp_ref[0:HALO, :] = prev_ref[...] * keep_p
    xp_ref[HALO:HALO + tb, :] = blk_ref[...]
    xp_ref[HALO + tb:2 * HALO + tb, :] = next_ref[...] * keep_n


def _conv4(xp_ref, w_ref, tb):
    acc = xp_ref[HALO - 2:HALO - 2 + tb, :] * w_ref[0:1, :]
    acc = acc + xp_ref[HALO - 1:HALO - 1 + tb, :] * w_ref[1:2, :]
    acc = acc + xp_ref[HALO:HALO + tb, :] * w_ref[2:3, :]
    acc = acc + xp_ref[HALO + 1:HALO + 1 + tb, :] * w_ref[3:4, :]
    return acc


def _iota2(shape):
    return (lax.broadcasted_iota(jnp.int32, shape, 0),
            lax.broadcasted_iota(jnp.int32, shape, 1))


def _tri_masks(c, rev):
    r, cc = _iota2((c, c))
    incl = (cc >= r) if rev else (cc <= r)
    strict = (cc > r) if rev else (cc < r)
    return r, cc, incl, strict


def _bdot(a, b):
    return lax.dot_general(a, b, (((2,), (1,)), ((0,), (0,))), preferred_element_type=F32)


def _bdot1_nt(a, b):
    return lax.dot_general(_bf(a), _bf(b), (((2,), (2,)), ((0,), (0,))),
                           preferred_element_type=F32)


def _bdot1(a, b):
    return _bdot(_bf(a), _bf(b))


def _unit_tri_inverse(a, r, c):
    n = a.shape[-1]
    eye = jnp.where(r == c, 1.0, 0.0)[None]
    d = jnp.where(((r >> 3) == (c >> 3))[None], a, 0.0)
    d2 = _bdot1(d, d)
    x = eye - d
    x = x + _bdot1(x, d2)
    d4 = _bdot1(d2, d2)
    x = x + _bdot1(x, d4)
    b = 16
    sh = 4
    while b <= n:
        joined = ((r >> sh) == (c >> sh)) & ((r >> (sh - 1)) != (c >> (sh - 1)))
        e = jnp.where(joined[None], a, 0.0)
        x = x - _bdot1(_bdot1(x, e), x)
        b *= 2
        sh += 1
    return x


def _inproj_kernel(x_ref, ln_ref, w_ref, o_ref, h_s):
    @pl.when(pl.program_id(1) == 0)
    def _():
        h_s[...] = _bf(_rms(x_ref[...], ln_ref[...]))

    o_ref[...] = _dot(h_s[...], w_ref[...])


def _inproj(x, ln, w):
    t, d = x.shape
    n = w.shape[1]
    tm = 512
    tn = n // 3
    return pl.pallas_call(
        _inproj_kernel,
        grid=(t // tm, n // tn),
        in_specs=[pl.BlockSpec((tm, d), lambda i, j: (i, 0)),
                  pl.BlockSpec((1, d), lambda i, j: (0, 0)),
                  pl.BlockSpec((d, tn), lambda i, j: (0, j))],
        out_specs=pl.BlockSpec((tm, tn), lambda i, j: (i, j)),
        out_shape=jax.ShapeDtypeStruct((t, n), F32),
        scratch_shapes=[pltpu.VMEM((tm, d), BF16)],
        compiler_params=pltpu.CompilerParams(
            dimension_semantics=("arbitrary", "arbitrary"),
            vmem_limit_bytes=VMEM_LIMIT),
        name="inproj",
    )(x, ln.reshape(1, d), w)


def _bidx(nb, rev):
    if rev:
        return lambda i: nb - 1 - i
    return lambda i: i


def _main_spec(tb, w, col, nb, rev):
    assert col % w == 0
    f = _bidx(nb, rev)
    return pl.BlockSpec((tb, w), lambda i: (f(i), col // w))


def _halo_specs(tb, w, col, nb, rev, t):
    assert col % w == 0
    f = _bidx(nb, rev)
    r = tb // HALO
    prev = pl.BlockSpec((HALO, w), lambda i: (jnp.maximum(f(i) * r - 1, 0), col // w))
    nxt = pl.BlockSpec((HALO, w), lambda i: (jnp.minimum((f(i) + 1) * r, t // HALO - 1), col // w))
    return prev, nxt


def _const_spec(shape):
    nd = len(shape)
    return pl.BlockSpec(shape, lambda i: (0,) * nd)


def _scan_params():
    return pltpu.CompilerParams(dimension_semantics=("arbitrary",),
                                vmem_limit_bytes=VMEM_LIMIT)


def _gdn_kernel(rev, final, tb, nb, seq, *refs):
    if final:
        (qkv_ref, prev_ref, next_ref, small_ref, cw_ref, arow_ref, brow_ref,
         z_ref, ofwd_ref, nw_ref, gn_ref, out_ref,
         xp_ref, q3, k3, v3, bet3, gi3, gt3, gj3, w3, u3, qk3, st_ref) = refs
    else:
        (qkv_ref, prev_ref, next_ref, small_ref, cw_ref, arow_ref, brow_ref, out_ref,
         xp_ref, q3, k3, v3, bet3, gi3, gt3, gj3, w3, u3, qk3, st_ref) = refs
    d = 1 if rev else 0
    c = CHUNK
    nc = tb // c
    nh = GDN_HEADS
    hw = GDN_DK
    kw = GDN_HEADS * GDN_DK
    first, last = _block_pos(nb, tb, rev, seq)

    _fill_pad(xp_ref, qkv_ref, prev_ref, next_ref, first, last, tb)
    y = _silu(_conv4(xp_ref, cw_ref, tb))
    for h in range(nh):
        qh = y[:, h * hw:(h + 1) * hw]
        qh = qh * lax.rsqrt(jnp.sum(qh * qh, axis=-1, keepdims=True) + EPS) * (GDN_DK ** -0.5)
        kh = y[:, kw + h * hw:kw + (h + 1) * hw]
        kh = kh * lax.rsqrt(jnp.sum(kh * kh, axis=-1, keepdims=True) + EPS)
        vh = y[:, 2 * kw + h * hw:2 * kw + (h + 1) * hw]
        for ci in range(nc):
            q3[ci * nh + h] = qh[ci * c:(ci + 1) * c, :]
            k3[ci * nh + h] = kh[ci * c:(ci + 1) * c, :]
            v3[ci * nh + h] = vh[ci * c:(ci + 1) * c, :]

    sm = small_ref[...]
    beta_all = _sig(sm)
    g_all = -jnp.exp(arow_ref[...]) * _softplus(sm + brow_ref[...])
    rb, cb_ = _iota2((tb, tb))
    same = (rb >> 6) == (cb_ >> 6)
    incl_b = same & ((cb_ >= rb) if rev else (cb_ <= rb))
    gcum = _dot_mask_l(jnp.where(incl_b, 1.0, 0.0).astype(BF16), g_all)
    gtot = _dot_mask_l(jnp.where(same, 1.0, 0.0).astype(BF16), g_all)
    gcum_t = gcum.T
    for ci in range(nc):
        rows = slice(ci * c, (ci + 1) * c)
        for h in range(nh):
            cb = SM_BETA + d * nh + h
            cg = SM_ALPHA + d * nh + h
            b = ci * nh + h
            bet3[b] = jnp.broadcast_to(beta_all[rows, cb:cb + 1], (c, hw))
            gi3[b] = jnp.broadcast_to(gcum[rows, cg:cg + 1], (c, hw))
            gt3[b] = jnp.broadcast_to(gtot[rows, cg:cg + 1], (c, hw))
            gj3[b] = jnp.broadcast_to(gcum_t[cg:cg + 1, rows], (c, c))

    @pl.when(last if rev else first)
    def _():
        st_ref[...] = jnp.zeros_like(st_ref)

    r, cc, incl, strict = _tri_masks(c, rev)
    q = q3[...]
    k = k3[...]
    bet = bet3[...]
    gi = gi3[...]
    decay = jnp.where(incl[None], jnp.exp(gi[:, :, 0:c] - gj3[...]), 0.0)
    kb = k * bet
    a = jnp.where(strict[None], _bdot1_nt(kb, k) * decay, 0.0)
    tm = _unit_tri_inverse(a, r, cc)
    eg = jnp.exp(gi)
    w3[...] = _bdot1(tm, kb * eg)
    u3[...] = _bdot1(tm, v3[...] * bet)
    qk3[...] = _bdot1_nt(q, k) * decay
    q3[...] = q * eg
    k3[...] = k * jnp.exp(gt3[...] - gi)

    def chunk(ci, carry):
        cj = (nc - 1 - ci) if rev else ci
        r0 = pl.multiple_of(cj * c, c)
        for h in range(nh):
            b = cj * nh + h
            s_h = st_ref[h]
            v_new = u3[b] - _dot1(w3[b], s_h)
            o = _dot1(q3[b], s_h) + _dot1(qk3[b], v_new)
            st_ref[h] = s_h * jnp.exp(gt3[b][0:1, :]) + _dot1_tn(k3[b], v_new)
            out_ref[pl.ds(r0, c), h * hw:(h + 1) * hw] = o
        return carry

    lax.fori_loop(0, nc, chunk, 0)

    if final:
        o = out_ref[...] + ofwd_ref[...]
        z = z_ref[...]
        for h in range(GDN_HEADS):
            oh = _rms(o[:, h * hw:(h + 1) * hw], nw_ref[...])
            out_ref[:, h * hw:(h + 1) * hw] = oh * _silu(z[:, h * hw:(h + 1) * hw])
        out_ref[...] = _rms(out_ref[...], gn_ref[...])


def _gdn(u, conv_w, a_log, dt_bias, norm_w, gn_w, tb, seq):
    t = u.shape[0]
    nb = t // tb
    w3 = 3 * GROUP_W
    arow = jnp.zeros((1, 128), F32).at[0, SM_ALPHA:SM_ALPHA + 8].set(a_log.reshape(8))
    brow = jnp.zeros((1, 128), F32).at[0, SM_ALPHA:SM_ALPHA + 8].set(dt_bias.reshape(8))
    nbat = (tb // CHUNK) * GDN_HEADS
    wide = pltpu.VMEM((nbat, CHUNK, GDN_DK), F32)
    square = pltpu.VMEM((nbat, CHUNK, CHUNK), F32)
    scratch = [pltpu.VMEM((tb + 2 * HALO, w3), F32),
               wide, wide, wide, wide, wide, wide, square, wide, wide, square,
               pltpu.VMEM((GDN_HEADS, GDN_DK, GDN_DK), F32)]

    def call(rev, final, extra_in, extra_specs):
        prev, nxt = _halo_specs(tb, w3, COL_GDN_QKV, nb, rev, t)
        in_specs = [_main_spec(tb, w3, COL_GDN_QKV, nb, rev), prev, nxt,
                    _main_spec(tb, 128, COL_SMALL, nb, rev),
                    _const_spec((4, w3)), _const_spec((1, 128)), _const_spec((1, 128))] + extra_specs
        return pl.pallas_call(
            functools.partial(_gdn_kernel, rev, final, tb, nb, seq),
            grid=(nb,),
            in_specs=in_specs,
            out_specs=_main_spec(tb, GROUP_W, 0, nb, rev),
            out_shape=jax.ShapeDtypeStruct((t, GROUP_W), F32),
            scratch_shapes=scratch,
            compiler_params=_scan_params(),
            name="gdn_bwd" if rev else "gdn_fwd",
        )(u, u, u, u, conv_w, arow, brow, *extra_in)

    o_fwd = call(False, False, [], [])
    return call(True, True,
                [u, o_fwd, norm_w.reshape(1, -1), gn_w.reshape(1, -1)],
                [_main_spec(tb, GROUP_W, COL_GDN_Z, nb, True),
                 _main_spec(tb, GROUP_W, 0, nb, True),
                 _const_spec((1, GDN_DK)), _const_spec((1, GROUP_W))])


def _lru_kernel(rev, final, tb, nb, seq, *refs):
    if final:
        (x_ref, prev_ref, next_ref, cw_ref, cb_ref, w_ref, ba_ref, bi_ref, lam_ref,
         gate_ref, ofwd_ref, gn_ref, out_ref, xp_ref, h_ref) = refs
    else:
        (x_ref, prev_ref, next_ref, cw_ref, cb_ref, w_ref, ba_ref, bi_ref, lam_ref,
         out_ref, xp_ref, h_ref) = refs
    first, last = _block_pos(nb, tb, rev, seq)
    start = last if rev else first
    bw = LRU_BW

    _fill_pad(xp_ref, x_ref, prev_ref, next_ref, first, last, tb)
    xc = _conv4(xp_ref, cw_ref, tb) + cb_ref[...]

    @pl.when(start)
    def _():
        h_ref[...] = jnp.zeros_like(h_ref)

    rows = lax.broadcasted_iota(jnp.int32, (tb, bw), 0)
    tile_rows = rows & (SUBLANES - 1)
    start_row = (tb - 1) if rev else 0
    is_start_row = (rows == start_row) & start
    sp = _softplus(-lam_ref[...])
    for n in range(LRU_BLOCKS):
        xb = xc[:, n * bw:(n + 1) * bw]
        pre = _dot1(xb, w_ref[n])
        rg = _sig(pre[:, :bw] + ba_ref[:, n * bw:(n + 1) * bw])
        ig = _sig(pre[:, bw:] + bi_ref[:, n * bw:(n + 1) * bw])
        log_a = (-LRU_C) * rg * sp[:, n * bw:(n + 1) * bw]
        a = jnp.exp(log_a)
        mult = jnp.sqrt((1.0 + a * a) * jnp.tanh(-log_a))
        mult = jnp.where(is_start_row, 1.0, mult)
        b = mult * (ig * xb)
        s = 1
        while s < SUBLANES:
            if rev:
                a_sh = pltpu.roll(a, tb - s, 0)
                b_sh = pltpu.roll(b, tb - s, 0)
                ok = tile_rows < SUBLANES - s
            else:
                a_sh = pltpu.roll(a, s, 0)
                b_sh = pltpu.roll(b, s, 0)
                ok = tile_rows >= s
            b = jnp.where(ok, a * b_sh, 0.0) + b
            a = jnp.where(ok, a * a_sh, a)
            s *= 2
        h_prev = h_ref[:, n * bw:(n + 1) * bw]
        ntiles = tb // SUBLANES
        for ti in (range(ntiles - 1, -1, -1) if rev else range(ntiles)):
            ts = slice(ti * SUBLANES, (ti + 1) * SUBLANES)
            ht = a[ts, :] * h_prev + b[ts, :]
            out_ref[ts, n * bw:(n + 1) * bw] = ht
            h_prev = ht[0:1, :] if rev else ht[SUBLANES - 1:SUBLANES, :]
        h_ref[:, n * bw:(n + 1) * bw] = h_prev

    if final:
        o = (out_ref[...] + ofwd_ref[...]) * _gelu_tanh(gate_ref[...])
        out_ref[...] = _rms(o, gn_ref[...])


def _lru(u, conv_w, conv_b, wa, ba, wi, bi, lam, gn_w, tb, seq):
    t = u.shape[0]
    nb = t // tb
    w = GROUP_W
    wcat = jnp.concatenate([wa, wi], axis=-1).astype(BF16)
    scratch = [pltpu.VMEM((tb + 2 * HALO, w), F32), pltpu.VMEM((1, w), F32)]

    def call(rev, final, extra_in, extra_specs):
        d = 1 if rev else 0
        prev, nxt = _halo_specs(tb, w, COL_LRU_X, nb, rev, t)
        in_specs = [_main_spec(tb, w, COL_LRU_X, nb, rev), prev, nxt,
                    _const_spec((4, w)), _const_spec((1, w)),
                    _const_spec((LRU_BLOCKS, LRU_BW, 2 * LRU_BW)),
                    _const_spec((1, w)), _const_spec((1, w)), _const_spec((1, w))] + extra_specs
        return pl.pallas_call(
            functools.partial(_lru_kernel, rev, final, tb, nb, seq),
            grid=(nb,),
            in_specs=in_specs,
            out_specs=_main_spec(tb, w, 0, nb, rev),
            out_shape=jax.ShapeDtypeStruct((t, w), F32),
            scratch_shapes=scratch,
            compiler_params=_scan_params(),
            name="lru_bwd" if rev else "lru_fwd",
        )(u, u, u, conv_w, conv_b.reshape(1, w), wcat[d], ba[d].reshape(1, w),
          bi[d].reshape(1, w), lam[d].reshape(1, w), *extra_in)

    o_fwd = call(False, False, [], [])
    return call(True, True, [u, o_fwd, gn_w.reshape(1, w)],
                [_main_spec(tb, w, COL_LRU_GATE, nb, True), _main_spec(tb, w, 0, nb, True),
                 _const_spec((1, w))])


def _ssd_kernel(rev, final, tb, nb, seq, *refs):
    if final:
        (xbc_ref, prev_ref, next_ref, small_ref, cw_ref, cb_ref, arow_ref, brow_ref,
         z_ref, ofwd_ref, dskip_ref, nw_ref, gn_ref, out_ref, xp_ref, xbc_s, st_ref) = refs
    else:
        (xbc_ref, prev_ref, next_ref, small_ref, cw_ref, cb_ref, arow_ref, brow_ref,
         out_ref, xp_ref, xbc_s, st_ref) = refs
    d = 1 if rev else 0
    c = CHUNK
    nc = tb // c
    di = GROUP_W
    p = SSD_HEADDIM
    gw = di // SSD_GROUPS
    first, last = _block_pos(nb, tb, rev, seq)

    assert p == c
    hpg = SSD_HEADS // SSD_GROUPS

    _fill_pad(xp_ref, xbc_ref, prev_ref, next_ref, first, last, tb)
    xbc_s[...] = _silu(_conv4(xp_ref, cw_ref, tb) + cb_ref[...])
    dt = _softplus(small_ref[...] + brow_ref[...])
    da = dt * (-jnp.exp(arow_ref[...]))

    rb, cb_ = _iota2((tb, tb))
    same = (rb >> 6) == (cb_ >> 6)
    incl_b = same & ((cb_ >= rb) if rev else (cb_ <= rb))
    acs = _dot_mask_l(jnp.where(incl_b, 1.0, 0.0).astype(BF16), da)
    tot = _dot_mask_l(jnp.where(same, 1.0, 0.0).astype(BF16), da)
    acs_t = acs.T
    sr, sc = _iota2((128, di))
    sel = jnp.where(sr == SM_DT + d * SSD_HEADS + (sc >> 6), 1.0, 0.0).astype(BF16)
    acs_x = _dot_mask_r(acs, sel)
    tot_x = _dot_mask_r(tot, sel)
    dt_x = _dot_mask_r(dt, sel)
    xdt = xbc_s[:, 0:di] * dt_x
    xdt_e = xdt * jnp.exp(tot_x - acs_x)
    e_acs = jnp.exp(acs_x)
    e_tot = jnp.exp(tot_x)
    xdt_b = _bf(xdt)

    @pl.when(last if rev else first)
    def _():
        st_ref[...] = jnp.zeros_like(st_ref)

    rr, lj = _iota2((c, gw))
    incl_cat = ((lj & (c - 1)) >= rr) if rev else ((lj & (c - 1)) <= rr)
    rbd, cbd = _iota2((gw, gw))
    head_diag = (rbd >> 6) == (cbd >> 6)

    for g in range(SSD_GROUPS):
        gs = slice(g * gw, (g + 1) * gw)
        s_g = st_ref[g]
        for ci in range(nc):
            cj = (nc - 1 - ci) if rev else ci
            rs = slice(cj * c, (cj + 1) * c)
            bm = xbc_s[rs, di + g * SSD_STATE:di + (g + 1) * SSD_STATE]
            cm = xbc_s[rs, di + (SSD_GROUPS + g) * SSD_STATE:di + (SSD_GROUPS + g + 1) * SSD_STATE]
            cbm = _dot1_nt(cm, bm)
            cb_cat = jnp.concatenate([cbm] * hpg, axis=1)
            col0 = SM_DT + d * SSD_HEADS + g * hpg
            a_row = jnp.concatenate([acs_t[col0 + hh:col0 + hh + 1, rs] for hh in range(hpg)], axis=1)
            l_cat = jnp.where(incl_cat, jnp.exp(acs_x[rs, gs] - a_row), 0.0)
            x_bd = jnp.where(head_diag, jnp.concatenate([xdt_b[rs, gs]] * hpg, axis=0),
                             jnp.zeros((), BF16))
            y_d = _dot1(cb_cat * l_cat, x_bd)
            y_off = _dot1(cm, s_g) * e_acs[rs, gs]
            out_ref[rs, gs] = y_d + y_off
            s_g = s_g * e_tot[cj * c:cj * c + 1, gs] + _dot1_tn(bm, xdt_e[rs, gs])
        st_ref[g] = s_g

    if final:
        y = out_ref[...] + ofwd_ref[...] + dskip_ref[...] * xbc_s[:, 0:di]
        y = y * _silu(z_ref[...])
        for g in range(SSD_GROUPS):
            out_ref[:, g * gw:(g + 1) * gw] = _rms(y[:, g * gw:(g + 1) * gw], nw_ref[:, g * gw:(g + 1) * gw])
        out_ref[...] = _rms(out_ref[...], gn_ref[...])


def _ssd(u, conv_w, conv_b, a_log, dt_bias, d_skip, norm_w, gn_w, tb, seq):
    t = u.shape[0]
    nb = t // tb
    wx = GROUP_W + 2 * SSD_GROUPS * SSD_STATE
    arow = jnp.zeros((1, 128), F32).at[0, SM_DT:SM_DT + 16].set(a_log.reshape(16))
    brow = jnp.zeros((1, 128), F32).at[0, SM_DT:SM_DT + 16].set(dt_bias.reshape(16))
    dsk = jnp.repeat(d_skip, SSD_HEADDIM).reshape(1, GROUP_W)
    scratch = [pltpu.VMEM((tb + 2 * HALO, wx), F32), pltpu.VMEM((tb, wx), F32),
               pltpu.VMEM((SSD_GROUPS, SSD_STATE, GROUP_W // SSD_GROUPS), F32)]

    def call(rev, final, extra_in, extra_specs):
        prev, nxt = _halo_specs(tb, wx, COL_SSD_XBC, nb, rev, t)
        in_specs = [_main_spec(tb, wx, COL_SSD_XBC, nb, rev), prev, nxt,
                    _main_spec(tb, 128, COL_SMALL, nb, rev),
                    _const_spec((4, wx)), _const_spec((1, wx)),
                    _const_spec((1, 128)), _const_spec((1, 128))] + extra_specs
        return pl.pallas_call(
            functools.partial(_ssd_kernel, rev, final, tb, nb, seq),
            grid=(nb,),
            in_specs=in_specs,
            out_specs=_main_spec(tb, GROUP_W, 0, nb, rev),
            out_shape=jax.ShapeDtypeStruct((t, GROUP_W), F32),
            scratch_shapes=scratch,
            compiler_params=_scan_params(),
            name="ssd_bwd" if rev else "ssd_fwd",
        )(u, u, u, u, conv_w, conv_b.reshape(1, wx), arow, brow, *extra_in)

    o_fwd = call(False, False, [], [])
    w = GROUP_W
    return call(True, True,
                [u, o_fwd, dsk, norm_w.reshape(1, w), gn_w.reshape(1, w)],
                [_main_spec(tb, w, COL_SSD_Z, nb, True), _main_spec(tb, w, 0, nb, True),
                 _const_spec((1, w)), _const_spec((1, w)), _const_spec((1, w))])


def _hgrn_kernel(rev, final, layer, tb, nb, seq, *refs):
    if final:
        (q_ref, f_ref, i_ref, lb_ref, mask_ref, g_ref, ofwd_ref, nw_ref, gn_ref, out_ref,
         d_s, z_s, qd_s, kd_s, st_ref) = refs
    else:
        (q_ref, f_ref, i_ref, lb_ref, mask_ref, out_ref,
         d_s, z_s, qd_s, kd_s, st_ref) = refs
    c = HGRN_STATE_CHUNK
    nl = HGRN_LEVELS
    nc = tb // c
    w = GROUP_W
    hw = HGRN_DK
    first, last = _block_pos(nb, tb, rev, seq)

    lbp = lb_ref[...]
    mx = jnp.max(lbp, axis=0, keepdims=True)
    ex = jnp.exp(lbp - mx)
    den = jnp.sum(ex, axis=0, keepdims=True)
    lb = jnp.zeros((1, w), F32)
    for m in range(1, layer + 1):
        lb = lb + ex[m:m + 1, :] / den

    f = lb + (1.0 - lb) * _sig(f_ref[...])
    logf = jnp.log(f)
    kk = 1.0 - f
    qq = _silu(q_ref[...])
    vv = i_ref[...]

    l1 = _bf(logf)
    l2 = _bf(logf - l1.astype(F32))

    d_s[...] = _dot(mask_ref[...], l1) + _dot(mask_ref[...], l2)

    def range_sum(blk):
        return d_s[blk * tb:(blk + 1) * tb, :]

    rows = lax.broadcasted_iota(jnp.int32, (tb, 1), 0)
    z_s[0] = _bf(qq)
    z_s[1] = _bf(kk)
    for lv in range(1, nl + 1):
        bit = (rows >> (lv - 1)) & 1
        is_query = (bit == 0) if rev else (bit == 1)
        z_s[lv + 1] = _bf(jnp.exp(range_sum(lv - 1)) * jnp.where(is_query, qq, kk))
    bsc = range_sum(nl)
    btot = range_sum(nl + 1)
    qd_s[...] = _bf(qq * jnp.exp(bsc))
    kd_s[...] = _bf(kk * jnp.exp(btot - bsc))
    gl = jnp.exp(btot)
    vb = _bf(vv)

    r64, c64 = _iota2((c, c))
    pair_masks = [r64 == c64]
    for lv in range(1, nl + 1):
        rbit = (r64 >> (lv - 1)) & 1
        cbit = (c64 >> (lv - 1)) & 1
        split = ((rbit == 0) & (cbit == 1)) if rev else ((rbit == 1) & (cbit == 0))
        pair_masks.append(((r64 >> lv) == (c64 >> lv)) & split)

    @pl.when(last if rev else first)
    def _():
        st_ref[...] = jnp.zeros_like(st_ref)

    for h in range(HGRN_HEADS):
        sl = slice(h * hw, (h + 1) * hw)
        s_h = st_ref[h]
        for ci in range(nc):
            cj = (nc - 1 - ci) if rev else ci
            rs = slice(cj * c, (cj + 1) * c)
            sc = jnp.where(pair_masks[0], _dot1_nt(z_s[0, rs, sl], z_s[1, rs, sl]), 0.0)
            for lv in range(1, nl + 1):
                zz = z_s[lv + 1, rs, sl]
                sc = sc + jnp.where(pair_masks[lv], _dot1_nt(zz, zz), 0.0)
            out_ref[rs, sl] = _dot1(sc, vb[rs, sl]) + _dot1_nt(qd_s[rs, sl], s_h)
            s_h = s_h * gl[cj * c:cj * c + 1, sl] + _dot1_tn(vb[rs, sl], kd_s[rs, sl])
        st_ref[h] = s_h

    if final:
        o = out_ref[...] + ofwd_ref[...]
        gg = g_ref[...]
        for h in range(HGRN_HEADS):
            sl = slice(h * hw, (h + 1) * hw)
            out_ref[:, sl] = _rms(o[:, sl], nw_ref[...]) * _silu(gg[:, sl])
        out_ref[...] = _rms(out_ref[...], gn_ref[...])


def _hgrn_masks(tb, rev):
    t = np.arange(tb)[:, None]
    r = np.arange(tb)[None, :]
    blocks = []
    for lv in range(1, HGRN_LEVELS + 1):
        g = 1 << lv
        mid = (t // g) * g + g // 2
        if rev:
            m = np.where(t < mid, (r >= t) & (r < mid), (r >= mid) & (r < t))
        else:
            m = np.where(t >= mid, (r >= mid) & (r <= t), (r > t) & (r < mid))
        blocks.append(m)
    same = (r // HGRN_STATE_CHUNK) == (t // HGRN_STATE_CHUNK)
    blocks.append(same & ((r >= t) if rev else (r <= t)))
    blocks.append(same)
    return jnp.asarray(np.concatenate(blocks, axis=0).astype(np.float32), dtype=BF16)


def _hgrn(u, lb_param, layer, norm_w, gn_w, tb, seq):
    t = u.shape[0]
    nb = t // tb
    w = GROUP_W
    nblk = HGRN_LEVELS + 2
    scratch = [pltpu.VMEM((nblk * tb, w), F32),
               pltpu.VMEM((nblk, tb, w), BF16), pltpu.VMEM((tb, w), BF16),
               pltpu.VMEM((tb, w), BF16),
               pltpu.VMEM((HGRN_HEADS, HGRN_DK, HGRN_DK), F32)]

    def call(rev, final, extra_in, extra_specs):
        d = 1 if rev else 0
        in_specs = [_main_spec(tb, w, COL_HG_Q, nb, rev),
                    _main_spec(tb, w, COL_HG_F + d * w, nb, rev),
                    _main_spec(tb, w, COL_HG_I, nb, rev),
                    _const_spec((DEPTH, w)), _const_spec((nblk * tb, tb))] + extra_specs
        return pl.pallas_call(
            functools.partial(_hgrn_kernel, rev, final, layer, tb, nb, seq),
            grid=(nb,),
            in_specs=in_specs,
            out_specs=_main_spec(tb, w, 0, nb, rev),
            out_shape=jax.ShapeDtypeStruct((t, w), F32),
            scratch_shapes=scratch,
            compiler_params=_scan_params(),
            name="hgrn_bwd" if rev else "hgrn_fwd",
        )(u, u, u, lb_param[d], _hgrn_masks(tb, rev), *extra_in)

    o_fwd = call(False, False, [], [])
    return call(True, True, [u, o_fwd, norm_w.reshape(1, -1), gn_w.reshape(1, w)],
                [_main_spec(tb, w, COL_HG_G, nb, True), _main_spec(tb, w, 0, nb, True),
                 _const_spec((1, HGRN_DK)), _const_spec((1, w))])


def _outproj_kernel(x_ref, ya_ref, yb_ref, yc_ref, yd_ref, w_ref, o_ref):
    acc = x_ref[...]
    for n, y_ref in enumerate((ya_ref, yb_ref, yc_ref, yd_ref)):
        acc = acc + _dot(_bf(y_ref[...]), w_ref[n * GROUP_W:(n + 1) * GROUP_W, :])
    o_ref[...] = acc


def _outproj(x, ys, w):
    t, d = x.shape
    tm = 512
    yspec = pl.BlockSpec((tm, GROUP_W), lambda i: (i, 0))
    return pl.pallas_call(
        _outproj_kernel,
        grid=(t // tm,),
        in_specs=[pl.BlockSpec((tm, d), lambda i: (i, 0)), yspec, yspec, yspec, yspec,
                  pl.BlockSpec(w.shape, lambda i: (0, 0))],
        out_specs=pl.BlockSpec((tm, d), lambda i: (i, 0)),
        out_shape=jax.ShapeDtypeStruct((t, d), F32),
        compiler_params=pltpu.CompilerParams(dimension_semantics=("arbitrary",),
                                             vmem_limit_bytes=VMEM_LIMIT),
        name="outproj",
    )(x, *ys, w)


def _ffn_kernel(apply_final, tm, nrb, nf, seq, *refs):
    (x_ref, prev_ref, next_ref, ln_ref, wg_ref, wv_ref, cwg_ref, cwv_ref, cbg_ref, cbv_ref,
     wd_ref, fn_ref, o_ref, h_s, g_s, v_s, acc_s) = refs
    j = pl.program_id(1)
    i = pl.program_id(0)
    row0 = i * tm
    first = _seq_start(row0, seq)
    last = _seq_start(row0 + tm, seq)

    @pl.when(j == 0)
    def _():
        keep_p = jnp.where(first, 0.0, 1.0)
        keep_n = jnp.where(last, 0.0, 1.0)
        h_s[0:FHALO, :] = _bf(_rms(prev_ref[...], ln_ref[...]) * keep_p)
        h_s[FHALO:FHALO + tm, :] = _bf(_rms(x_ref[...], ln_ref[...]))
        h_s[FHALO + tm:2 * FHALO + tm, :] = _bf(_rms(next_ref[...], ln_ref[...]) * keep_n)
        acc_s[...] = jnp.zeros_like(acc_s)

    hb = h_s[...]
    g_s[...] = _dot(hb, wg_ref[...])
    v_s[...] = _dot(hb, wv_ref[...])

    def conv3(s_ref, cw_ref, cb_ref):
        acc = s_ref[FHALO - 1:FHALO - 1 + tm, :] * cw_ref[0:1, :]
        acc = acc + s_ref[FHALO:FHALO + tm, :] * cw_ref[1:2, :]
        acc = acc + s_ref[FHALO + 1:FHALO + 1 + tm, :] * cw_ref[2:3, :]
        return acc + cb_ref[...]

    act = _silu(conv3(g_s, cwg_ref, cbg_ref)) * conv3(v_s, cwv_ref, cbv_ref)
    acc_s[...] += _dot(_bf(act), wd_ref[...])

    @pl.when(j == nf - 1)
    def _():
        y = x_ref[...] + acc_s[...]
        if apply_final:
            y = _rms(y, fn_ref[...])
        o_ref[...] = y


def _ffn(x, ln, w_up, conv_w, conv_b, w_down, final_w, apply_final, seq):
    t, d = x.shape
    tm = 512
    tf = 512
    nf = D_FF // tf
    nrb = t // tm
    r = tm // FHALO
    cb = conv_b.reshape(1, 2 * D_FF)
    up_buf = pltpu.VMEM((tm + 2 * FHALO, tf), F32)
    return pl.pallas_call(
        functools.partial(_ffn_kernel, apply_final, tm, nrb, nf, seq),
        grid=(nrb, nf),
        in_specs=[pl.BlockSpec((tm, d), lambda i, j: (i, 0)),
                  pl.BlockSpec((FHALO, d), lambda i, j: (jnp.maximum(i * r - 1, 0), 0)),
                  pl.BlockSpec((FHALO, d), lambda i, j: (jnp.minimum((i + 1) * r, t // FHALO - 1), 0)),
                  pl.BlockSpec((1, d), lambda i, j: (0, 0)),
                  pl.BlockSpec((d, tf), lambda i, j: (0, j)),
                  pl.BlockSpec((d, tf), lambda i, j: (0, j + nf)),
                  pl.BlockSpec((3, tf), lambda i, j: (0, j)),
                  pl.BlockSpec((3, tf), lambda i, j: (0, j + nf)),
                  pl.BlockSpec((1, tf), lambda i, j: (0, j)),
                  pl.BlockSpec((1, tf), lambda i, j: (0, j + nf)),
                  pl.BlockSpec((tf, d), lambda i, j: (j, 0)),
                  pl.BlockSpec((1, d), lambda i, j: (0, 0))],
        out_specs=pl.BlockSpec((tm, d), lambda i, j: (i, 0)),
        out_shape=jax.ShapeDtypeStruct((t, d), F32),
        scratch_shapes=[pltpu.VMEM((tm + 2 * FHALO, d), BF16), up_buf, up_buf,
                        pltpu.VMEM((tm, d), F32)],
        compiler_params=pltpu.CompilerParams(dimension_semantics=("arbitrary", "arbitrary"),
                                             vmem_limit_bytes=VMEM_LIMIT),
        name="ffn",
    )(x, x, x, ln.reshape(1, d), w_up, w_up, conv_w, conv_w, cb, cb, w_down, final_w.reshape(1, d))


def _permute_w_in(w):
    sizes = (1536, 512, 8, 8, 512, 512, 512, 1024, 16, 512, 1024, 512, 512)
    offs = [0]
    for s in sizes:
        offs.append(offs[-1] + s)
    (qkv, z, beta, alpha, lx, lg, sz, sxbc, sdt, hq, hf, hi, hg) = [
        w[:, offs[n]:offs[n + 1]] for n in range(len(sizes))]
    padw = jnp.zeros((w.shape[0], D_IN_P - COL_SMALL - 32), w.dtype)
    return jnp.concatenate([qkv, z, sxbc, lx, lg, sz, hq, hf, hi, hg, beta, alpha, sdt, padw], axis=1)


def kernel(x_prompt, x_sample, ln1, w_in, gdn_conv_w, gdn_a_log, gdn_dt_bias, gdn_norm_w, lru_conv_w, lru_conv_b, lru_wa, lru_ba, lru_wi, lru_bi, lru_lambda, ssd_conv_w, ssd_conv_b, ssd_a_log, ssd_dt_bias, ssd_d, ssd_norm_w, hgrn_lb, hgrn_norm_w, group_norm_w, w_out, ln2, w_up, ffn_conv_w, ffn_conv_b, w_down, final_norm):
    b, s, dm = x_prompt.shape
    db, ds, _ = x_sample.shape
    seq = (b * s, s, ds)
    tb = 256
    x = jnp.concatenate([x_prompt.reshape(b * s, dm), x_sample.reshape(db * ds, dm)], axis=0)
    depth = w_in.shape[0]
    for l in range(depth):
        u = _inproj(x, ln1[l], _permute_w_in(w_in[l]).astype(BF16))
        gn = group_norm_w[l]
        ya = _gdn(u, gdn_conv_w[l], gdn_a_log[l], gdn_dt_bias[l], gdn_norm_w[l], gn[0], tb, seq)
        yb = _lru(u, lru_conv_w[l], lru_conv_b[l], lru_wa[l], lru_ba[l], lru_wi[l], lru_bi[l],
                  lru_lambda[l], gn[1], tb, seq)
        yc = _ssd(u, ssd_conv_w[l], ssd_conv_b[l], ssd_a_log[l], ssd_dt_bias[l], ssd_d[l],
                  ssd_norm_w[l], gn[2], tb, seq)
        yd = _hgrn(u, hgrn_lb, l, hgrn_norm_w[l], gn[3], tb, seq)
        x = _outproj(x, (ya, yb, yc, yd), w_out[l].astype(BF16))
        x = _ffn(x, ln2[l], w_up[l].astype(BF16), ffn_conv_w[l], ffn_conv_b[l],
                 w_down[l].astype(BF16), final_norm, l == depth - 1, seq)
    return (x[:b * s].reshape(b, s, dm), x[b * s:].reshape(db, ds, dm))
```

```python
import functools
import math

import jax
import jax.numpy as jnp
import numpy as np
from jax import lax
from jax.experimental import pallas as pl
from jax.experimental.pallas import tpu as pltpu

F32 = jnp.float32
BF16 = jnp.bfloat16

D_MODEL = 2048
DEPTH = 2
GROUP_W = 512
GDN_HEADS = 4
GDN_DK = 128
LRU_BLOCKS = 4
LRU_BW = 128
LRU_C = 8.0
SSD_HEADS = 8
SSD_HEADDIM = 64
SSD_GROUPS = 2
SSD_STATE = 128
HGRN_HEADS = 4
HGRN_DK = 128
D_FF = 5632
CHUNK = 64
HGRN_LEVELS = 6
HGRN_STATE_CHUNK = 1 << HGRN_LEVELS
EPS = 1e-6

COL_GDN_QKV = 0
COL_GDN_Z = 1536
COL_SSD_XBC = 2048
COL_LRU_X = 3072
COL_LRU_GATE = 3584
COL_SSD_Z = 4096
COL_HG_Q = 4608
COL_HG_F = 5120
COL_HG_I = 6144
COL_HG_G = 6656
COL_SMALL = 7168
D_IN_P = 7296
SM_BETA = 0
SM_ALPHA = 8
SM_DT = 16

SUBLANES = 8
HALO = 8
FHALO = 16
VMEM_LIMIT = 56 * 1024 * 1024
TB_SCAN = 512
TB_HGRN = 256


def _sig(x):
    return 1.0 / (1.0 + jnp.exp(-x))


def _silu(x):
    return x * _sig(x)


def _softplus(x):
    return jnp.maximum(x, 0.0) + jnp.log1p(jnp.exp(-jnp.abs(x)))


def _gelu_tanh(x):
    c = math.sqrt(2.0 / math.pi)
    return 0.5 * x * (1.0 + jnp.tanh(c * (x + 0.044715 * (x * x * x))))


def _bf(x):
    return x.astype(BF16)


def _dot(a, b):
    return jnp.dot(a, b, preferred_element_type=F32)


def _dot1(a, b):
    return _dot(_bf(a), _bf(b))


def _dot1_nt(a, b):
    return lax.dot_general(_bf(a), _bf(b), (((1,), (1,)), ((), ())),
                           preferred_element_type=F32)


def _dot1_tn(a, b):
    return lax.dot_general(_bf(a), _bf(b), (((0,), (0,)), ((), ())),
                           preferred_element_type=F32)


def _dot_mask_l(m, b):
    b1 = _bf(b)
    r1 = b - b1.astype(F32)
    b2 = _bf(r1)
    b3 = _bf(r1 - b2.astype(F32))
    return _dot(m, b1) + (_dot(m, b2) + _dot(m, b3))


def _dot_mask_r(a, m):
    a1 = _bf(a)
    r1 = a - a1.astype(F32)
    a2 = _bf(r1)
    a3 = _bf(r1 - a2.astype(F32))
    return _dot(a1, m) + (_dot(a2, m) + _dot(a3, m))


def _rms(x, w):
    return x * lax.rsqrt(jnp.mean(x * x, axis=-1, keepdims=True) + EPS) * w


def _seq_start(row, seq):
    p_rows, s_len, ds_len = seq
    return jnp.where(row < p_rows, lax.rem(row, s_len) == 0,
                     lax.rem(row - p_rows, ds_len) == 0)


def _block_pos(nb, tb, rev, seq):
    i = pl.program_id(0)
    j = (nb - 1 - i) if rev else i
    row0 = j * tb
    first = _seq_start(row0, seq)
    last = _seq_start(row0 + tb, seq)
    return first, last


def _fill_pad(xp_ref, blk_ref, prev_ref, next_ref, first, last, tb):
    keep_p = jnp.where(first, 0.0, 1.0)
    keep_n = jnp.where(last, 0.0, 1.0)
    xp_ref[0:HALO, :] = prev_ref[...] * keep_p
    xp_ref[HALO:HALO + tb, :] = blk_ref[...]
    xp_ref[HALO + tb:2 * HALO + tb, :] = next_ref[...] * keep_n


def _conv4(xp_ref, w_ref, tb):
    acc = xp_ref[HALO - 2:HALO - 2 + tb, :] * w_ref[0:1, :]
    acc = acc + xp_ref[HALO - 1:HALO - 1 + tb, :] * w_ref[1:2, :]
    acc = acc + xp_ref[HALO:HALO + tb, :] * w_ref[2:3, :]
    acc = acc + xp_ref[HALO + 1:HALO + 1 + tb, :] * w_ref[3:4, :]
    return acc


def _iota2(shape):
    return (lax.broadcasted_iota(jnp.int32, shape, 0),
            lax.broadcasted_iota(jnp.int32, shape, 1))


def _tri_masks(c, rev):
    r, cc = _iota2((c, c))
    incl = (cc >= r) if rev else (cc <= r)
    strict = (cc > r) if rev else (cc < r)
    return r, cc, incl, strict


def _bdot(a, b):
    return lax.dot_general(a, b, (((2,), (1,)), ((0,), (0,))), preferred_element_type=F32)


def _bdot1_nt(a, b):
    return lax.dot_general(_bf(a), _bf(b), (((2,), (2,)), ((0,), (0,))),
                           preferred_element_type=F32)


def _bdot1(a, b):
    return _bdot(_bf(a), _bf(b))


def _unit_tri_inverse(a, r, c):
    n = a.shape[-1]
    eye = jnp.where(r == c, 1.0, 0.0)[None]
    d = jnp.where(((r >> 3) == (c >> 3))[None], a, 0.0)
    d2 = _bdot1(d, d)
    x = eye - d
    x = x + _bdot1(x, d2)
    d4 = _bdot1(d2, d2)
    x = x + _bdot1(x, d4)
    b = 16
    sh = 4
    while b <= n:
        joined = ((r >> sh) == (c >> sh)) & ((r >> (sh - 1)) != (c >> (sh - 1)))
        e = jnp.where(joined[None], a, 0.0)
        x = x - _bdot1(_bdot1(x, e), x)
        b *= 2
        sh += 1
    return x


def _inproj_kernel(x_ref, ln_ref, w_ref, o_ref, h_s):
    @pl.when(pl.program_id(1) == 0)
    def _():
        h_s[...] = _bf(_rms(x_ref[...], ln_ref[...]))

    o_ref[...] = _dot(h_s[...], w_ref[...])


def _inproj(x, ln, w):
    t, d = x.shape
    n = w.shape[1]
    tm = 512
    tn = n // 3
    return pl.pallas_call(
        _inproj_kernel,
        grid=(t // tm, n // tn),
        in_specs=[pl.BlockSpec((tm, d), lambda i, j: (i, 0)),
                  pl.BlockSpec((1, d), lambda i, j: (0, 0)),
                  pl.BlockSpec((d, tn), lambda i, j: (0, j))],
        out_specs=pl.BlockSpec((tm, tn), lambda i, j: (i, j)),
        out_shape=jax.ShapeDtypeStruct((t, n), F32),
        scratch_shapes=[pltpu.VMEM((tm, d), BF16)],
        compiler_params=pltpu.CompilerParams(
            dimension_semantics=("arbitrary", "arbitrary"),
            vmem_limit_bytes=VMEM_LIMIT),
        name="inproj",
    )(x, ln.reshape(1, d), w)


def _bidx(nb, rev):
    if rev:
        return lambda i: nb - 1 - i
    return lambda i: i


def _main_spec(tb, w, col, nb, rev):
    assert col % w == 0
    f = _bidx(nb, rev)
    return pl.BlockSpec((tb, w), lambda i: (f(i), col // w))


def _halo_specs(tb, w, col, nb, rev, t):
    assert col % w == 0
    f = _bidx(nb, rev)
    r = tb // HALO
    prev = pl.BlockSpec((HALO, w), lambda i: (jnp.maximum(f(i) * r - 1, 0), col // w))
    nxt = pl.BlockSpec((HALO, w), lambda i: (jnp.minimum((f(i) + 1) * r, t // HALO - 1), col // w))
    return prev, nxt


def _const_spec(shape):
    nd = len(shape)
    return pl.BlockSpec(shape, lambda i: (0,) * nd)


def _scan_params():
    return pltpu.CompilerParams(dimension_semantics=("arbitrary",),
                                vmem_limit_bytes=VMEM_LIMIT)


def _gdn_kernel(rev, final, tb, nb, seq, *refs):
    if final:
        (qkv_ref, prev_ref, next_ref, small_ref, cw_ref, arow_ref, brow_ref,
         z_ref, ofwd_ref, nw_ref, gn_ref, out_ref,
         xp_ref, q3, k3, v3, bet3, gi3, gt3, gj3, w3, u3, qk3, st_ref) = refs
    else:
        (qkv_ref, prev_ref, next_ref, small_ref, cw_ref, arow_ref, brow_ref, out_ref,
         xp_ref, q3, k3, v3, bet3, gi3, gt3, gj3, w3, u3, qk3, st_ref) = refs
    d = 1 if rev else 0
    c = CHUNK
    nc = tb // c
    nh = GDN_HEADS
    hw = GDN_DK
    kw = GDN_HEADS * GDN_DK
    first, last = _block_pos(nb, tb, rev, seq)

    _fill_pad(xp_ref, qkv_ref, prev_ref, next_ref, first, last, tb)
    y = _silu(_conv4(xp_ref, cw_ref, tb))
    for h in range(nh):
        qh = y[:, h * hw:(h + 1) * hw]
        qh = qh * lax.rsqrt(jnp.sum(qh * qh, axis=-1, keepdims=True) + EPS) * (GDN_DK ** -0.5)
        kh = y[:, kw + h * hw:kw + (h + 1) * hw]
        kh = kh * lax.rsqrt(jnp.sum(kh * kh, axis=-1, keepdims=True) + EPS)
        vh = y[:, 2 * kw + h * hw:2 * kw + (h + 1) * hw]
        for ci in range(nc):
            q3[ci * nh + h] = qh[ci * c:(ci + 1) * c, :]
            k3[ci * nh + h] = kh[ci * c:(ci + 1) * c, :]
            v3[ci * nh + h] = vh[ci * c:(ci + 1) * c, :]

    sm = small_ref[...]
    beta_all = _sig(sm)
    g_all = -jnp.exp(arow_ref[...]) * _softplus(sm + brow_ref[...])
    rb, cb_ = _iota2((tb, tb))
    same = (rb >> 6) == (cb_ >> 6)
    incl_b = same & ((cb_ >= rb) if rev else (cb_ <= rb))
    gcum = _dot_mask_l(jnp.where(incl_b, 1.0, 0.0).astype(BF16), g_all)
    gtot = _dot_mask_l(jnp.where(same, 1.0, 0.0).astype(BF16), g_all)
    gcum_t = gcum.T
    for ci in range(nc):
        rows = slice(ci * c, (ci + 1) * c)
        for h in range(nh):
            cb = SM_BETA + d * nh + h
            cg = SM_ALPHA + d * nh + h
            b = ci * nh + h
            bet3[b] = jnp.broadcast_to(beta_all[rows, cb:cb + 1], (c, hw))
            gi3[b] = jnp.broadcast_to(gcum[rows, cg:cg + 1], (c, hw))
            gt3[b] = jnp.broadcast_to(gtot[rows, cg:cg + 1], (c, hw))
            gj3[b] = jnp.broadcast_to(gcum_t[cg:cg + 1, rows], (c, c))

    @pl.when(last if rev else first)
    def _():
        st_ref[...] = jnp.zeros_like(st_ref)

    r, cc, incl, strict = _tri_masks(c, rev)
    q = q3[...]
    k = k3[...]
    bet = bet3[...]
    gi = gi3[...]
    decay = jnp.where(incl[None], jnp.exp(gi[:, :, 0:c] - gj3[...]), 0.0)
    kb = k * bet
    a = jnp.where(strict[None], _bdot1_nt(kb, k) * decay, 0.0)
    tm = _unit_tri_inverse(a, r, cc)
    eg = jnp.exp(gi)
    w3[...] = _bdot1(tm, kb * eg)
    u3[...] = _bdot1(tm, v3[...] * bet)
    qk3[...] = _bdot1_nt(q, k) * decay
    q3[...] = q * eg
    k3[...] = k * jnp.exp(gt3[...] - gi)

    def chunk(ci, carry):
        cj = (nc - 1 - ci) if rev else ci
        r0 = pl.multiple_of(cj * c, c)
        hs = range(nh)
        s = [st_ref[h] for h in hs]
        sb = [_bf(s[h]) for h in hs]
        ws = [_dot(_bf(w3[cj * nh + h]), sb[h]) for h in hs]
        qs = [_dot(_bf(q3[cj * nh + h]), sb[h]) for h in hs]
        v_new = [_bf(u3[cj * nh + h] - ws[h]) for h in hs]
        kv = [_dot1_tn(k3[cj * nh + h], v_new[h]) for h in hs]
        ov = [_dot(_bf(qk3[cj * nh + h]), v_new[h]) for h in hs]
        for h in hs:
            st_ref[h] = s[h] * jnp.exp(gt3[cj * nh + h][0:1, :]) + kv[h]
            out_ref[pl.ds(r0, c), h * hw:(h + 1) * hw] = qs[h] + ov[h]
        return carry

    lax.fori_loop(0, nc, chunk, 0)

    if final:
        o = out_ref[...] + ofwd_ref[...]
        z = z_ref[...]
        for h in range(GDN_HEADS):
            oh = _rms(o[:, h * hw:(h + 1) * hw], nw_ref[...])
            out_ref[:, h * hw:(h + 1) * hw] = oh * _silu(z[:, h * hw:(h + 1) * hw])
        out_ref[...] = _rms(out_ref[...], gn_ref[...])


def _gdn(u, conv_w, a_log, dt_bias, norm_w, gn_w, tb, seq):
    t = u.shape[0]
    nb = t // tb
    w3 = 3 * GROUP_W
    arow = jnp.zeros((1, 128), F32).at[0, SM_ALPHA:SM_ALPHA + 8].set(a_log.reshape(8))
    brow = jnp.zeros((1, 128), F32).at[0, SM_ALPHA:SM_ALPHA + 8].set(dt_bias.reshape(8))
    nbat = (tb // CHUNK) * GDN_HEADS
    wide = pltpu.VMEM((nbat, CHUNK, GDN_DK), F32)
    square = pltpu.VMEM((nbat, CHUNK, CHUNK), F32)
    scratch = [pltpu.VMEM((tb + 2 * HALO, w3), F32),
               wide, wide, wide, wide, wide, wide, square, wide, wide, square,
               pltpu.VMEM((GDN_HEADS, GDN_DK, GDN_DK), F32)]

    def call(rev, final, extra_in, extra_specs):
        prev, nxt = _halo_specs(tb, w3, COL_GDN_QKV, nb, rev, t)
        in_specs = [_main_spec(tb, w3, COL_GDN_QKV, nb, rev), prev, nxt,
                    _main_spec(tb, 128, COL_SMALL, nb, rev),
                    _const_spec((4, w3)), _const_spec((1, 128)), _const_spec((1, 128))] + extra_specs
        return pl.pallas_call(
            functools.partial(_gdn_kernel, rev, final, tb, nb, seq),
            grid=(nb,),
            in_specs=in_specs,
            out_specs=_main_spec(tb, GROUP_W, 0, nb, rev),
            out_shape=jax.ShapeDtypeStruct((t, GROUP_W), F32),
            scratch_shapes=scratch,
            compiler_params=_scan_params(),
            name="gdn_bwd" if rev else "gdn_fwd",
        )(u, u, u, u, conv_w, arow, brow, *extra_in)

    o_fwd = call(False, False, [], [])
    return call(True, True,
                [u, o_fwd, norm_w.reshape(1, -1), gn_w.reshape(1, -1)],
                [_main_spec(tb, GROUP_W, COL_GDN_Z, nb, True),
                 _main_spec(tb, GROUP_W, 0, nb, True),
                 _const_spec((1, GDN_DK)), _const_spec((1, GROUP_W))])


def _lru_kernel(rev, final, tb, nb, seq, *refs):
    if final:
        (x_ref, prev_ref, next_ref, cw_ref, cb_ref, w_ref, ba_ref, bi_ref, lam_ref,
         gate_ref, ofwd_ref, gn_ref, out_ref, xp_ref, h_ref) = refs
    else:
        (x_ref, prev_ref, next_ref, cw_ref, cb_ref, w_ref, ba_ref, bi_ref, lam_ref,
         out_ref, xp_ref, h_ref) = refs
    first, last = _block_pos(nb, tb, rev, seq)
    start = last if rev else first
    bw = LRU_BW

    _fill_pad(xp_ref, x_ref, prev_ref, next_ref, first, last, tb)
    xc = _conv4(xp_ref, cw_ref, tb) + cb_ref[...]

    @pl.when(start)
    def _():
        h_ref[...] = jnp.zeros_like(h_ref)

    rows = lax.broadcasted_iota(jnp.int32, (tb, bw), 0)
    tile_rows = rows & (SUBLANES - 1)
    start_row = (tb - 1) if rev else 0
    is_start_row = (rows == start_row) & start
    sp = _softplus(-lam_ref[...])
    for n in range(LRU_BLOCKS):
        xb = xc[:, n * bw:(n + 1) * bw]
        pre = _dot1(xb, w_ref[n])
        rg = _sig(pre[:, :bw] + ba_ref[:, n * bw:(n + 1) * bw])
        ig = _sig(pre[:, bw:] + bi_ref[:, n * bw:(n + 1) * bw])
        log_a = (-LRU_C) * rg * sp[:, n * bw:(n + 1) * bw]
        a = jnp.exp(log_a)
        mult = jnp.sqrt((1.0 + a * a) * jnp.tanh(-log_a))
        mult = jnp.where(is_start_row, 1.0, mult)
        b = mult * (ig * xb)
        s = 1
        while s < SUBLANES:
            if rev:
                a_sh = pltpu.roll(a, tb - s, 0)
                b_sh = pltpu.roll(b, tb - s, 0)
                ok = tile_rows < SUBLANES - s
            else:
                a_sh = pltpu.roll(a, s, 0)
                b_sh = pltpu.roll(b, s, 0)
                ok = tile_rows >= s
            b = jnp.where(ok, a * b_sh, 0.0) + b
            a = jnp.where(ok, a * a_sh, a)
            s *= 2
        h_prev = h_ref[:, n * bw:(n + 1) * bw]
        ntiles = tb // SUBLANES
        for ti in (range(ntiles - 1, -1, -1) if rev else range(ntiles)):
            ts = slice(ti * SUBLANES, (ti + 1) * SUBLANES)
            ht = a[ts, :] * h_prev + b[ts, :]
            out_ref[ts, n * bw:(n + 1) * bw] = ht
            h_prev = ht[0:1, :] if rev else ht[SUBLANES - 1:SUBLANES, :]
        h_ref[:, n * bw:(n + 1) * bw] = h_prev

    if final:
        o = (out_ref[...] + ofwd_ref[...]) * _gelu_tanh(gate_ref[...])
        out_ref[...] = _rms(o, gn_ref[...])


def _lru(u, conv_w, conv_b, wa, ba, wi, bi, lam, gn_w, tb, seq):
    t = u.shape[0]
    nb = t // tb
    w = GROUP_W
    wcat = jnp.concatenate([wa, wi], axis=-1).astype(BF16)
    scratch = [pltpu.VMEM((tb + 2 * HALO, w), F32), pltpu.VMEM((1, w), F32)]

    def call(rev, final, extra_in, extra_specs):
        d = 1 if rev else 0
        prev, nxt = _halo_specs(tb, w, COL_LRU_X, nb, rev, t)
        in_specs = [_main_spec(tb, w, COL_LRU_X, nb, rev), prev, nxt,
                    _const_spec((4, w)), _const_spec((1, w)),
                    _const_spec((LRU_BLOCKS, LRU_BW, 2 * LRU_BW)),
                    _const_spec((1, w)), _const_spec((1, w)), _const_spec((1, w))] + extra_specs
        return pl.pallas_call(
            functools.partial(_lru_kernel, rev, final, tb, nb, seq),
            grid=(nb,),
            in_specs=in_specs,
            out_specs=_main_spec(tb, w, 0, nb, rev),
            out_shape=jax.ShapeDtypeStruct((t, w), F32),
            scratch_shapes=scratch,
            compiler_params=_scan_params(),
            name="lru_bwd" if rev else "lru_fwd",
        )(u, u, u, conv_w, conv_b.reshape(1, w), wcat[d], ba[d].reshape(1, w),
          bi[d].reshape(1, w), lam[d].reshape(1, w), *extra_in)

    o_fwd = call(False, False, [], [])
    return call(True, True, [u, o_fwd, gn_w.reshape(1, w)],
                [_main_spec(tb, w, COL_LRU_GATE, nb, True), _main_spec(tb, w, 0, nb, True),
                 _const_spec((1, w))])


def _ssd_kernel(rev, final, tb, nb, seq, *refs):
    if final:
        (xbc_ref, prev_ref, next_ref, small_ref, cw_ref, cb_ref, arow_ref, brow_ref,
         z_ref, ofwd_ref, dskip_ref, nw_ref, gn_ref, out_ref, xp_ref, xbc_s, st_ref) = refs
    else:
        (xbc_ref, prev_ref, next_ref, small_ref, cw_ref, cb_ref, arow_ref, brow_ref,
         out_ref, xp_ref, xbc_s, st_ref) = refs
    d = 1 if rev else 0
    c = CHUNK
    nc = tb // c
    di = GROUP_W
    p = SSD_HEADDIM
    gw = di // SSD_GROUPS
    first, last = _block_pos(nb, tb, rev, seq)

    assert p == c
    hpg = SSD_HEADS // SSD_GROUPS

    _fill_pad(xp_ref, xbc_ref, prev_ref, next_ref, first, last, tb)
    xbc_s[...] = _silu(_conv4(xp_ref, cw_ref, tb) + cb_ref[...])
    dt = _softplus(small_ref[...] + brow_ref[...])
    da = dt * (-jnp.exp(arow_ref[...]))

    rb, cb_ = _iota2((tb, tb))
    same = (rb >> 6) == (cb_ >> 6)
    incl_b = same & ((cb_ >= rb) if rev else (cb_ <= rb))
    acs = _dot_mask_l(jnp.where(incl_b, 1.0, 0.0).astype(BF16), da)
    tot = _dot_mask_l(jnp.where(same, 1.0, 0.0).astype(BF16), da)
    acs_t = acs.T
    sr, sc = _iota2((128, di))
    sel = jnp.where(sr == SM_DT + d * SSD_HEADS + (sc >> 6), 1.0, 0.0).astype(BF16)
    acs_x = _dot_mask_r(acs, sel)
    tot_x = _dot_mask_r(tot, sel)
    dt_x = _dot_mask_r(dt, sel)
    xdt = xbc_s[:, 0:di] * dt_x
    xdt_e = xdt * jnp.exp(tot_x - acs_x)
    e_acs = jnp.exp(acs_x)
    e_tot = jnp.exp(tot_x)
    xdt_b = _bf(xdt)

    @pl.when(last if rev else first)
    def _():
        st_ref[...] = jnp.zeros_like(st_ref)

    rr, lj = _iota2((c, gw))
    incl_cat = ((lj & (c - 1)) >= rr) if rev else ((lj & (c - 1)) <= rr)
    rbd, cbd = _iota2((gw, gw))
    head_diag = (rbd >> 6) == (cbd >> 6)

    gss = [slice(g * gw, (g + 1) * gw) for g in range(SSD_GROUPS)]
    pairs = [(cj, g) for cj in range(nc) for g in range(SSD_GROUPS)]
    rsl = {cj: slice(cj * c, (cj + 1) * c) for cj in range(nc)}
    bm = {(cj, g): _bf(xbc_s[rsl[cj], di + g * SSD_STATE:di + (g + 1) * SSD_STATE]) for cj, g in pairs}
    cm = {(cj, g): _bf(xbc_s[rsl[cj], di + (SSD_GROUPS + g) * SSD_STATE:di + (SSD_GROUPS + g + 1) * SSD_STATE])
          for cj, g in pairs}
    cbm = {k: _dot1_nt(cm[k], bm[k]) for k in pairs}
    kx = {(cj, g): _dot1_tn(bm[(cj, g)], xdt_e[rsl[cj], gss[g]]) for cj, g in pairs}
    y_d = {}
    for cj, g in pairs:
        rs = rsl[cj]
        cb_cat = jnp.concatenate([cbm[(cj, g)]] * hpg, axis=1)
        col0 = SM_DT + d * SSD_HEADS + g * hpg
        a_row = jnp.concatenate([acs_t[col0 + hh:col0 + hh + 1, rs] for hh in range(hpg)], axis=1)
        l_cat = jnp.where(incl_cat, jnp.exp(acs_x[rs, gss[g]] - a_row), 0.0)
        x_bd = jnp.where(head_diag, jnp.concatenate([xdt_b[rs, gss[g]]] * hpg, axis=0),
                         jnp.zeros((), BF16))
        y_d[(cj, g)] = _dot1(cb_cat * l_cat, x_bd)
    s = [st_ref[g] for g in range(SSD_GROUPS)]
    for ci in range(nc):
        cj = (nc - 1 - ci) if rev else ci
        rs = rsl[cj]
        y_off = [_dot(cm[(cj, g)], _bf(s[g])) for g in range(SSD_GROUPS)]
        for g in range(SSD_GROUPS):
            out_ref[rs, gss[g]] = y_d[(cj, g)] + y_off[g] * e_acs[rs, gss[g]]
            s[g] = s[g] * e_tot[cj * c:cj * c + 1, gss[g]] + kx[(cj, g)]
    for g in range(SSD_GROUPS):
        st_ref[g] = s[g]

    if final:
        y = out_ref[...] + ofwd_ref[...] + dskip_ref[...] * xbc_s[:, 0:di]
        y = y * _silu(z_ref[...])
        for g in range(SSD_GROUPS):
            out_ref[:, g * gw:(g + 1) * gw] = _rms(y[:, g * gw:(g + 1) * gw], nw_ref[:, g * gw:(g + 1) * gw])
        out_ref[...] = _rms(out_ref[...], gn_ref[...])


def _ssd(u, conv_w, conv_b, a_log, dt_bias, d_skip, norm_w, gn_w, tb, seq):
    t = u.shape[0]
    nb = t // tb
    wx = GROUP_W + 2 * SSD_GROUPS * SSD_STATE
    arow = jnp.zeros((1, 128), F32).at[0, SM_DT:SM_DT + 16].set(a_log.reshape(16))
    brow = jnp.zeros((1, 128), F32).at[0, SM_DT:SM_DT + 16].set(dt_bias.reshape(16))
    dsk = jnp.repeat(d_skip, SSD_HEADDIM).reshape(1, GROUP_W)
    scratch = [pltpu.VMEM((tb + 2 * HALO, wx), F32), pltpu.VMEM((tb, wx), F32),
               pltpu.VMEM((SSD_GROUPS, SSD_STATE, GROUP_W // SSD_GROUPS), F32)]

    def call(rev, final, extra_in, extra_specs):
        prev, nxt = _halo_specs(tb, wx, COL_SSD_XBC, nb, rev, t)
        in_specs = [_main_spec(tb, wx, COL_SSD_XBC, nb, rev), prev, nxt,
                    _main_spec(tb, 128, COL_SMALL, nb, rev),
                    _const_spec((4, wx)), _const_spec((1, wx)),
                    _const_spec((1, 128)), _const_spec((1, 128))] + extra_specs
        return pl.pallas_call(
            functools.partial(_ssd_kernel, rev, final, tb, nb, seq),
            grid=(nb,),
            in_specs=in_specs,
            out_specs=_main_spec(tb, GROUP_W, 0, nb, rev),
            out_shape=jax.ShapeDtypeStruct((t, GROUP_W), F32),
            scratch_shapes=scratch,
            compiler_params=_scan_params(),
            name="ssd_bwd" if rev else "ssd_fwd",
        )(u, u, u, u, conv_w, conv_b.reshape(1, wx), arow, brow, *extra_in)

    o_fwd = call(False, False, [], [])
    w = GROUP_W
    return call(True, True,
                [u, o_fwd, dsk, norm_w.reshape(1, w), gn_w.reshape(1, w)],
                [_main_spec(tb, w, COL_SSD_Z, nb, True), _main_spec(tb, w, 0, nb, True),
                 _const_spec((1, w)), _const_spec((1, w)), _const_spec((1, w))])


def _hgrn_kernel(rev, final, layer, tb, nb, seq, *refs):
    if final:
        (q_ref, f_ref, i_ref, lb_ref, mask_ref, g_ref, ofwd_ref, nw_ref, gn_ref, out_ref,
         d_s, z_s, qd_s, kd_s, st_ref) = refs
    else:
        (q_ref, f_ref, i_ref, lb_ref, mask_ref, out_ref,
         d_s, z_s, qd_s, kd_s, st_ref) = refs
    c = HGRN_STATE_CHUNK
    nl = HGRN_LEVELS
    nc = tb // c
    w = GROUP_W
    hw = HGRN_DK
    first, last = _block_pos(nb, tb, rev, seq)

    lbp = lb_ref[...]
    mx = jnp.max(lbp, axis=0, keepdims=True)
    ex = jnp.exp(lbp - mx)
    den = jnp.sum(ex, axis=0, keepdims=True)
    lb = jnp.zeros((1, w), F32)
    for m in range(1, layer + 1):
        lb = lb + ex[m:m + 1, :] / den

    f = lb + (1.0 - lb) * _sig(f_ref[...])
    logf = jnp.log(f)
    kk = 1.0 - f
    qq = _silu(q_ref[...])
    vv = i_ref[...]

    l1 = _bf(logf)
    l2 = _bf(logf - l1.astype(F32))

    d_s[...] = _dot(mask_ref[...], l1) + _dot(mask_ref[...], l2)

    def range_sum(blk):
        return d_s[blk * tb:(blk + 1) * tb, :]

    rows = lax.broadcasted_iota(jnp.int32, (tb, 1), 0)
    z_s[0] = _bf(qq)
    z_s[1] = _bf(kk)
    for lv in range(1, nl + 1):
        bit = (rows >> (lv - 1)) & 1
        is_query = (bit == 0) if rev else (bit == 1)
        z_s[lv + 1] = _bf(jnp.exp(range_sum(lv - 1)) * jnp.where(is_query, qq, kk))
    bsc = range_sum(nl)
    btot = range_sum(nl + 1)
    qd_s[...] = _bf(qq * jnp.exp(bsc))
    kd_s[...] = _bf(kk * jnp.exp(btot - bsc))
    gl = jnp.exp(btot)
    vb = _bf(vv)

    r64, c64 = _iota2((c, c))
    pair_masks = [r64 == c64]
    for lv in range(1, nl + 1):
        rbit = (r64 >> (lv - 1)) & 1
        cbit = (c64 >> (lv - 1)) & 1
        split = ((rbit == 0) & (cbit == 1)) if rev else ((rbit == 1) & (cbit == 0))
        pair_masks.append(((r64 >> lv) == (c64 >> lv)) & split)

    @pl.when(last if rev else first)
    def _():
        st_ref[...] = jnp.zeros_like(st_ref)

    hs = range(HGRN_HEADS)
    sls = [slice(h * hw, (h + 1) * hw) for h in hs]
    s = [st_ref[h] for h in hs]
    for ci in range(nc):
        cj = (nc - 1 - ci) if rev else ci
        rs = slice(cj * c, (cj + 1) * c)
        prods = [[_dot1_nt(z_s[0, rs, sls[h]], z_s[1, rs, sls[h]])]
                 + [_dot1_nt(z_s[lv + 1, rs, sls[h]], z_s[lv + 1, rs, sls[h]]) for lv in range(1, nl + 1)]
                 for h in hs]
        o_st = [_dot1_nt(qd_s[rs, sls[h]], s[h]) for h in hs]
        kv = [_dot1_tn(vb[rs, sls[h]], kd_s[rs, sls[h]]) for h in hs]
        sc = []
        for h in hs:
            acc = jnp.where(pair_masks[0], prods[h][0], 0.0)
            for lv in range(1, nl + 1):
                acc = acc + jnp.where(pair_masks[lv], prods[h][lv], 0.0)
            sc.append(_bf(acc))
        y_in = [_dot(sc[h], vb[rs, sls[h]]) for h in hs]
        for h in hs:
            out_ref[rs, sls[h]] = y_in[h] + o_st[h]
            s[h] = s[h] * gl[cj * c:cj * c + 1, sls[h]] + kv[h]
    for h in hs:
        st_ref[h] = s[h]

    if final:
        o = out_ref[...] + ofwd_ref[...]
        gg = g_ref[...]
        for h in range(HGRN_HEADS):
            sl = slice(h * hw, (h + 1) * hw)
            out_ref[:, sl] = _rms(o[:, sl], nw_ref[...]) * _silu(gg[:, sl])
        out_ref[...] = _rms(out_ref[...], gn_ref[...])


def _hgrn_masks(tb, rev):
    t = np.arange(tb)[:, None]
    r = np.arange(tb)[None, :]
    blocks = []
    for lv in range(1, HGRN_LEVELS + 1):
        g = 1 << lv
        mid = (t // g) * g + g // 2
        if rev:
            m = np.where(t < mid, (r >= t) & (r < mid), (r >= mid) & (r < t))
        else:
            m = np.where(t >= mid, (r >= mid) & (r <= t), (r > t) & (r < mid))
        blocks.append(m)
    same = (r // HGRN_STATE_CHUNK) == (t // HGRN_STATE_CHUNK)
    blocks.append(same & ((r >= t) if rev else (r <= t)))
    blocks.append(same)
    return jnp.asarray(np.concatenate(blocks, axis=0).astype(np.float32), dtype=BF16)


def _hgrn(u, lb_param, layer, norm_w, gn_w, tb, seq):
    t = u.shape[0]
    nb = t // tb
    w = GROUP_W
    nblk = HGRN_LEVELS + 2
    scratch = [pltpu.VMEM((nblk * tb, w), F32),
               pltpu.VMEM((nblk, tb, w), BF16), pltpu.VMEM((tb, w), BF16),
               pltpu.VMEM((tb, w), BF16),
               pltpu.VMEM((HGRN_HEADS, HGRN_DK, HGRN_DK), F32)]

    def call(rev, final, extra_in, extra_specs):
        d = 1 if rev else 0
        in_specs = [_main_spec(tb, w, COL_HG_Q, nb, rev),
                    _main_spec(tb, w, COL_HG_F + d * w, nb, rev),
                    _main_spec(tb, w, COL_HG_I, nb, rev),
                    _const_spec((DEPTH, w)), _const_spec((nblk * tb, tb))] + extra_specs
        return pl.pallas_call(
            functools.partial(_hgrn_kernel, rev, final, layer, tb, nb, seq),
            grid=(nb,),
            in_specs=in_specs,
            out_specs=_main_spec(tb, w, 0, nb, rev),
            out_shape=jax.ShapeDtypeStruct((t, w), F32),
            scratch_shapes=scratch,
            compiler_params=_scan_params(),
            name="hgrn_bwd" if rev else "hgrn_fwd",
        )(u, u, u, lb_param[d], _hgrn_masks(tb, rev), *extra_in)

    o_fwd = call(False, False, [], [])
    return call(True, True, [u, o_fwd, norm_w.reshape(1, -1), gn_w.reshape(1, w)],
                [_main_spec(tb, w, COL_HG_G, nb, True), _main_spec(tb, w, 0, nb, True),
                 _const_spec((1, HGRN_DK)), _const_spec((1, w))])


def _outproj_kernel(x_ref, ya_ref, yb_ref, yc_ref, yd_ref, w_ref, o_ref):
    acc = x_ref[...]
    for n, y_ref in enumerate((ya_ref, yb_ref, yc_ref, yd_ref)):
        acc = acc + _dot(_bf(y_ref[...]), w_ref[n * GROUP_W:(n + 1) * GROUP_W, :])
    o_ref[...] = acc


def _outproj(x, ys, w):
    t, d = x.shape
    tm = 512
    yspec = pl.BlockSpec((tm, GROUP_W), lambda i: (i, 0))
    return pl.pallas_call(
        _outproj_kernel,
        grid=(t // tm,),
        in_specs=[pl.BlockSpec((tm, d), lambda i: (i, 0)), yspec, yspec, yspec, yspec,
                  pl.BlockSpec(w.shape, lambda i: (0, 0))],
        out_specs=pl.BlockSpec((tm, d), lambda i: (i, 0)),
        out_shape=jax.ShapeDtypeStruct((t, d), F32),
        compiler_params=pltpu.CompilerParams(dimension_semantics=("arbitrary",),
                                             vmem_limit_bytes=VMEM_LIMIT),
        name="outproj",
    )(x, *ys, w)


def _ffn_kernel(apply_final, tm, nrb, nf, seq, *refs):
    (x_ref, prev_ref, next_ref, ln_ref, wg_ref, wv_ref, cwg_ref, cwv_ref, cbg_ref, cbv_ref,
     wd_ref, fn_ref, o_ref, h_s, g_s, v_s, acc_s) = refs
    j = pl.program_id(1)
    i = pl.program_id(0)
    row0 = i * tm
    first = _seq_start(row0, seq)
    last = _seq_start(row0 + tm, seq)

    @pl.when(j == 0)
    def _():
        keep_p = jnp.where(first, 0.0, 1.0)
        keep_n = jnp.where(last, 0.0, 1.0)
        h_s[0:FHALO, :] = _bf(_rms(prev_ref[...], ln_ref[...]) * keep_p)
        h_s[FHALO:FHALO + tm, :] = _bf(_rms(x_ref[...], ln_ref[...]))
        h_s[FHALO + tm:2 * FHALO + tm, :] = _bf(_rms(next_ref[...], ln_ref[...]) * keep_n)
        acc_s[...] = jnp.zeros_like(acc_s)

    hb = h_s[...]
    g_s[...] = _dot(hb, wg_ref[...])
    v_s[...] = _dot(hb, wv_ref[...])

    def conv3(s_ref, cw_ref, cb_ref):
        acc = s_ref[FHALO - 1:FHALO - 1 + tm, :] * cw_ref[0:1, :]
        acc = acc + s_ref[FHALO:FHALO + tm, :] * cw_ref[1:2, :]
        acc = acc + s_ref[FHALO + 1:FHALO + 1 + tm, :] * cw_ref[2:3, :]
        return acc + cb_ref[...]

    act = _silu(conv3(g_s, cwg_ref, cbg_ref)) * conv3(v_s, cwv_ref, cbv_ref)
    acc_s[...] += _dot(_bf(act), wd_ref[...])

    @pl.when(j == nf - 1)
    def _():
        y = x_ref[...] + acc_s[...]
        if apply_final:
            y = _rms(y, fn_ref[...])
        o_ref[...] = y


def _ffn(x, ln, w_up, conv_w, conv_b, w_down, final_w, apply_final, seq):
    t, d = x.shape
    tm = 512
    tf = 512
    nf = D_FF // tf
    nrb = t // tm
    r = tm // FHALO
    cb = conv_b.reshape(1, 2 * D_FF)
    up_buf = pltpu.VMEM((tm + 2 * FHALO, tf), F32)
    return pl.pallas_call(
        functools.partial(_ffn_kernel, apply_final, tm, nrb, nf, seq),
        grid=(nrb, nf),
        in_specs=[pl.BlockSpec((tm, d), lambda i, j: (i, 0)),
                  pl.BlockSpec((FHALO, d), lambda i, j: (jnp.maximum(i * r - 1, 0), 0)),
                  pl.BlockSpec((FHALO, d), lambda i, j: (jnp.minimum((i + 1) * r, t // FHALO - 1), 0)),
                  pl.BlockSpec((1, d), lambda i, j: (0, 0)),
                  pl.BlockSpec((d, tf), lambda i, j: (0, j)),
                  pl.BlockSpec((d, tf), lambda i, j: (0, j + nf)),
                  pl.BlockSpec((3, tf), lambda i, j: (0, j)),
                  pl.BlockSpec((3, tf), lambda i, j: (0, j + nf)),
                  pl.BlockSpec((1, tf), lambda i, j: (0, j)),
                  pl.BlockSpec((1, tf), lambda i, j: (0, j + nf)),
                  pl.BlockSpec((tf, d), lambda i, j: (j, 0)),
                  pl.BlockSpec((1, d), lambda i, j: (0, 0))],
        out_specs=pl.BlockSpec((tm, d), lambda i, j: (i, 0)),
        out_shape=jax.ShapeDtypeStruct((t, d), F32),
        scratch_shapes=[pltpu.VMEM((tm + 2 * FHALO, d), BF16), up_buf, up_buf,
                        pltpu.VMEM((tm, d), F32)],
        compiler_params=pltpu.CompilerParams(dimension_semantics=("arbitrary", "arbitrary"),
                                             vmem_limit_bytes=VMEM_LIMIT),
        name="ffn",
    )(x, x, x, ln.reshape(1, d), w_up, w_up, conv_w, conv_w, cb, cb, w_down, final_w.reshape(1, d))


def _permute_w_in(w):
    sizes = (1536, 512, 8, 8, 512, 512, 512, 1024, 16, 512, 1024, 512, 512)
    offs = [0]
    for s in sizes:
        offs.append(offs[-1] + s)
    (qkv, z, beta, alpha, lx, lg, sz, sxbc, sdt, hq, hf, hi, hg) = [
        w[:, offs[n]:offs[n + 1]] for n in range(len(sizes))]
    padw = jnp.zeros((w.shape[0], D_IN_P - COL_SMALL - 32), w.dtype)
    return jnp.concatenate([qkv, z, sxbc, lx, lg, sz, hq, hf, hi, hg, beta, alpha, sdt, padw], axis=1)


def kernel(x_prompt, x_sample, ln1, w_in, gdn_conv_w, gdn_a_log, gdn_dt_bias, gdn_norm_w, lru_conv_w, lru_conv_b, lru_wa, lru_ba, lru_wi, lru_bi, lru_lambda, ssd_conv_w, ssd_conv_b, ssd_a_log, ssd_dt_bias, ssd_d, ssd_norm_w, hgrn_lb, hgrn_norm_w, group_norm_w, w_out, ln2, w_up, ffn_conv_w, ffn_conv_b, w_down, final_norm):
    b, s, dm = x_prompt.shape
    db, ds, _ = x_sample.shape
    seq = (b * s, s, ds)
    assert s % TB_SCAN == 0 and ds % TB_SCAN == 0 and s % TB_HGRN == 0 and ds % TB_HGRN == 0
    x = jnp.concatenate([x_prompt.reshape(b * s, dm), x_sample.reshape(db * ds, dm)], axis=0)
    depth = w_in.shape[0]
    for l in range(depth):
        u = _inproj(x, ln1[l], _permute_w_in(w_in[l]).astype(BF16))
        gn = group_norm_w[l]
        ya = _gdn(u, gdn_conv_w[l], gdn_a_log[l], gdn_dt_bias[l], gdn_norm_w[l], gn[0], TB_SCAN, seq)
        yb = _lru(u, lru_conv_w[l], lru_conv_b[l], lru_wa[l], lru_ba[l], lru_wi[l], lru_bi[l],
                  lru_lambda[l], gn[1], TB_SCAN, seq)
        yc = _ssd(u, ssd_conv_w[l], ssd_conv_b[l], ssd_a_log[l], ssd_dt_bias[l], ssd_d[l],
                  ssd_norm_w[l], gn[2], TB_SCAN, seq)
        yd = _hgrn(u, hgrn_lb, l, hgrn_norm_w[l], gn[3], TB_HGRN, seq)
        x = _outproj(x, (ya, yb, yc, yd), w_out[l].astype(BF16))
        x = _ffn(x, ln2[l], w_up[l].astype(BF16), ffn_conv_w[l], ffn_conv_b[l],
                 w_down[l].astype(BF16), final_norm, l == depth - 1, seq)
    return (x[:b * s].reshape(b, s, dm), x[b * s:].reshape(db, ds, dm))
```

```python
import functools
import math

import jax
import jax.numpy as jnp
import numpy as np
from jax import lax
from jax.experimental import pallas as pl
from jax.experimental.pallas import tpu as pltpu

F32 = jnp.float32
BF16 = jnp.bfloat16

D_MODEL = 2048
DEPTH = 2
GROUP_W = 512
GDN_HEADS = 4
GDN_DK = 128
LRU_BLOCKS = 4
LRU_BW = 128
LRU_C = 8.0
SSD_HEADS = 8
SSD_HEADDIM = 64
SSD_GROUPS = 2
SSD_STATE = 128
HGRN_HEADS = 4
HGRN_DK = 128
D_FF = 5632
CHUNK = 64
HGRN_LEVELS = 6
HGRN_STATE_CHUNK = 1 << HGRN_LEVELS
EPS = 1e-6

COL_GDN_QKV = 0
COL_GDN_Z = 1536
COL_SSD_XBC = 2048
COL_LRU_X = 3072
COL_LRU_GATE = 3584
COL_SSD_Z = 4096
COL_HG_Q = 4608
COL_HG_F = 5120
COL_HG_I = 6144
COL_HG_G = 6656
COL_SMALL = 7168
D_IN_P = 7296
SM_BETA = 0
SM_ALPHA = 8
SM_DT = 16

SUBLANES = 8
HALO = 8
FHALO = 16
VMEM_LIMIT = 56 * 1024 * 1024
TB_SCAN = 512
TB_HGRN = 256


def _sig(x):
    return 1.0 / (1.0 + jnp.exp(-x))


def _silu(x):
    return x * _sig(x)


def _softplus(x):
    return jnp.maximum(x, 0.0) + jnp.log1p(jnp.exp(-jnp.abs(x)))


def _gelu_tanh(x):
    c = math.sqrt(2.0 / math.pi)
    return 0.5 * x * (1.0 + jnp.tanh(c * (x + 0.044715 * (x * x * x))))


def _bf(x):
    return x.astype(BF16)


def _dot(a, b):
    return jnp.dot(a, b, preferred_element_type=F32)


def _dot1(a, b):
    return _dot(_bf(a), _bf(b))


def _dot1_nt(a, b):
    return lax.dot_general(_bf(a), _bf(b), (((1,), (1,)), ((), ())),
                           preferred_element_type=F32)


def _dot1_tn(a, b):
    return lax.dot_general(_bf(a), _bf(b), (((0,), (0,)), ((), ())),
                           preferred_element_type=F32)


def _dot_mask_l(m, b):
    b1 = _bf(b)
    r1 = b - b1.astype(F32)
    b2 = _bf(r1)
    b3 = _bf(r1 - b2.astype(F32))
    return _dot(m, b1) + (_dot(m, b2) + _dot(m, b3))


def _dot_mask_r(a, m):
    a1 = _bf(a)
    r1 = a - a1.astype(F32)
    a2 = _bf(r1)
    a3 = _bf(r1 - a2.astype(F32))
    return _dot(a1, m) + (_dot(a2, m) + _dot(a3, m))


def _rms(x, w):
    return x * lax.rsqrt(jnp.mean(x * x, axis=-1, keepdims=True) + EPS) * w


def _seq_start(row, seq):
    p_rows, s_len, ds_len = seq
    return jnp.where(row < p_rows, lax.rem(row, s_len) == 0,
                     lax.rem(row - p_rows, ds_len) == 0)


def _block_pos(nb, tb, rev, seq):
    i = pl.program_id(0)
    j = (nb - 1 - i) if rev else i
    row0 = j * tb
    first = _seq_start(row0, seq)
    last = _seq_start(row0 + tb, seq)
    return first, last


def _fill_pad(xp_ref, blk_ref, prev_ref, next_ref, first, last, tb):
    keep_p = jnp.where(first, 0.0, 1.0)
    keep_n = jnp.where(last, 0.0, 1.0)
    xp_ref[0:HALO, :] = prev_ref[...] * keep_p
    xp_ref[HALO:HALO + tb, :] = blk_ref[...]
    xp_ref[HALO + tb:2 * HALO + tb, :] = next_ref[...] * keep_n


def _conv4(xp_ref, w_ref, tb):
    acc = xp_ref[HALO - 2:HALO - 2 + tb, :] * w_ref[0:1, :]
    acc = acc + xp_ref[HALO - 1:HALO - 1 + tb, :] * w_ref[1:2, :]
    acc = acc + xp_ref[HALO:HALO + tb, :] * w_ref[2:3, :]
    acc = acc + xp_ref[HALO + 1:HALO + 1 + tb, :] * w_ref[3:4, :]
    return acc


def _iota2(shape):
    return (lax.broadcasted_iota(jnp.int32, shape, 0),
            lax.broadcasted_iota(jnp.int32, shape, 1))


def _tri_masks(c, rev):
    r, cc = _iota2((c, c))
    incl = (cc >= r) if rev else (cc <= r)
    strict = (cc > r) if rev else (cc < r)
    return r, cc, incl, strict


def _bdot(a, b):
    return lax.dot_general(a, b, (((2,), (1,)), ((0,), (0,))), preferred_element_type=F32)


def _bdot1_nt(a, b):
    return lax.dot_general(_bf(a), _bf(b), (((2,), (2,)), ((0,), (0,))),
                           preferred_element_type=F32)


def _bdot1(a, b):
    return _bdot(_bf(a), _bf(b))


def _unit_tri_inverse(a, r, c):
    n = a.shape[-1]
    eye = jnp.where(r == c, 1.0, 0.0)[None]
    d = jnp.where(((r >> 3) == (c >> 3))[None], a, 0.0)
    d2 = _bdot1(d, d)
    x = eye - d
    x = x + _bdot1(x, d2)
    d4 = _bdot1(d2, d2)
    x = x + _bdot1(x, d4)
    b = 16
    sh = 4
    while b <= n:
        joined = ((r >> sh) == (c >> sh)) & ((r >> (sh - 1)) != (c >> (sh - 1)))
        e = jnp.where(joined[None], a, 0.0)
        x = x - _bdot1(_bdot1(x, e), x)
        b *= 2
        sh += 1
    return x


def _row_source_specs(xs, tm):
    d = xs[0].shape[1]
    starts = []
    specs = []
    off = 0
    for x in xs:
        assert x.shape[0] % tm == 0
        nblk = x.shape[0] // tm

        def imap(i, *_, off=off, nblk=nblk):
            return (jnp.clip(i - off, 0, nblk - 1), 0)

        specs.append(pl.BlockSpec((tm, d), imap))
        starts.append(off)
        off += nblk
    return specs, starts


def _select_rows(i, refs, starts, fn):
    for n, ref in enumerate(refs):
        lo = starts[n]
        hi = starts[n + 1] if n + 1 < len(starts) else None
        cond = i >= lo if hi is None else ((i >= lo) & (i < hi))
        if len(refs) == 1:
            fn(ref)
        else:
            pl.when(cond)(functools.partial(fn, ref))


def _inproj_kernel(starts, *refs):
    nsrc = len(starts)
    x_refs = refs[:nsrc]
    ln_ref, w_ref, o_ref, h_s = refs[nsrc:]

    def norm_rows(x_ref):
        h_s[...] = _bf(_rms(x_ref[...], ln_ref[...]))

    @pl.when(pl.program_id(1) == 0)
    def _():
        _select_rows(pl.program_id(0), x_refs, starts, norm_rows)

    o_ref[...] = _dot(h_s[...], w_ref[...])


def _inproj(xs, ln, w):
    d = xs[0].shape[1]
    t = sum(x.shape[0] for x in xs)
    n = w.shape[1]
    tm = 512
    tn = n // 3
    x_specs, starts = _row_source_specs(xs, tm)
    return pl.pallas_call(
        functools.partial(_inproj_kernel, starts),
        grid=(t // tm, n // tn),
        in_specs=x_specs + [pl.BlockSpec((1, d), lambda i, j: (0, 0)),
                            pl.BlockSpec((d, tn), lambda i, j: (0, j))],
        out_specs=pl.BlockSpec((tm, tn), lambda i, j: (i, j)),
        out_shape=jax.ShapeDtypeStruct((t, n), F32),
        scratch_shapes=[pltpu.VMEM((tm, d), BF16)],
        compiler_params=pltpu.CompilerParams(
            dimension_semantics=("arbitrary", "arbitrary"),
            vmem_limit_bytes=VMEM_LIMIT),
        name="inproj",
    )(*xs, ln.reshape(1, d), w)


def _bidx(nb, rev):
    if rev:
        return lambda i: nb - 1 - i
    return lambda i: i


def _main_spec(tb, w, col, nb, rev):
    assert col % w == 0
    f = _bidx(nb, rev)
    return pl.BlockSpec((tb, w), lambda i: (f(i), col // w))


def _halo_specs(tb, w, col, nb, rev, t):
    assert col % w == 0
    f = _bidx(nb, rev)
    r = tb // HALO
    prev = pl.BlockSpec((HALO, w), lambda i: (jnp.maximum(f(i) * r - 1, 0), col // w))
    nxt = pl.BlockSpec((HALO, w), lambda i: (jnp.minimum((f(i) + 1) * r, t // HALO - 1), col // w))
    return prev, nxt


def _const_spec(shape):
    nd = len(shape)
    return pl.BlockSpec(shape, lambda i: (0,) * nd)


def _scan_params():
    return pltpu.CompilerParams(dimension_semantics=("arbitrary",),
                                vmem_limit_bytes=VMEM_LIMIT)


def _gdn_kernel(rev, final, tb, nb, seq, *refs):
    if final:
        (qkv_in_ref, small_ref, arow_ref, brow_ref,
         z_ref, ofwd_ref, nw_ref, gn_ref, out_ref,
         q3, k3, v3, bet3, gi3, gt3, gj3, w3, u3, qk3, st_ref) = refs
    else:
        (qkv_ref, prev_ref, next_ref, small_ref, cw_ref, arow_ref, brow_ref, out_ref, qkv_out_ref,
         xp_ref, q3, k3, v3, bet3, gi3, gt3, gj3, w3, u3, qk3, st_ref) = refs
    d = 1 if rev else 0
    c = CHUNK
    nc = tb // c
    nh = GDN_HEADS
    hw = GDN_DK
    kw = GDN_HEADS * GDN_DK
    first, last = _block_pos(nb, tb, rev, seq)

    if not final:
        _fill_pad(xp_ref, qkv_ref, prev_ref, next_ref, first, last, tb)
        y = _silu(_conv4(xp_ref, cw_ref, tb))
    for h in range(nh):
        if final:
            qh = qkv_in_ref[:, h * hw:(h + 1) * hw]
            kh = qkv_in_ref[:, kw + h * hw:kw + (h + 1) * hw]
            vh = qkv_in_ref[:, 2 * kw + h * hw:2 * kw + (h + 1) * hw]
        else:
            qh = y[:, h * hw:(h + 1) * hw]
            qh = qh * lax.rsqrt(jnp.sum(qh * qh, axis=-1, keepdims=True) + EPS) * (GDN_DK ** -0.5)
            kh = y[:, kw + h * hw:kw + (h + 1) * hw]
            kh = kh * lax.rsqrt(jnp.sum(kh * kh, axis=-1, keepdims=True) + EPS)
            vh = y[:, 2 * kw + h * hw:2 * kw + (h + 1) * hw]
            qkv_out_ref[:, h * hw:(h + 1) * hw] = qh
            qkv_out_ref[:, kw + h * hw:kw + (h + 1) * hw] = kh
            qkv_out_ref[:, 2 * kw + h * hw:2 * kw + (h + 1) * hw] = vh
        for ci in range(nc):
            q3[ci * nh + h] = qh[ci * c:(ci + 1) * c, :]
            k3[ci * nh + h] = kh[ci * c:(ci + 1) * c, :]
            v3[ci * nh + h] = vh[ci * c:(ci + 1) * c, :]

    sm = small_ref[...]
    beta_all = _sig(sm)
    g_all = -jnp.exp(arow_ref[...]) * _softplus(sm + brow_ref[...])
    rb, cb_ = _iota2((tb, tb))
    same = (rb >> 6) == (cb_ >> 6)
    incl_b = same & ((cb_ >= rb) if rev else (cb_ <= rb))
    gcum = _dot_mask_l(jnp.where(incl_b, 1.0, 0.0).astype(BF16), g_all)
    gtot = _dot_mask_l(jnp.where(same, 1.0, 0.0).astype(BF16), g_all)
    gcum_t = gcum.T
    for ci in range(nc):
        rows = slice(ci * c, (ci + 1) * c)
        for h in range(nh):
            cb = SM_BETA + d * nh + h
            cg = SM_ALPHA + d * nh + h
            b = ci * nh + h
            bet3[b] = jnp.broadcast_to(beta_all[rows, cb:cb + 1], (c, hw))
            gi3[b] = jnp.broadcast_to(gcum[rows, cg:cg + 1], (c, hw))
            gt3[b] = jnp.broadcast_to(gtot[rows, cg:cg + 1], (c, hw))
            gj3[b] = jnp.broadcast_to(gcum_t[cg:cg + 1, rows], (c, c))

    @pl.when(last if rev else first)
    def _():
        st_ref[...] = jnp.zeros_like(st_ref)

    r, cc, incl, strict = _tri_masks(c, rev)
    q = q3[...]
    k = k3[...]
    bet = bet3[...]
    gi = gi3[...]
    decay = jnp.where(incl[None], jnp.exp(gi[:, :, 0:c] - gj3[...]), 0.0)
    kb = k * bet
    a = jnp.where(strict[None], _bdot1_nt(kb, k) * decay, 0.0)
    tm = _unit_tri_inverse(a, r, cc)
    eg = jnp.exp(gi)
    w3[...] = _bdot1(tm, kb * eg)
    u3[...] = _bdot1(tm, v3[...] * bet)
    qk3[...] = _bdot1_nt(q, k) * decay
    q3[...] = q * eg
    k3[...] = k * jnp.exp(gt3[...] - gi)

    def chunk(ci, carry):
        cj = (nc - 1 - ci) if rev else ci
        r0 = pl.multiple_of(cj * c, c)
        hs = range(nh)
        s = [st_ref[h] for h in hs]
        sb = [_bf(s[h]) for h in hs]
        ws = [_dot(_bf(w3[cj * nh + h]), sb[h]) for h in hs]
        qs = [_dot(_bf(q3[cj * nh + h]), sb[h]) for h in hs]
        v_new = [_bf(u3[cj * nh + h] - ws[h]) for h in hs]
        kv = [_dot1_tn(k3[cj * nh + h], v_new[h]) for h in hs]
        ov = [_dot(_bf(qk3[cj * nh + h]), v_new[h]) for h in hs]
        for h in hs:
            st_ref[h] = s[h] * jnp.exp(gt3[cj * nh + h][0:1, :]) + kv[h]
            out_ref[pl.ds(r0, c), h * hw:(h + 1) * hw] = qs[h] + ov[h]
        return carry

    lax.fori_loop(0, nc, chunk, 0)

    if final:
        o = out_ref[...] + ofwd_ref[...]
        z = z_ref[...]
        for h in range(GDN_HEADS):
            oh = _rms(o[:, h * hw:(h + 1) * hw], nw_ref[...])
            out_ref[:, h * hw:(h + 1) * hw] = oh * _silu(z[:, h * hw:(h + 1) * hw])
        out_ref[...] = _rms(out_ref[...], gn_ref[...])


def _gdn(u, conv_w, a_log, dt_bias, norm_w, gn_w, tb, seq):
    t = u.shape[0]
    nb = t // tb
    w3 = 3 * GROUP_W
    arow = jnp.zeros((1, 128), F32).at[0, SM_ALPHA:SM_ALPHA + 8].set(a_log.reshape(8))
    brow = jnp.zeros((1, 128), F32).at[0, SM_ALPHA:SM_ALPHA + 8].set(dt_bias.reshape(8))
    nbat = (tb // CHUNK) * GDN_HEADS
    wide = pltpu.VMEM((nbat, CHUNK, GDN_DK), F32)
    square = pltpu.VMEM((nbat, CHUNK, CHUNK), F32)
    batch_scratch = [wide, wide, wide, wide, wide, wide, square, wide, wide, square,
                     pltpu.VMEM((GDN_HEADS, GDN_DK, GDN_DK), F32)]
    o_sds = jax.ShapeDtypeStruct((t, GROUP_W), F32)

    prev, nxt = _halo_specs(tb, w3, COL_GDN_QKV, nb, False, t)
    o_fwd, qkv_act = pl.pallas_call(
        functools.partial(_gdn_kernel, False, False, tb, nb, seq),
        grid=(nb,),
        in_specs=[_main_spec(tb, w3, COL_GDN_QKV, nb, False), prev, nxt,
                  _main_spec(tb, 128, COL_SMALL, nb, False),
                  _const_spec((4, w3)), _const_spec((1, 128)), _const_spec((1, 128))],
        out_specs=(_main_spec(tb, GROUP_W, 0, nb, False), _main_spec(tb, w3, 0, nb, False)),
        out_shape=(o_sds, jax.ShapeDtypeStruct((t, w3), F32)),
        scratch_shapes=[pltpu.VMEM((tb + 2 * HALO, w3), F32)] + batch_scratch,
        compiler_params=_scan_params(),
        name="gdn_fwd",
    )(u, u, u, u, conv_w, arow, brow)
    return pl.pallas_call(
        functools.partial(_gdn_kernel, True, True, tb, nb, seq),
        grid=(nb,),
        in_specs=[_main_spec(tb, w3, 0, nb, True), _main_spec(tb, 128, COL_SMALL, nb, True),
                  _const_spec((1, 128)), _const_spec((1, 128)),
                  _main_spec(tb, GROUP_W, COL_GDN_Z, nb, True), _main_spec(tb, GROUP_W, 0, nb, True),
                  _const_spec((1, GDN_DK)), _const_spec((1, GROUP_W))],
        out_specs=_main_spec(tb, GROUP_W, 0, nb, True),
        out_shape=o_sds,
        scratch_shapes=batch_scratch,
        compiler_params=_scan_params(),
        name="gdn_bwd",
    )(qkv_act, u, arow, brow, u, o_fwd, norm_w.reshape(1, -1), gn_w.reshape(1, -1))


def _lru_kernel(rev, final, tb, nb, seq, *refs):
    if final:
        (x_ref, prev_ref, next_ref, cw_ref, cb_ref, w_ref, ba_ref, bi_ref, lam_ref,
         gate_ref, ofwd_ref, gn_ref, out_ref, xp_ref, h_ref) = refs
    else:
        (x_ref, prev_ref, next_ref, cw_ref, cb_ref, w_ref, ba_ref, bi_ref, lam_ref,
         out_ref, xp_ref, h_ref) = refs
    first, last = _block_pos(nb, tb, rev, seq)
    start = last if rev else first
    bw = LRU_BW

    _fill_pad(xp_ref, x_ref, prev_ref, next_ref, first, last, tb)
    xc = _conv4(xp_ref, cw_ref, tb) + cb_ref[...]

    @pl.when(start)
    def _():
        h_ref[...] = jnp.zeros_like(h_ref)

    rows = lax.broadcasted_iota(jnp.int32, (tb, bw), 0)
    tile_rows = lax.broadcasted_iota(jnp.int32, (tb // SUBLANES, SUBLANES, bw), 1)
    start_row = (tb - 1) if rev else 0
    is_start_row = (rows == start_row) & start
    sp = _softplus(-lam_ref[...])
    for n in range(LRU_BLOCKS):
        xb = xc[:, n * bw:(n + 1) * bw]
        pre = _dot1(xb, w_ref[n])
        rg = _sig(pre[:, :bw] + ba_ref[:, n * bw:(n + 1) * bw])
        ig = _sig(pre[:, bw:] + bi_ref[:, n * bw:(n + 1) * bw])
        log_a = (-LRU_C) * rg * sp[:, n * bw:(n + 1) * bw]
        a = jnp.exp(log_a)
        mult = jnp.sqrt((1.0 + a * a) * jnp.tanh(-log_a))
        mult = jnp.where(is_start_row, 1.0, mult)
        b = mult * (ig * xb)
        ntiles = tb // SUBLANES
        a = a.reshape(ntiles, SUBLANES, bw)
        b = b.reshape(ntiles, SUBLANES, bw)
        s = 1
        while s < SUBLANES:
            sh = (SUBLANES - s) if rev else s
            a_sh = pltpu.roll(a, sh, 1)
            b_sh = pltpu.roll(b, sh, 1)
            ok = (tile_rows < SUBLANES - s) if rev else (tile_rows >= s)
            b = jnp.where(ok, a * b_sh, 0.0) + b
            a = jnp.where(ok, a * a_sh, a)
            s *= 2
        h_prev = h_ref[:, n * bw:(n + 1) * bw]
        for ti in (range(ntiles - 1, -1, -1) if rev else range(ntiles)):
            ht = a[ti] * h_prev + b[ti]
            out_ref[ti * SUBLANES:(ti + 1) * SUBLANES, n * bw:(n + 1) * bw] = ht
            h_prev = ht[0:1, :] if rev else ht[SUBLANES - 1:SUBLANES, :]
        h_ref[:, n * bw:(n + 1) * bw] = h_prev

    if final:
        o = (out_ref[...] + ofwd_ref[...]) * _gelu_tanh(gate_ref[...])
        out_ref[...] = _rms(o, gn_ref[...])


def _lru(u, conv_w, conv_b, wa, ba, wi, bi, lam, gn_w, tb, seq):
    t = u.shape[0]
    nb = t // tb
    w = GROUP_W
    wcat = jnp.concatenate([wa, wi], axis=-1).astype(BF16)
    scratch = [pltpu.VMEM((tb + 2 * HALO, w), F32), pltpu.VMEM((1, w), F32)]

    def call(rev, final, extra_in, extra_specs):
        d = 1 if rev else 0
        prev, nxt = _halo_specs(tb, w, COL_LRU_X, nb, rev, t)
        in_specs = [_main_spec(tb, w, COL_LRU_X, nb, rev), prev, nxt,
                    _const_spec((4, w)), _const_spec((1, w)),
                    _const_spec((LRU_BLOCKS, LRU_BW, 2 * LRU_BW)),
                    _const_spec((1, w)), _const_spec((1, w)), _const_spec((1, w))] + extra_specs
        return pl.pallas_call(
            functools.partial(_lru_kernel, rev, final, tb, nb, seq),
            grid=(nb,),
            in_specs=in_specs,
            out_specs=_main_spec(tb, w, 0, nb, rev),
            out_shape=jax.ShapeDtypeStruct((t, w), F32),
            scratch_shapes=scratch,
            compiler_params=_scan_params(),
            name="lru_bwd" if rev else "lru_fwd",
        )(u, u, u, conv_w, conv_b.reshape(1, w), wcat[d], ba[d].reshape(1, w),
          bi[d].reshape(1, w), lam[d].reshape(1, w), *extra_in)

    o_fwd = call(False, False, [], [])
    return call(True, True, [u, o_fwd, gn_w.reshape(1, w)],
                [_main_spec(tb, w, COL_LRU_GATE, nb, True), _main_spec(tb, w, 0, nb, True),
                 _const_spec((1, w))])


def _ssd_kernel(rev, final, tb, nb, seq, *refs):
    if final:
        (xbc_s, small_ref, arow_ref, brow_ref,
         z_ref, ofwd_ref, dskip_ref, nw_ref, gn_ref, out_ref, st_ref) = refs
    else:
        (xbc_ref, prev_ref, next_ref, small_ref, cw_ref, cb_ref, arow_ref, brow_ref,
         out_ref, xbc_s, xp_ref, st_ref) = refs
    d = 1 if rev else 0
    c = CHUNK
    nc = tb // c
    di = GROUP_W
    p = SSD_HEADDIM
    gw = di // SSD_GROUPS
    first, last = _block_pos(nb, tb, rev, seq)

    assert p == c
    hpg = SSD_HEADS // SSD_GROUPS

    if not final:
        _fill_pad(xp_ref, xbc_ref, prev_ref, next_ref, first, last, tb)
        xbc_s[...] = _silu(_conv4(xp_ref, cw_ref, tb) + cb_ref[...])
    dt = _softplus(small_ref[...] + brow_ref[...])
    da = dt * (-jnp.exp(arow_ref[...]))

    rb, cb_ = _iota2((tb, tb))
    same = (rb >> 6) == (cb_ >> 6)
    incl_b = same & ((cb_ >= rb) if rev else (cb_ <= rb))
    acs = _dot_mask_l(jnp.where(incl_b, 1.0, 0.0).astype(BF16), da)
    tot = _dot_mask_l(jnp.where(same, 1.0, 0.0).astype(BF16), da)
    acs_t = acs.T
    sr, sc = _iota2((128, di))
    sel = jnp.where(sr == SM_DT + d * SSD_HEADS + (sc >> 6), 1.0, 0.0).astype(BF16)
    acs_x = _dot_mask_r(acs, sel)
    tot_x = _dot_mask_r(tot, sel)
    dt_x = _dot_mask_r(dt, sel)
    xdt = xbc_s[:, 0:di] * dt_x
    xdt_e = xdt * jnp.exp(tot_x - acs_x)
    e_acs = jnp.exp(acs_x)
    e_tot = jnp.exp(tot_x)
    xdt_b = _bf(xdt)

    @pl.when(last if rev else first)
    def _():
        st_ref[...] = jnp.zeros_like(st_ref)

    rr, lj = _iota2((c, gw))
    incl_cat = ((lj & (c - 1)) >= rr) if rev else ((lj & (c - 1)) <= rr)
    rbd, cbd = _iota2((gw, gw))
    head_diag = (rbd >> 6) == (cbd >> 6)

    gss = [slice(g * gw, (g + 1) * gw) for g in range(SSD_GROUPS)]
    pairs = [(cj, g) for cj in range(nc) for g in range(SSD_GROUPS)]
    rsl = {cj: slice(cj * c, (cj + 1) * c) for cj in range(nc)}
    bm = {(cj, g): _bf(xbc_s[rsl[cj], di + g * SSD_STATE:di + (g + 1) * SSD_STATE]) for cj, g in pairs}
    cm = {(cj, g): _bf(xbc_s[rsl[cj], di + (SSD_GROUPS + g) * SSD_STATE:di + (SSD_GROUPS + g + 1) * SSD_STATE])
          for cj, g in pairs}
    cbm = {k: _dot1_nt(cm[k], bm[k]) for k in pairs}
    kx = {(cj, g): _dot1_tn(bm[(cj, g)], xdt_e[rsl[cj], gss[g]]) for cj, g in pairs}
    y_d = {}
    for cj, g in pairs:
        rs = rsl[cj]
        cb_cat = jnp.concatenate([cbm[(cj, g)]] * hpg, axis=1)
        col0 = SM_DT + d * SSD_HEADS + g * hpg
        a_row = jnp.concatenate([acs_t[col0 + hh:col0 + hh + 1, rs] for hh in range(hpg)], axis=1)
        l_cat = jnp.where(incl_cat, jnp.exp(acs_x[rs, gss[g]] - a_row), 0.0)
        x_bd = jnp.where(head_diag, jnp.concatenate([xdt_b[rs, gss[g]]] * hpg, axis=0),
                         jnp.zeros((), BF16))
        y_d[(cj, g)] = _dot1(cb_cat * l_cat, x_bd)
    s = [st_ref[g] for g in range(SSD_GROUPS)]
    for ci in range(nc):
        cj = (nc - 1 - ci) if rev else ci
        rs = rsl[cj]
        y_off = [_dot(cm[(cj, g)], _bf(s[g])) for g in range(SSD_GROUPS)]
        for g in range(SSD_GROUPS):
            out_ref[rs, gss[g]] = y_d[(cj, g)] + y_off[g] * e_acs[rs, gss[g]]
            s[g] = s[g] * e_tot[cj * c:cj * c + 1, gss[g]] + kx[(cj, g)]
    for g in range(SSD_GROUPS):
        st_ref[g] = s[g]

    if final:
        y = out_ref[...] + ofwd_ref[...] + dskip_ref[...] * xbc_s[:, 0:di]
        y = y * _silu(z_ref[...])
        for g in range(SSD_GROUPS):
            out_ref[:, g * gw:(g + 1) * gw] = _rms(y[:, g * gw:(g + 1) * gw], nw_ref[:, g * gw:(g + 1) * gw])
        out_ref[...] = _rms(out_ref[...], gn_ref[...])


def _ssd(u, conv_w, conv_b, a_log, dt_bias, d_skip, norm_w, gn_w, tb, seq):
    t = u.shape[0]
    nb = t // tb
    wx = GROUP_W + 2 * SSD_GROUPS * SSD_STATE
    arow = jnp.zeros((1, 128), F32).at[0, SM_DT:SM_DT + 16].set(a_log.reshape(16))
    brow = jnp.zeros((1, 128), F32).at[0, SM_DT:SM_DT + 16].set(dt_bias.reshape(16))
    dsk = jnp.repeat(d_skip, SSD_HEADDIM).reshape(1, GROUP_W)
    state = pltpu.VMEM((SSD_GROUPS, SSD_STATE, GROUP_W // SSD_GROUPS), F32)
    w = GROUP_W
    o_sds = jax.ShapeDtypeStruct((t, w), F32)

    prev, nxt = _halo_specs(tb, wx, COL_SSD_XBC, nb, False, t)
    o_fwd, xbc_act = pl.pallas_call(
        functools.partial(_ssd_kernel, False, False, tb, nb, seq),
        grid=(nb,),
        in_specs=[_main_spec(tb, wx, COL_SSD_XBC, nb, False), prev, nxt,
                  _main_spec(tb, 128, COL_SMALL, nb, False),
                  _const_spec((4, wx)), _const_spec((1, wx)),
                  _const_spec((1, 128)), _const_spec((1, 128))],
        out_specs=(_main_spec(tb, w, 0, nb, False), _main_spec(tb, wx, 0, nb, False)),
        out_shape=(o_sds, jax.ShapeDtypeStruct((t, wx), F32)),
        scratch_shapes=[pltpu.VMEM((tb + 2 * HALO, wx), F32), state],
        compiler_params=_scan_params(),
        name="ssd_fwd",
    )(u, u, u, u, conv_w, conv_b.reshape(1, wx), arow, brow)
    return pl.pallas_call(
        functools.partial(_ssd_kernel, True, True, tb, nb, seq),
        grid=(nb,),
        in_specs=[_main_spec(tb, wx, 0, nb, True), _main_spec(tb, 128, COL_SMALL, nb, True),
                  _const_spec((1, 128)), _const_spec((1, 128)),
                  _main_spec(tb, w, COL_SSD_Z, nb, True), _main_spec(tb, w, 0, nb, True),
                  _const_spec((1, w)), _const_spec((1, w)), _const_spec((1, w))],
        out_specs=_main_spec(tb, w, 0, nb, True),
        out_shape=o_sds,
        scratch_shapes=[state],
        compiler_params=_scan_params(),
        name="ssd_bwd",
    )(xbc_act, u, arow, brow, u, o_fwd, dsk, norm_w.reshape(1, w), gn_w.reshape(1, w))


def _hgrn_kernel(rev, final, layer, tb, nb, seq, *refs):
    if final:
        (q_ref, f_ref, i_ref, lb_ref, mask_ref, g_ref, ofwd_ref, nw_ref, gn_ref, out_ref,
         d_s, z_s, qd_s, kd_s, st_ref) = refs
    else:
        (q_ref, f_ref, i_ref, lb_ref, mask_ref, out_ref,
         d_s, z_s, qd_s, kd_s, st_ref) = refs
    c = HGRN_STATE_CHUNK
    nl = HGRN_LEVELS
    nc = tb // c
    w = GROUP_W
    hw = HGRN_DK
    first, last = _block_pos(nb, tb, rev, seq)

    lbp = lb_ref[...]
    mx = jnp.max(lbp, axis=0, keepdims=True)
    ex = jnp.exp(lbp - mx)
    den = jnp.sum(ex, axis=0, keepdims=True)
    lb = jnp.zeros((1, w), F32)
    for m in range(1, layer + 1):
        lb = lb + ex[m:m + 1, :] / den

    f = lb + (1.0 - lb) * _sig(f_ref[...])
    logf = jnp.log(f)
    kk = 1.0 - f
    qq = _silu(q_ref[...])
    vv = i_ref[...]

    l1 = _bf(logf)
    l2 = _bf(logf - l1.astype(F32))

    d_s[...] = _dot(mask_ref[...], l1) + _dot(mask_ref[...], l2)

    def range_sum(blk):
        return d_s[blk * tb:(blk + 1) * tb, :]

    rows = lax.broadcasted_iota(jnp.int32, (tb, 1), 0)
    z_s[0] = _bf(qq)
    z_s[1] = _bf(kk)
    for lv in range(1, nl + 1):
        bit = (rows >> (lv - 1)) & 1
        is_query = (bit == 0) if rev else (bit == 1)
        z_s[lv + 1] = _bf(jnp.exp(range_sum(lv - 1)) * jnp.where(is_query, qq, kk))
    bsc = range_sum(nl)
    btot = range_sum(nl + 1)
    qd_s[...] = _bf(qq * jnp.exp(bsc))
    kd_s[...] = _bf(kk * jnp.exp(btot - bsc))
    gl = jnp.exp(btot)
    vb = _bf(vv)

    r64, c64 = _iota2((c, c))
    pair_masks = [r64 == c64]
    for lv in range(1, nl + 1):
        rbit = (r64 >> (lv - 1)) & 1
        cbit = (c64 >> (lv - 1)) & 1
        split = ((rbit == 0) & (cbit == 1)) if rev else ((rbit == 1) & (cbit == 0))
        pair_masks.append(((r64 >> lv) == (c64 >> lv)) & split)

    @pl.when(last if rev else first)
    def _():
        st_ref[...] = jnp.zeros_like(st_ref)

    hs = range(HGRN_HEADS)
    sls = [slice(h * hw, (h + 1) * hw) for h in hs]
    s = [st_ref[h] for h in hs]
    for ci in range(nc):
        cj = (nc - 1 - ci) if rev else ci
        rs = slice(cj * c, (cj + 1) * c)
        prods = [[_dot1_nt(z_s[0, rs, sls[h]], z_s[1, rs, sls[h]])]
                 + [_dot1_nt(z_s[lv + 1, rs, sls[h]], z_s[lv + 1, rs, sls[h]]) for lv in range(1, nl + 1)]
                 for h in hs]
        o_st = [_dot1_nt(qd_s[rs, sls[h]], s[h]) for h in hs]
        kv = [_dot1_tn(vb[rs, sls[h]], kd_s[rs, sls[h]]) for h in hs]
        sc = []
        for h in hs:
            acc = jnp.where(pair_masks[0], prods[h][0], 0.0)
            for lv in range(1, nl + 1):
                acc = acc + jnp.where(pair_masks[lv], prods[h][lv], 0.0)
            sc.append(_bf(acc))
        y_in = [_dot(sc[h], vb[rs, sls[h]]) for h in hs]
        for h in hs:
            out_ref[rs, sls[h]] = y_in[h] + o_st[h]
            s[h] = s[h] * gl[cj * c:cj * c + 1, sls[h]] + kv[h]
    for h in hs:
        st_ref[h] = s[h]

    if final:
        o = out_ref[...] + ofwd_ref[...]
        gg = g_ref[...]
        for h in range(HGRN_HEADS):
            sl = slice(h * hw, (h + 1) * hw)
            out_ref[:, sl] = _rms(o[:, sl], nw_ref[...]) * _silu(gg[:, sl])
        out_ref[...] = _rms(out_ref[...], gn_ref[...])


def _hgrn_masks(tb, rev):
    t = np.arange(tb)[:, None]
    r = np.arange(tb)[None, :]
    blocks = []
    for lv in range(1, HGRN_LEVELS + 1):
        g = 1 << lv
        mid = (t // g) * g + g // 2
        if rev:
            m = np.where(t < mid, (r >= t) & (r < mid), (r >= mid) & (r < t))
        else:
            m = np.where(t >= mid, (r >= mid) & (r <= t), (r > t) & (r < mid))
        blocks.append(m)
    same = (r // HGRN_STATE_CHUNK) == (t // HGRN_STATE_CHUNK)
    blocks.append(same & ((r >= t) if rev else (r <= t)))
    blocks.append(same)
    return jnp.asarray(np.concatenate(blocks, axis=0).astype(np.float32), dtype=BF16)


def _hgrn(u, lb_param, layer, norm_w, gn_w, tb, seq):
    t = u.shape[0]
    nb = t // tb
    w = GROUP_W
    nblk = HGRN_LEVELS + 2
    scratch = [pltpu.VMEM((nblk * tb, w), F32),
               pltpu.VMEM((nblk, tb, w), BF16), pltpu.VMEM((tb, w), BF16),
               pltpu.VMEM((tb, w), BF16),
               pltpu.VMEM((HGRN_HEADS, HGRN_DK, HGRN_DK), F32)]

    def call(rev, final, extra_in, extra_specs):
        d = 1 if rev else 0
        in_specs = [_main_spec(tb, w, COL_HG_Q, nb, rev),
                    _main_spec(tb, w, COL_HG_F + d * w, nb, rev),
                    _main_spec(tb, w, COL_HG_I, nb, rev),
                    _const_spec((DEPTH, w)), _const_spec((nblk * tb, tb))] + extra_specs
        return pl.pallas_call(
            functools.partial(_hgrn_kernel, rev, final, layer, tb, nb, seq),
            grid=(nb,),
            in_specs=in_specs,
            out_specs=_main_spec(tb, w, 0, nb, rev),
            out_shape=jax.ShapeDtypeStruct((t, w), F32),
            scratch_shapes=scratch,
            compiler_params=_scan_params(),
            name="hgrn_bwd" if rev else "hgrn_fwd",
        )(u, u, u, lb_param[d], _hgrn_masks(tb, rev), *extra_in)

    o_fwd = call(False, False, [], [])
    return call(True, True, [u, o_fwd, norm_w.reshape(1, -1), gn_w.reshape(1, w)],
                [_main_spec(tb, w, COL_HG_G, nb, True), _main_spec(tb, w, 0, nb, True),
                 _const_spec((1, HGRN_DK)), _const_spec((1, w))])


def _outproj_kernel(starts, *refs):
    nsrc = len(starts)
    x_refs = refs[:nsrc]
    ya_ref, yb_ref, yc_ref, yd_ref, w_ref, o_ref = refs[nsrc:]
    acc = None
    for n, y_ref in enumerate((ya_ref, yb_ref, yc_ref, yd_ref)):
        part = _dot(_bf(y_ref[...]), w_ref[n * GROUP_W:(n + 1) * GROUP_W, :])
        acc = part if acc is None else acc + part

    def add_residual(x_ref):
        o_ref[...] = x_ref[...] + acc

    _select_rows(pl.program_id(0), x_refs, starts, add_residual)


def _outproj(xs, ys, w):
    d = xs[0].shape[1]
    t = sum(x.shape[0] for x in xs)
    tm = 512
    x_specs, starts = _row_source_specs(xs, tm)
    yspec = pl.BlockSpec((tm, GROUP_W), lambda i: (i, 0))
    return pl.pallas_call(
        functools.partial(_outproj_kernel, starts),
        grid=(t // tm,),
        in_specs=x_specs + [yspec, yspec, yspec, yspec, pl.BlockSpec(w.shape, lambda i: (0, 0))],
        out_specs=pl.BlockSpec((tm, d), lambda i: (i, 0)),
        out_shape=jax.ShapeDtypeStruct((t, d), F32),
        compiler_params=pltpu.CompilerParams(dimension_semantics=("arbitrary",),
                                             vmem_limit_bytes=VMEM_LIMIT),
        name="outproj",
    )(*xs, *ys, w)


def _ffn_kernel(apply_final, tm, nrb, nf, seq, out_starts, *refs):
    (x_ref, prev_ref, next_ref, ln_ref, wg_ref, wv_ref, cwg_ref, cwv_ref, cbg_ref, cbv_ref,
     wd_ref, fn_ref) = refs[:12]
    o_refs = refs[12:12 + len(out_starts)]
    h_s, g_s, v_s, acc_s = refs[12 + len(out_starts):]
    j = pl.program_id(1)
    i = pl.program_id(0)
    row0 = i * tm
    first = _seq_start(row0, seq)
    last = _seq_start(row0 + tm, seq)

    @pl.when(j == 0)
    def _():
        keep_p = jnp.where(first, 0.0, 1.0)
        keep_n = jnp.where(last, 0.0, 1.0)
        h_s[0:FHALO, :] = _bf(_rms(prev_ref[...], ln_ref[...]) * keep_p)
        h_s[FHALO:FHALO + tm, :] = _bf(_rms(x_ref[...], ln_ref[...]))
        h_s[FHALO + tm:2 * FHALO + tm, :] = _bf(_rms(next_ref[...], ln_ref[...]) * keep_n)
        acc_s[...] = jnp.zeros_like(acc_s)

    hb = h_s[...]
    g_s[...] = _dot(hb, wg_ref[...])
    v_s[...] = _dot(hb, wv_ref[...])

    def conv3(s_ref, cw_ref, cb_ref):
        acc = s_ref[FHALO - 1:FHALO - 1 + tm, :] * cw_ref[0:1, :]
        acc = acc + s_ref[FHALO:FHALO + tm, :] * cw_ref[1:2, :]
        acc = acc + s_ref[FHALO + 1:FHALO + 1 + tm, :] * cw_ref[2:3, :]
        return acc + cb_ref[...]

    act = _silu(conv3(g_s, cwg_ref, cbg_ref)) * conv3(v_s, cwv_ref, cbv_ref)
    acc_s[...] += _dot(_bf(act), wd_ref[...])

    @pl.when(j == nf - 1)
    def _():
        y = x_ref[...] + acc_s[...]
        if apply_final:
            y = _rms(y, fn_ref[...])

        def write(o_ref):
            o_ref[...] = y

        _select_rows(i, o_refs, out_starts, write)


def _ffn(x, ln, w_up, conv_w, conv_b, w_down, final_w, apply_final, seq, out_rows):
    t, d = x.shape
    tm = 512
    tf = 512
    nf = D_FF // tf
    nrb = t // tm
    r = tm // FHALO
    cb = conv_b.reshape(1, 2 * D_FF)
    up_buf = pltpu.VMEM((tm + 2 * FHALO, tf), F32)
    assert sum(out_rows) == t
    out_specs, out_starts = _row_source_specs([jax.ShapeDtypeStruct((n, d), F32) for n in out_rows], tm)
    return pl.pallas_call(
        functools.partial(_ffn_kernel, apply_final, tm, nrb, nf, seq, out_starts),
        grid=(nrb, nf),
        in_specs=[pl.BlockSpec((tm, d), lambda i, j: (i, 0)),
                  pl.BlockSpec((FHALO, d), lambda i, j: (jnp.maximum(i * r - 1, 0), 0)),
                  pl.BlockSpec((FHALO, d), lambda i, j: (jnp.minimum((i + 1) * r, t // FHALO - 1), 0)),
                  pl.BlockSpec((1, d), lambda i, j: (0, 0)),
                  pl.BlockSpec((d, tf), lambda i, j: (0, j)),
                  pl.BlockSpec((d, tf), lambda i, j: (0, j + nf)),
                  pl.BlockSpec((3, tf), lambda i, j: (0, j)),
                  pl.BlockSpec((3, tf), lambda i, j: (0, j + nf)),
                  pl.BlockSpec((1, tf), lambda i, j: (0, j)),
                  pl.BlockSpec((1, tf), lambda i, j: (0, j + nf)),
                  pl.BlockSpec((tf, d), lambda i, j: (j, 0)),
                  pl.BlockSpec((1, d), lambda i, j: (0, 0))],
        out_specs=tuple(out_specs),
        out_shape=tuple(jax.ShapeDtypeStruct((n, d), F32) for n in out_rows),
        scratch_shapes=[pltpu.VMEM((tm + 2 * FHALO, d), BF16), up_buf, up_buf,
                        pltpu.VMEM((tm, d), F32)],
        compiler_params=pltpu.CompilerParams(dimension_semantics=("arbitrary", "arbitrary"),
                                             vmem_limit_bytes=VMEM_LIMIT),
        name="ffn",
    )(x, x, x, ln.reshape(1, d), w_up, w_up, conv_w, conv_w, cb, cb, w_down, final_w.reshape(1, d))


def _permute_w_in(w):
    sizes = (1536, 512, 8, 8, 512, 512, 512, 1024, 16, 512, 1024, 512, 512)
    offs = [0]
    for s in sizes:
        offs.append(offs[-1] + s)
    (qkv, z, beta, alpha, lx, lg, sz, sxbc, sdt, hq, hf, hi, hg) = [
        w[:, offs[n]:offs[n + 1]] for n in range(len(sizes))]
    padw = jnp.zeros((w.shape[0], D_IN_P - COL_SMALL - 32), w.dtype)
    return jnp.concatenate([qkv, z, sxbc, lx, lg, sz, hq, hf, hi, hg, beta, alpha, sdt, padw], axis=1)


def kernel(x_prompt, x_sample, ln1, w_in, gdn_conv_w, gdn_a_log, gdn_dt_bias, gdn_norm_w, lru_conv_w, lru_conv_b, lru_wa, lru_ba, lru_wi, lru_bi, lru_lambda, ssd_conv_w, ssd_conv_b, ssd_a_log, ssd_dt_bias, ssd_d, ssd_norm_w, hgrn_lb, hgrn_norm_w, group_norm_w, w_out, ln2, w_up, ffn_conv_w, ffn_conv_b, w_down, final_norm):
    b, s, dm = x_prompt.shape
    db, ds, _ = x_sample.shape
    seq = (b * s, s, ds)
    assert s % TB_SCAN == 0 and ds % TB_SCAN == 0 and s % TB_HGRN == 0 and ds % TB_HGRN == 0
    xs = [x_prompt.reshape(b * s, dm), x_sample.reshape(db * ds, dm)]
    depth = w_in.shape[0]
    for l in range(depth):
        last_layer = l == depth - 1
        u = _inproj(xs, ln1[l], _permute_w_in(w_in[l]).astype(BF16))
        gn = group_norm_w[l]
        ya = _gdn(u, gdn_conv_w[l], gdn_a_log[l], gdn_dt_bias[l], gdn_norm_w[l], gn[0], TB_SCAN, seq)
        yb = _lru(u, lru_conv_w[l], lru_conv_b[l], lru_wa[l], lru_ba[l], lru_wi[l], lru_bi[l],
                  lru_lambda[l], gn[1], TB_SCAN, seq)
        yc = _ssd(u, ssd_conv_w[l], ssd_conv_b[l], ssd_a_log[l], ssd_dt_bias[l], ssd_d[l],
                  ssd_norm_w[l], gn[2], TB_SCAN, seq)
        yd = _hgrn(u, hgrn_lb, l, hgrn_norm_w[l], gn[3], TB_HGRN, seq)
        x = _outproj(xs, (ya, yb, yc, yd), w_out[l].astype(BF16))
        out_rows = [b * s, db * ds] if last_layer else [b * s + db * ds]
        xs = list(_ffn(x, ln2[l], w_up[l].astype(BF16), ffn_conv_w[l], ffn_conv_b[l],
                       w_down[l].astype(BF16), final_norm, last_layer, seq, out_rows))
    return (xs[0].reshape(b, s, dm), xs[1].reshape(db, ds, dm))
```

```python
import functools
import math

import jax
import jax.numpy as jnp
import numpy as np
from jax import lax
from jax.experimental import pallas as pl
from jax.experimental.pallas import tpu as pltpu

F32 = jnp.float32
BF16 = jnp.bfloat16

D_MODEL = 2048
DEPTH = 2
GROUP_W = 512
GDN_HEADS = 4
GDN_DK = 128
LRU_BLOCKS = 4
LRU_BW = 128
LRU_C = 8.0
SSD_HEADS = 8
SSD_HEADDIM = 64
SSD_GROUPS = 2
SSD_STATE = 128
HGRN_HEADS = 4
HGRN_DK = 128
D_FF = 5632
CHUNK = 64
HGRN_LEVELS = 6
HGRN_STATE_CHUNK = 1 << HGRN_LEVELS
EPS = 1e-6

COL_GDN_QKV = 0
COL_GDN_Z = 1536
COL_SSD_XBC = 2048
COL_LRU_X = 3072
COL_LRU_GATE = 3584
COL_SSD_Z = 4096
COL_HG_Q = 4608
COL_HG_F = 5120
COL_HG_I = 6144
COL_HG_G = 6656
COL_SMALL = 7168
D_IN_P = 7296
SM_BETA = 0
SM_ALPHA = 8
SM_DT = 16

SUBLANES = 8
HALO = 8
FHALO = 16
VMEM_LIMIT = 56 * 1024 * 1024
TB_SCAN = 512
TB_HGRN = 256

def _sig(x):
    return 1.0 / (1.0 + jnp.exp(-x))


def _silu(x):
    return x * _sig(x)


def _softplus(x):
    return jnp.maximum(x, 0.0) + jnp.log1p(jnp.exp(-jnp.abs(x)))


def _gelu_tanh(x):
    c = math.sqrt(2.0 / math.pi)
    return 0.5 * x * (1.0 + jnp.tanh(c * (x + 0.044715 * (x * x * x))))


def _bf(x):
    return x.astype(BF16)


def _dot(a, b):
    return jnp.dot(a, b, preferred_element_type=F32)


def _dot1(a, b):
    return _dot(_bf(a), _bf(b))


def _dot1_nt(a, b):
    return lax.dot_general(_bf(a), _bf(b), (((1,), (1,)), ((), ())),
                           preferred_element_type=F32)


def _dot1_tn(a, b):
    return lax.dot_general(_bf(a), _bf(b), (((0,), (0,)), ((), ())),
                           preferred_element_type=F32)


def _dot_mask_l(m, b):
    b1 = _bf(b)
    r1 = b - b1.astype(F32)
    b2 = _bf(r1)
    b3 = _bf(r1 - b2.astype(F32))
    return _dot(m, b1) + (_dot(m, b2) + _dot(m, b3))


def _dot_mask_r(a, m):
    a1 = _bf(a)
    r1 = a - a1.astype(F32)
    a2 = _bf(r1)
    a3 = _bf(r1 - a2.astype(F32))
    return _dot(a1, m) + (_dot(a2, m) + _dot(a3, m))


def _rms(x, w):
    return x * lax.rsqrt(jnp.mean(x * x, axis=-1, keepdims=True) + EPS) * w


def _seq_start(row, seq):
    p_rows, s_len, ds_len = seq
    return jnp.where(row < p_rows, lax.rem(row, s_len) == 0,
                     lax.rem(row - p_rows, ds_len) == 0)


def _block_pos(nb, tb, rev, seq):
    i = pl.program_id(0)
    j = (nb - 1 - i) if rev else i
    row0 = j * tb
    first = _seq_start(row0, seq)
    last = _seq_start(row0 + tb, seq)
    return first, last


def _fill_pad(xp_ref, blk_ref, prev_ref, next_ref, first, last, tb):
    keep_p = jnp.where(first, 0.0, 1.0)
    keep_n = jnp.where(last, 0.0, 1.0)
    xp_ref[0:HALO, :] = prev_ref[...] * keep_p
    xp_ref[HALO:HALO + tb, :] = blk_ref[...]
    xp_ref[HALO + tb:2 * HALO + tb, :] = next_ref[...] * keep_n


def _conv4(xp_ref, w_ref, tb):
    acc = xp_ref[HALO - 2:HALO - 2 + tb, :] * w_ref[0:1, :]
    acc = acc + xp_ref[HALO - 1:HALO - 1 + tb, :] * w_ref[1:2, :]
    acc = acc + xp_ref[HALO:HALO + tb, :] * w_ref[2:3, :]
    acc = acc + xp_ref[HALO + 1:HALO + 1 + tb, :] * w_ref[3:4, :]
    return acc


def _iota2(shape):
    return (lax.broadcasted_iota(jnp.int32, shape, 0),
            lax.broadcasted_iota(jnp.int32, shape, 1))


def _bdot(a, b):
    return lax.dot_general(a, b, (((2,), (1,)), ((0,), (0,))), preferred_element_type=F32)


def _bdot1_nt(a, b):
    return lax.dot_general(_bf(a), _bf(b), (((2,), (2,)), ((0,), (0,))),
                           preferred_element_type=F32)


def _bdot1(a, b):
    return _bdot(_bf(a), _bf(b))


def _unit_tri_inverse(a, r, c, n):
    eye = jnp.where(r == c, 1.0, 0.0)[None]
    d = jnp.where(((r >> 3) == (c >> 3))[None], a, 0.0)
    d2 = _bdot1(d, d)
    x = eye - d
    x = x + _bdot1(x, d2)
    d4 = _bdot1(d2, d2)
    x = x + _bdot1(x, d4)
    b = 16
    sh = 4
    while b <= n:
        joined = ((r >> sh) == (c >> sh)) & ((r >> (sh - 1)) != (c >> (sh - 1)))
        e = jnp.where(joined[None], a, 0.0)
        x = x - _bdot1(_bdot1(x, e), x)
        b *= 2
        sh += 1
    return x


def _row_source_specs(xs, tm):
    d = xs[0].shape[1]
    starts = []
    specs = []
    off = 0
    for x in xs:
        assert x.shape[0] % tm == 0
        nblk = x.shape[0] // tm

        def imap(i, *_, off=off, nblk=nblk):
            return (jnp.clip(i - off, 0, nblk - 1), 0)

        specs.append(pl.BlockSpec((tm, d), imap))
        starts.append(off)
        off += nblk
    return specs, starts


def _select_rows(i, refs, starts, fn):
    for n, ref in enumerate(refs):
        lo = starts[n]
        hi = starts[n + 1] if n + 1 < len(starts) else None
        cond = i >= lo if hi is None else ((i >= lo) & (i < hi))
        if len(refs) == 1:
            fn(ref)
        else:
            pl.when(cond)(functools.partial(fn, ref))


def _inproj_kernel(starts, *refs):
    nsrc = len(starts)
    x_refs = refs[:nsrc]
    ln_ref, w_ref, o_ref, h_s = refs[nsrc:]

    def norm_rows(x_ref):
        h_s[...] = _bf(_rms(x_ref[...], ln_ref[...]))

    @pl.when(pl.program_id(1) == 0)
    def _():
        _select_rows(pl.program_id(0), x_refs, starts, norm_rows)

    o_ref[...] = _dot(h_s[...], w_ref[...])


def _inproj(xs, ln, w):
    d = xs[0].shape[1]
    t = sum(x.shape[0] for x in xs)
    n = w.shape[1]
    tm = 512
    tn = n // 3
    x_specs, starts = _row_source_specs(xs, tm)
    return pl.pallas_call(
        functools.partial(_inproj_kernel, starts),
        grid=(t // tm, n // tn),
        in_specs=x_specs + [pl.BlockSpec((1, d), lambda i, j: (0, 0)),
                            pl.BlockSpec((d, tn), lambda i, j: (0, j))],
        out_specs=pl.BlockSpec((tm, tn), lambda i, j: (i, j)),
        out_shape=jax.ShapeDtypeStruct((t, n), F32),
        scratch_shapes=[pltpu.VMEM((tm, d), BF16)],
        compiler_params=pltpu.CompilerParams(
            dimension_semantics=("arbitrary", "arbitrary"),
            vmem_limit_bytes=VMEM_LIMIT),
        name="inproj",
    )(*xs, ln.reshape(1, d), w)


def _bidx(nb, rev):
    if rev:
        return lambda i: nb - 1 - i
    return lambda i: i


def _main_spec(tb, w, col, nb, rev):
    assert col % w == 0
    f = _bidx(nb, rev)
    return pl.BlockSpec((tb, w), lambda i: (f(i), col // w))


def _halo_specs(tb, w, col, nb, rev, t):
    assert col % w == 0
    f = _bidx(nb, rev)
    r = tb // HALO
    prev = pl.BlockSpec((HALO, w), lambda i: (jnp.maximum(f(i) * r - 1, 0), col // w))
    nxt = pl.BlockSpec((HALO, w), lambda i: (jnp.minimum((f(i) + 1) * r, t // HALO - 1), col // w))
    return prev, nxt


def _const_spec(shape):
    nd = len(shape)
    return pl.BlockSpec(shape, lambda i: (0,) * nd)


def _scan_params():
    return pltpu.CompilerParams(dimension_semantics=("arbitrary",),
                                vmem_limit_bytes=VMEM_LIMIT)


def _gdn_kernel(rev, final, tb, nb, seq, *refs):
    if final:
        (qkv_in_ref, small_ref, arow_ref, brow_ref,
         z_ref, ofwd_ref, nw_ref, gn_ref, out_ref,
         q3, k3, v3, bet3, gi3, gt3, gj3, w3, u3, qk3, st_ref) = refs
    else:
        (qkv_ref, prev_ref, next_ref, small_ref, cw_ref, arow_ref, brow_ref, out_ref, qkv_out_ref,
         xp_ref, q3, k3, v3, bet3, gi3, gt3, gj3, w3, u3, qk3, st_ref) = refs
    d = 1 if rev else 0
    c = CHUNK
    c2 = 2 * CHUNK
    npair = tb // c2
    nh = GDN_HEADS
    hw = GDN_DK
    kw = GDN_HEADS * GDN_DK
    first, last = _block_pos(nb, tb, rev, seq)

    if not final:
        _fill_pad(xp_ref, qkv_ref, prev_ref, next_ref, first, last, tb)
        y = _silu(_conv4(xp_ref, cw_ref, tb))
    for h in range(nh):
        if final:
            qh = qkv_in_ref[:, h * hw:(h + 1) * hw]
            kh = qkv_in_ref[:, kw + h * hw:kw + (h + 1) * hw]
            vh = qkv_in_ref[:, 2 * kw + h * hw:2 * kw + (h + 1) * hw]
        else:
            qh = y[:, h * hw:(h + 1) * hw]
            qh = qh * lax.rsqrt(jnp.sum(qh * qh, axis=-1, keepdims=True) + EPS) * (GDN_DK ** -0.5)
            kh = y[:, kw + h * hw:kw + (h + 1) * hw]
            kh = kh * lax.rsqrt(jnp.sum(kh * kh, axis=-1, keepdims=True) + EPS)
            vh = y[:, 2 * kw + h * hw:2 * kw + (h + 1) * hw]
            qkv_out_ref[:, h * hw:(h + 1) * hw] = qh
            qkv_out_ref[:, kw + h * hw:kw + (h + 1) * hw] = kh
            qkv_out_ref[:, 2 * kw + h * hw:2 * kw + (h + 1) * hw] = vh
        for pi in range(npair):
            q3[pi * nh + h] = qh[pi * c2:(pi + 1) * c2, :]
            k3[pi * nh + h] = kh[pi * c2:(pi + 1) * c2, :]
            v3[pi * nh + h] = vh[pi * c2:(pi + 1) * c2, :]

    sm = small_ref[...]
    beta_all = _sig(sm)
    g_all = -jnp.exp(arow_ref[...]) * _softplus(sm + brow_ref[...])
    rb, cb_ = _iota2((tb, tb))
    same = (rb >> 6) == (cb_ >> 6)
    incl_b = same & ((cb_ >= rb) if rev else (cb_ <= rb))
    gcum = _dot_mask_l(jnp.where(incl_b, 1.0, 0.0).astype(BF16), g_all)
    gtot = _dot_mask_l(jnp.where(same, 1.0, 0.0).astype(BF16), g_all)
    gcum_t = gcum.T
    for pi in range(npair):
        rows = slice(pi * c2, (pi + 1) * c2)
        for h in range(nh):
            cb = SM_BETA + d * nh + h
            cg = SM_ALPHA + d * nh + h
            b = pi * nh + h
            bet3[b] = jnp.broadcast_to(beta_all[rows, cb:cb + 1], (c2, hw))
            gi3[b] = jnp.broadcast_to(gcum[rows, cg:cg + 1], (c2, hw))
            gt3[b] = jnp.broadcast_to(gtot[rows, cg:cg + 1], (c2, hw))
            gj3[b] = jnp.broadcast_to(gcum_t[cg:cg + 1, rows], (c2, c2))

    @pl.when(last if rev else first)
    def _():
        st_ref[...] = jnp.zeros_like(st_ref)

    r, cc = _iota2((c2, c2))
    same_chunk = (r >> 6) == (cc >> 6)
    incl = same_chunk & ((cc >= r) if rev else (cc <= r))
    strict = same_chunk & ((cc > r) if rev else (cc < r))
    q = q3[...]
    k = k3[...]
    bet = bet3[...]
    gi = gi3[...]
    decay = jnp.where(incl[None], jnp.exp(gi - gj3[...]), 0.0)
    kb = k * bet
    a = jnp.where(strict[None], _bdot1_nt(kb, k) * decay, 0.0)
    tm = _unit_tri_inverse(a, r, cc, c)
    eg = jnp.exp(gi)
    w3[...] = _bdot1(tm, kb * eg)
    u3[...] = _bdot1(tm, v3[...] * bet)
    qk3[...] = _bdot1_nt(q, k) * decay
    q3[...] = q * eg
    k3[...] = k * jnp.exp(gt3[...] - gi)

    def chunk_pair(pi, carry):
        pj = (npair - 1 - pi) if rev else pi
        hs = range(nh)
        for half in ((1, 0) if rev else (0, 1)):
            r0 = pl.multiple_of(pj * c2 + half * c, c)
            hr = slice(half * c, (half + 1) * c)
            s = [st_ref[h] for h in hs]
            sb = [_bf(s[h]) for h in hs]
            ws = [_dot(_bf(w3[pj * nh + h, hr, :]), sb[h]) for h in hs]
            qs = [_dot(_bf(q3[pj * nh + h, hr, :]), sb[h]) for h in hs]
            v_new = [_bf(u3[pj * nh + h, hr, :] - ws[h]) for h in hs]
            kv = [_dot1_tn(k3[pj * nh + h, hr, :], v_new[h]) for h in hs]
            ov = [_dot(_bf(qk3[pj * nh + h, hr, hr]), v_new[h]) for h in hs]
            for h in hs:
                st_ref[h] = s[h] * jnp.exp(gt3[pj * nh + h, half * c:half * c + 1, :]) + kv[h]
                out_ref[pl.ds(r0, c), h * hw:(h + 1) * hw] = qs[h] + ov[h]
        return carry

    lax.fori_loop(0, npair, chunk_pair, 0)

    if final:
        o = out_ref[...] + ofwd_ref[...]
        z = z_ref[...]
        for h in range(GDN_HEADS):
            oh = _rms(o[:, h * hw:(h + 1) * hw], nw_ref[...])
            out_ref[:, h * hw:(h + 1) * hw] = oh * _silu(z[:, h * hw:(h + 1) * hw])
        out_ref[...] = _rms(out_ref[...], gn_ref[...])


def _gdn(u, conv_w, a_log, dt_bias, norm_w, gn_w, tb, seq):
    t = u.shape[0]
    nb = t // tb
    w3 = 3 * GROUP_W
    arow = jnp.zeros((1, 128), F32).at[0, SM_ALPHA:SM_ALPHA + 8].set(a_log.reshape(8))
    brow = jnp.zeros((1, 128), F32).at[0, SM_ALPHA:SM_ALPHA + 8].set(dt_bias.reshape(8))
    assert GDN_DK == 2 * CHUNK
    nbat = (tb // (2 * CHUNK)) * GDN_HEADS
    entry = pltpu.VMEM((nbat, 2 * CHUNK, GDN_DK), F32)
    batch_scratch = [entry] * 10 + [pltpu.VMEM((GDN_HEADS, GDN_DK, GDN_DK), F32)]
    o_sds = jax.ShapeDtypeStruct((t, GROUP_W), F32)

    prev, nxt = _halo_specs(tb, w3, COL_GDN_QKV, nb, False, t)
    o_fwd, qkv_act = pl.pallas_call(
        functools.partial(_gdn_kernel, False, False, tb, nb, seq),
        grid=(nb,),
        in_specs=[_main_spec(tb, w3, COL_GDN_QKV, nb, False), prev, nxt,
                  _main_spec(tb, 128, COL_SMALL, nb, False),
                  _const_spec((4, w3)), _const_spec((1, 128)), _const_spec((1, 128))],
        out_specs=(_main_spec(tb, GROUP_W, 0, nb, False), _main_spec(tb, w3, 0, nb, False)),
        out_shape=(o_sds, jax.ShapeDtypeStruct((t, w3), F32)),
        scratch_shapes=[pltpu.VMEM((tb + 2 * HALO, w3), F32)] + batch_scratch,
        compiler_params=_scan_params(),
        name="gdn_fwd",
    )(u, u, u, u, conv_w, arow, brow)
    return pl.pallas_call(
        functools.partial(_gdn_kernel, True, True, tb, nb, seq),
        grid=(nb,),
        in_specs=[_main_spec(tb, w3, 0, nb, True), _main_spec(tb, 128, COL_SMALL, nb, True),
                  _const_spec((1, 128)), _const_spec((1, 128)),
                  _main_spec(tb, GROUP_W, COL_GDN_Z, nb, True), _main_spec(tb, GROUP_W, 0, nb, True),
                  _const_spec((1, GDN_DK)), _const_spec((1, GROUP_W))],
        out_specs=_main_spec(tb, GROUP_W, 0, nb, True),
        out_shape=o_sds,
        scratch_shapes=batch_scratch,
        compiler_params=_scan_params(),
        name="gdn_bwd",
    )(qkv_act, u, arow, brow, u, o_fwd, norm_w.reshape(1, -1), gn_w.reshape(1, -1))


def _lru_kernel(rev, final, tb, nb, seq, *refs):
    if final:
        (x_ref, prev_ref, next_ref, cw_ref, cb_ref, w_ref, ba_ref, bi_ref, lam_ref,
         gate_ref, ofwd_ref, gn_ref, out_ref, xp_ref, h_ref) = refs
    else:
        (x_ref, prev_ref, next_ref, cw_ref, cb_ref, w_ref, ba_ref, bi_ref, lam_ref,
         out_ref, xp_ref, h_ref) = refs
    first, last = _block_pos(nb, tb, rev, seq)
    start = last if rev else first
    bw = LRU_BW

    _fill_pad(xp_ref, x_ref, prev_ref, next_ref, first, last, tb)
    xc = _conv4(xp_ref, cw_ref, tb) + cb_ref[...]

    @pl.when(start)
    def _():
        h_ref[...] = jnp.zeros_like(h_ref)

    rows = lax.broadcasted_iota(jnp.int32, (tb, bw), 0)
    tile_rows = lax.broadcasted_iota(jnp.int32, (tb // SUBLANES, SUBLANES, bw), 1)
    start_row = (tb - 1) if rev else 0
    is_start_row = (rows == start_row) & start
    sp = _softplus(-lam_ref[...])
    for n in range(LRU_BLOCKS):
        xb = xc[:, n * bw:(n + 1) * bw]
        pre = _dot1(xb, w_ref[n])
        rg = _sig(pre[:, :bw] + ba_ref[:, n * bw:(n + 1) * bw])
        ig = _sig(pre[:, bw:] + bi_ref[:, n * bw:(n + 1) * bw])
        log_a = (-LRU_C) * rg * sp[:, n * bw:(n + 1) * bw]
        a = jnp.exp(log_a)
        mult = jnp.sqrt((1.0 + a * a) * jnp.tanh(-log_a))
        mult = jnp.where(is_start_row, 1.0, mult)
        b = mult * (ig * xb)
        ntiles = tb // SUBLANES
        a = a.reshape(ntiles, SUBLANES, bw)
        b = b.reshape(ntiles, SUBLANES, bw)
        s = 1
        while s < SUBLANES:
            sh = (SUBLANES - s) if rev else s
            a_sh = pltpu.roll(a, sh, 1)
            b_sh = pltpu.roll(b, sh, 1)
            ok = (tile_rows < SUBLANES - s) if rev else (tile_rows >= s)
            b = jnp.where(ok, a * b_sh, 0.0) + b
            a = jnp.where(ok, a * a_sh, a)
            s *= 2
        h_prev = h_ref[:, n * bw:(n + 1) * bw]
        for ti in (range(ntiles - 1, -1, -1) if rev else range(ntiles)):
            ht = a[ti] * h_prev + b[ti]
            out_ref[ti * SUBLANES:(ti + 1) * SUBLANES, n * bw:(n + 1) * bw] = ht
            h_prev = ht[0:1, :] if rev else ht[SUBLANES - 1:SUBLANES, :]
        h_ref[:, n * bw:(n + 1) * bw] = h_prev

    if final:
        o = (out_ref[...] + ofwd_ref[...]) * _gelu_tanh(gate_ref[...])
        out_ref[...] = _rms(o, gn_ref[...])


def _lru(u, conv_w, conv_b, wa, ba, wi, bi, lam, gn_w, tb, seq):
    t = u.shape[0]
    nb = t // tb
    w = GROUP_W
    wcat = jnp.concatenate([wa, wi], axis=-1).astype(BF16)
    scratch = [pltpu.VMEM((tb + 2 * HALO, w), F32), pltpu.VMEM((1, w), F32)]

    def call(rev, final, extra_in, extra_specs):
        d = 1 if rev else 0
        prev, nxt = _halo_specs(tb, w, COL_LRU_X, nb, rev, t)
        in_specs = [_main_spec(tb, w, COL_LRU_X, nb, rev), prev, nxt,
                    _const_spec((4, w)), _const_spec((1, w)),
                    _const_spec((LRU_BLOCKS, LRU_BW, 2 * LRU_BW)),
                    _const_spec((1, w)), _const_spec((1, w)), _const_spec((1, w))] + extra_specs
        return pl.pallas_call(
            functools.partial(_lru_kernel, rev, final, tb, nb, seq),
            grid=(nb,),
            in_specs=in_specs,
            out_specs=_main_spec(tb, w, 0, nb, rev),
            out_shape=jax.ShapeDtypeStruct((t, w), F32),
            scratch_shapes=scratch,
            compiler_params=_scan_params(),
            name="lru_bwd" if rev else "lru_fwd",
        )(u, u, u, conv_w, conv_b.reshape(1, w), wcat[d], ba[d].reshape(1, w),
          bi[d].reshape(1, w), lam[d].reshape(1, w), *extra_in)

    o_fwd = call(False, False, [], [])
    return call(True, True, [u, o_fwd, gn_w.reshape(1, w)],
                [_main_spec(tb, w, COL_LRU_GATE, nb, True), _main_spec(tb, w, 0, nb, True),
                 _const_spec((1, w))])


def _ssd_kernel(rev, final, tb, nb, seq, *refs):
    if final:
        (xbc_s, small_ref, arow_ref, brow_ref,
         z_ref, ofwd_ref, dskip_ref, nw_ref, gn_ref, out_ref, st_ref) = refs
    else:
        (xbc_ref, prev_ref, next_ref, small_ref, cw_ref, cb_ref, arow_ref, brow_ref,
         out_ref, xbc_s, xp_ref, st_ref) = refs
    d = 1 if rev else 0
    c = CHUNK
    nc = tb // c
    di = GROUP_W
    p = SSD_HEADDIM
    gw = di // SSD_GROUPS
    first, last = _block_pos(nb, tb, rev, seq)

    assert p == c
    hpg = SSD_HEADS // SSD_GROUPS

    if not final:
        _fill_pad(xp_ref, xbc_ref, prev_ref, next_ref, first, last, tb)
        xbc_s[...] = _silu(_conv4(xp_ref, cw_ref, tb) + cb_ref[...])
    dt = _softplus(small_ref[...] + brow_ref[...])
    da = dt * (-jnp.exp(arow_ref[...]))

    rb, cb_ = _iota2((tb, tb))
    same = (rb >> 6) == (cb_ >> 6)
    incl_b = same & ((cb_ >= rb) if rev else (cb_ <= rb))
    acs = _dot_mask_l(jnp.where(incl_b, 1.0, 0.0).astype(BF16), da)
    tot = _dot_mask_l(jnp.where(same, 1.0, 0.0).astype(BF16), da)
    acs_t = acs.T
    sr, sc = _iota2((128, di))
    sel = jnp.where(sr == SM_DT + d * SSD_HEADS + (sc >> 6), 1.0, 0.0).astype(BF16)
    acs_x = _dot_mask_r(acs, sel)
    tot_x = _dot_mask_r(tot, sel)
    dt_x = _dot_mask_r(dt, sel)
    xdt = xbc_s[:, 0:di] * dt_x
    xdt_e = xdt * jnp.exp(tot_x - acs_x)
    e_acs = jnp.exp(acs_x)
    e_tot = jnp.exp(tot_x)
    xdt_b = _bf(xdt)

    @pl.when(last if rev else first)
    def _():
        st_ref[...] = jnp.zeros_like(st_ref)

    rr, lj = _iota2((c, gw))
    incl_cat = ((lj & (c - 1)) >= rr) if rev else ((lj & (c - 1)) <= rr)
    rbd, cbd = _iota2((gw, gw))
    head_diag = (rbd >> 6) == (cbd >> 6)

    gss = [slice(g * gw, (g + 1) * gw) for g in range(SSD_GROUPS)]
    pairs = [(cj, g) for cj in range(nc) for g in range(SSD_GROUPS)]
    rsl = {cj: slice(cj * c, (cj + 1) * c) for cj in range(nc)}
    bm = {(cj, g): _bf(xbc_s[rsl[cj], di + g * SSD_STATE:di + (g + 1) * SSD_STATE]) for cj, g in pairs}
    cm = {(cj, g): _bf(xbc_s[rsl[cj], di + (SSD_GROUPS + g) * SSD_STATE:di + (SSD_GROUPS + g + 1) * SSD_STATE])
          for cj, g in pairs}
    cbm = {k: _dot1_nt(cm[k], bm[k]) for k in pairs}
    kx = {(cj, g): _dot1_tn(bm[(cj, g)], xdt_e[rsl[cj], gss[g]]) for cj, g in pairs}
    y_d = {}
    for cj, g in pairs:
        rs = rsl[cj]
        cb_cat = jnp.concatenate([cbm[(cj, g)]] * hpg, axis=1)
        col0 = SM_DT + d * SSD_HEADS + g * hpg
        a_row = jnp.concatenate([acs_t[col0 + hh:col0 + hh + 1, rs] for hh in range(hpg)], axis=1)
        l_cat = jnp.where(incl_cat, jnp.exp(acs_x[rs, gss[g]] - a_row), 0.0)
        x_bd = jnp.where(head_diag, jnp.concatenate([xdt_b[rs, gss[g]]] * hpg, axis=0),
                         jnp.zeros((), BF16))
        y_d[(cj, g)] = _dot1(cb_cat * l_cat, x_bd)
    s = [st_ref[g] for g in range(SSD_GROUPS)]
    for ci in range(nc):
        cj = (nc - 1 - ci) if rev else ci
        rs = rsl[cj]
        y_off = [_dot(cm[(cj, g)], _bf(s[g])) for g in range(SSD_GROUPS)]
        for g in range(SSD_GROUPS):
            out_ref[rs, gss[g]] = y_d[(cj, g)] + y_off[g] * e_acs[rs, gss[g]]
            s[g] = s[g] * e_tot[cj * c:cj * c + 1, gss[g]] + kx[(cj, g)]
    for g in range(SSD_GROUPS):
        st_ref[g] = s[g]

    if final:
        y = out_ref[...] + ofwd_ref[...] + dskip_ref[...] * xbc_s[:, 0:di]
        y = y * _silu(z_ref[...])
        for g in range(SSD_GROUPS):
            out_ref[:, g * gw:(g + 1) * gw] = _rms(y[:, g * gw:(g + 1) * gw], nw_ref[:, g * gw:(g + 1) * gw])
        out_ref[...] = _rms(out_ref[...], gn_ref[...])


def _ssd(u, conv_w, conv_b, a_log, dt_bias, d_skip, norm_w, gn_w, tb, seq):
    t = u.shape[0]
    nb = t // tb
    wx = GROUP_W + 2 * SSD_GROUPS * SSD_STATE
    arow = jnp.zeros((1, 128), F32).at[0, SM_DT:SM_DT + 16].set(a_log.reshape(16))
    brow = jnp.zeros((1, 128), F32).at[0, SM_DT:SM_DT + 16].set(dt_bias.reshape(16))
    dsk = jnp.repeat(d_skip, SSD_HEADDIM).reshape(1, GROUP_W)
    state = pltpu.VMEM((SSD_GROUPS, SSD_STATE, GROUP_W // SSD_GROUPS), F32)
    w = GROUP_W
    o_sds = jax.ShapeDtypeStruct((t, w), F32)

    prev, nxt = _halo_specs(tb, wx, COL_SSD_XBC, nb, False, t)
    o_fwd, xbc_act = pl.pallas_call(
        functools.partial(_ssd_kernel, False, False, tb, nb, seq),
        grid=(nb,),
        in_specs=[_main_spec(tb, wx, COL_SSD_XBC, nb, False), prev, nxt,
                  _main_spec(tb, 128, COL_SMALL, nb, False),
                  _const_spec((4, wx)), _const_spec((1, wx)),
                  _const_spec((1, 128)), _const_spec((1, 128))],
        out_specs=(_main_spec(tb, w, 0, nb, False), _main_spec(tb, wx, 0, nb, False)),
        out_shape=(o_sds, jax.ShapeDtypeStruct((t, wx), F32)),
        scratch_shapes=[pltpu.VMEM((tb + 2 * HALO, wx), F32), state],
        compiler_params=_scan_params(),
        name="ssd_fwd",
    )(u, u, u, u, conv_w, conv_b.reshape(1, wx), arow, brow)
    return pl.pallas_call(
        functools.partial(_ssd_kernel, True, True, tb, nb, seq),
        grid=(nb,),
        in_specs=[_main_spec(tb, wx, 0, nb, True), _main_spec(tb, 128, COL_SMALL, nb, True),
                  _const_spec((1, 128)), _const_spec((1, 128)),
                  _main_spec(tb, w, COL_SSD_Z, nb, True), _main_spec(tb, w, 0, nb, True),
                  _const_spec((1, w)), _const_spec((1, w)), _const_spec((1, w))],
        out_specs=_main_spec(tb, w, 0, nb, True),
        out_shape=o_sds,
        scratch_shapes=[state],
        compiler_params=_scan_params(),
        name="ssd_bwd",
    )(xbc_act, u, arow, brow, u, o_fwd, dsk, norm_w.reshape(1, w), gn_w.reshape(1, w))


def _hgrn_kernel(rev, final, layer, tb, nb, seq, *refs):
    if final:
        (q_ref, f_ref, i_ref, lb_ref, mask_ref, g_ref, ofwd_ref, nw_ref, gn_ref, out_ref,
         d_s, z_s, qd_s, kd_s, st_ref) = refs
    else:
        (q_ref, f_ref, i_ref, lb_ref, mask_ref, out_ref,
         d_s, z_s, qd_s, kd_s, st_ref) = refs
    c = HGRN_STATE_CHUNK
    nl = HGRN_LEVELS
    nc = tb // c
    w = GROUP_W
    hw = HGRN_DK
    first, last = _block_pos(nb, tb, rev, seq)

    lbp = lb_ref[...]
    mx = jnp.max(lbp, axis=0, keepdims=True)
    ex = jnp.exp(lbp - mx)
    den = jnp.sum(ex, axis=0, keepdims=True)
    lb = jnp.zeros((1, w), F32)
    for m in range(1, layer + 1):
        lb = lb + ex[m:m + 1, :] / den

    f = lb + (1.0 - lb) * _sig(f_ref[...])
    logf = jnp.log(f)
    kk = 1.0 - f
    qq = _silu(q_ref[...])
    vv = i_ref[...]

    l1 = _bf(logf)
    l2 = _bf(logf - l1.astype(F32))

    d_s[...] = _dot(mask_ref[...], l1) + _dot(mask_ref[...], l2)

    def range_sum(blk):
        return d_s[blk * tb:(blk + 1) * tb, :]

    rows = lax.broadcasted_iota(jnp.int32, (tb, 1), 0)
    z_s[0] = _bf(qq)
    z_s[1] = _bf(kk)
    for lv in range(1, nl + 1):
        bit = (rows >> (lv - 1)) & 1
        is_query = (bit == 0) if rev else (bit == 1)
        z_s[lv + 1] = _bf(jnp.exp(range_sum(lv - 1)) * jnp.where(is_query, qq, kk))
    bsc = range_sum(nl)
    btot = range_sum(nl + 1)
    qd_s[...] = _bf(qq * jnp.exp(bsc))
    kd_s[...] = _bf(kk * jnp.exp(btot - bsc))
    gl = jnp.exp(btot)
    vb = _bf(vv)

    r64, c64 = _iota2((c, c))
    pair_masks = [r64 == c64]
    for lv in range(1, nl + 1):
        rbit = (r64 >> (lv - 1)) & 1
        cbit = (c64 >> (lv - 1)) & 1
        split = ((rbit == 0) & (cbit == 1)) if rev else ((rbit == 1) & (cbit == 0))
        pair_masks.append(((r64 >> lv) == (c64 >> lv)) & split)

    @pl.when(last if rev else first)
    def _():
        st_ref[...] = jnp.zeros_like(st_ref)

    hs = range(HGRN_HEADS)
    sls = [slice(h * hw, (h + 1) * hw) for h in hs]
    s = [st_ref[h] for h in hs]
    for ci in range(nc):
        cj = (nc - 1 - ci) if rev else ci
        rs = slice(cj * c, (cj + 1) * c)
        prods = [[_dot1_nt(z_s[0, rs, sls[h]], z_s[1, rs, sls[h]])]
                 + [_dot1_nt(z_s[lv + 1, rs, sls[h]], z_s[lv + 1, rs, sls[h]]) for lv in range(1, nl + 1)]
                 for h in hs]
        o_st = [_dot1_nt(qd_s[rs, sls[h]], s[h]) for h in hs]
        kv = [_dot1_tn(vb[rs, sls[h]], kd_s[rs, sls[h]]) for h in hs]
        sc = []
        for h in hs:
            acc = jnp.where(pair_masks[0], prods[h][0], 0.0)
            for lv in range(1, nl + 1):
                acc = acc + jnp.where(pair_masks[lv], prods[h][lv], 0.0)
            sc.append(_bf(acc))
        y_in = [_dot(sc[h], vb[rs, sls[h]]) for h in hs]
        for h in hs:
            out_ref[rs, sls[h]] = y_in[h] + o_st[h]
            s[h] = s[h] * gl[cj * c:cj * c + 1, sls[h]] + kv[h]
    for h in hs:
        st_ref[h] = s[h]

    if final:
        o = out_ref[...] + ofwd_ref[...]
        gg = g_ref[...]
        for h in range(HGRN_HEADS):
            sl = slice(h * hw, (h + 1) * hw)
            out_ref[:, sl] = _rms(o[:, sl], nw_ref[...]) * _silu(gg[:, sl])
        out_ref[...] = _rms(out_ref[...], gn_ref[...])


def _hgrn_masks(tb, rev):
    t = np.arange(tb)[:, None]
    r = np.arange(tb)[None, :]
    blocks = []
    for lv in range(1, HGRN_LEVELS + 1):
        g = 1 << lv
        mid = (t // g) * g + g // 2
        if rev:
            m = np.where(t < mid, (r >= t) & (r < mid), (r >= mid) & (r < t))
        else:
            m = np.where(t >= mid, (r >= mid) & (r <= t), (r > t) & (r < mid))
        blocks.append(m)
    same = (r // HGRN_STATE_CHUNK) == (t // HGRN_STATE_CHUNK)
    blocks.append(same & ((r >= t) if rev else (r <= t)))
    blocks.append(same)
    return jnp.asarray(np.concatenate(blocks, axis=0).astype(np.float32), dtype=BF16)


def _hgrn(u, lb_param, layer, norm_w, gn_w, tb, seq):
    t = u.shape[0]
    nb = t // tb
    w = GROUP_W
    nblk = HGRN_LEVELS + 2
    scratch = [pltpu.VMEM((nblk * tb, w), F32),
               pltpu.VMEM((nblk, tb, w), BF16), pltpu.VMEM((tb, w), BF16),
               pltpu.VMEM((tb, w), BF16),
               pltpu.VMEM((HGRN_HEADS, HGRN_DK, HGRN_DK), F32)]

    def call(rev, final, extra_in, extra_specs):
        d = 1 if rev else 0
        in_specs = [_main_spec(tb, w, COL_HG_Q, nb, rev),
                    _main_spec(tb, w, COL_HG_F + d * w, nb, rev),
                    _main_spec(tb, w, COL_HG_I, nb, rev),
                    _const_spec((DEPTH, w)), _const_spec((nblk * tb, tb))] + extra_specs
        return pl.pallas_call(
            functools.partial(_hgrn_kernel, rev, final, layer, tb, nb, seq),
            grid=(nb,),
            in_specs=in_specs,
            out_specs=_main_spec(tb, w, 0, nb, rev),
            out_shape=jax.ShapeDtypeStruct((t, w), F32),
            scratch_shapes=scratch,
            compiler_params=_scan_params(),
            name="hgrn_bwd" if rev else "hgrn_fwd",
        )(u, u, u, lb_param[d], _hgrn_masks(tb, rev), *extra_in)

    o_fwd = call(False, False, [], [])
    return call(True, True, [u, o_fwd, norm_w.reshape(1, -1), gn_w.reshape(1, w)],
                [_main_spec(tb, w, COL_HG_G, nb, True), _main_spec(tb, w, 0, nb, True),
                 _const_spec((1, HGRN_DK)), _const_spec((1, w))])


def _outproj_kernel(starts, *refs):
    nsrc = len(starts)
    x_refs = refs[:nsrc]
    ya_ref, yb_ref, yc_ref, yd_ref, w_ref, o_ref = refs[nsrc:]
    acc = None
    for n, y_ref in enumerate((ya_ref, yb_ref, yc_ref, yd_ref)):
        part = _dot(_bf(y_ref[...]), w_ref[n * GROUP_W:(n + 1) * GROUP_W, :])
        acc = part if acc is None else acc + part

    def add_residual(x_ref):
        o_ref[...] = x_ref[...] + acc

    _select_rows(pl.program_id(0), x_refs, starts, add_residual)


def _outproj(xs, ys, w):
    d = xs[0].shape[1]
    t = sum(x.shape[0] for x in xs)
    tm = 512
    x_specs, starts = _row_source_specs(xs, tm)
    yspec = pl.BlockSpec((tm, GROUP_W), lambda i: (i, 0))
    return pl.pallas_call(
        functools.partial(_outproj_kernel, starts),
        grid=(t // tm,),
        in_specs=x_specs + [yspec, yspec, yspec, yspec, pl.BlockSpec(w.shape, lambda i: (0, 0))],
        out_specs=pl.BlockSpec((tm, d), lambda i: (i, 0)),
        out_shape=jax.ShapeDtypeStruct((t, d), F32),
        compiler_params=pltpu.CompilerParams(dimension_semantics=("arbitrary",),
                                             vmem_limit_bytes=VMEM_LIMIT),
        name="outproj",
    )(*xs, *ys, w)


def _ffn_kernel(apply_final, tm, nrb, nf, seq, out_starts, *refs):
    (x_ref, prev_ref, next_ref, ln_ref, wg_ref, wv_ref, cwg_ref, cwv_ref, cbg_ref, cbv_ref,
     wd_ref, fn_ref) = refs[:12]
    o_refs = refs[12:12 + len(out_starts)]
    h_s, g_s, v_s, acc_s = refs[12 + len(out_starts):]
    j = pl.program_id(1)
    i = pl.program_id(0)
    row0 = i * tm
    first = _seq_start(row0, seq)
    last = _seq_start(row0 + tm, seq)

    @pl.when(j == 0)
    def _():
        keep_p = jnp.where(first, 0.0, 1.0)
        keep_n = jnp.where(last, 0.0, 1.0)
        h_s[0:FHALO, :] = _bf(_rms(prev_ref[...], ln_ref[...]) * keep_p)
        h_s[FHALO:FHALO + tm, :] = _bf(_rms(x_ref[...], ln_ref[...]))
        h_s[FHALO + tm:2 * FHALO + tm, :] = _bf(_rms(next_ref[...], ln_ref[...]) * keep_n)
        acc_s[...] = jnp.zeros_like(acc_s)

    hb = h_s[...]
    g_s[...] = _dot(hb, wg_ref[...])
    v_s[...] = _dot(hb, wv_ref[...])

    def conv3(s_ref, cw_ref, cb_ref):
        acc = s_ref[FHALO - 1:FHALO - 1 + tm, :] * cw_ref[0:1, :]
        acc = acc + s_ref[FHALO:FHALO + tm, :] * cw_ref[1:2, :]
        acc = acc + s_ref[FHALO + 1:FHALO + 1 + tm, :] * cw_ref[2:3, :]
        return acc + cb_ref[...]

    act = _silu(conv3(g_s, cwg_ref, cbg_ref)) * conv3(v_s, cwv_ref, cbv_ref)
    acc_s[...] += _dot(_bf(act), wd_ref[...])

    @pl.when(j == nf - 1)
    def _():
        y = x_ref[...] + acc_s[...]
        if apply_final:
            y = _rms(y, fn_ref[...])

        def write(o_ref):
            o_ref[...] = y

        _select_rows(i, o_refs, out_starts, write)


def _ffn(x, ln, w_up, conv_w, conv_b, w_down, final_w, apply_final, seq, out_rows):
    t, d = x.shape
    tm = 512
    tf = 512
    nf = D_FF // tf
    nrb = t // tm
    r = tm // FHALO
    cb = conv_b.reshape(1, 2 * D_FF)
    up_buf = pltpu.VMEM((tm + 2 * FHALO, tf), F32)
    assert sum(out_rows) == t
    out_specs, out_starts = _row_source_specs([jax.ShapeDtypeStruct((n, d), F32) for n in out_rows], tm)
    return pl.pallas_call(
        functools.partial(_ffn_kernel, apply_final, tm, nrb, nf, seq, out_starts),
        grid=(nrb, nf),
        in_specs=[pl.BlockSpec((tm, d), lambda i, j: (i, 0)),
                  pl.BlockSpec((FHALO, d), lambda i, j: (jnp.maximum(i * r - 1, 0), 0)),
                  pl.BlockSpec((FHALO, d), lambda i, j: (jnp.minimum((i + 1) * r, t // FHALO - 1), 0)),
                  pl.BlockSpec((1, d), lambda i, j: (0, 0)),
                  pl.BlockSpec((d, tf), lambda i, j: (0, j)),
                  pl.BlockSpec((d, tf), lambda i, j: (0, j + nf)),
                  pl.BlockSpec((3, tf), lambda i, j: (0, j)),
                  pl.BlockSpec((3, tf), lambda i, j: (0, j + nf)),
                  pl.BlockSpec((1, tf), lambda i, j: (0, j)),
                  pl.BlockSpec((1, tf), lambda i, j: (0, j + nf)),
                  pl.BlockSpec((tf, d), lambda i, j: (j, 0)),
                  pl.BlockSpec((1, d), lambda i, j: (0, 0))],
        out_specs=tuple(out_specs),
        out_shape=tuple(jax.ShapeDtypeStruct((n, d), F32) for n in out_rows),
        scratch_shapes=[pltpu.VMEM((tm + 2 * FHALO, d), BF16), up_buf, up_buf,
                        pltpu.VMEM((tm, d), F32)],
        compiler_params=pltpu.CompilerParams(dimension_semantics=("arbitrary", "arbitrary"),
                                             vmem_limit_bytes=VMEM_LIMIT),
        name="ffn",
    )(x, x, x, ln.reshape(1, d), w_up, w_up, conv_w, conv_w, cb, cb, w_down, final_w.reshape(1, d))


def _permute_w_in(w):
    sizes = (1536, 512, 8, 8, 512, 512, 512, 1024, 16, 512, 1024, 512, 512)
    offs = [0]
    for s in sizes:
        offs.append(offs[-1] + s)
    (qkv, z, beta, alpha, lx, lg, sz, sxbc, sdt, hq, hf, hi, hg) = [
        w[:, offs[n]:offs[n + 1]] for n in range(len(sizes))]
    padw = jnp.zeros((w.shape[0], D_IN_P - COL_SMALL - 32), w.dtype)
    return jnp.concatenate([qkv, z, sxbc, lx, lg, sz, hq, hf, hi, hg, beta, alpha, sdt, padw], axis=1)


def kernel(x_prompt, x_sample, ln1, w_in, gdn_conv_w, gdn_a_log, gdn_dt_bias, gdn_norm_w, lru_conv_w, lru_conv_b, lru_wa, lru_ba, lru_wi, lru_bi, lru_lambda, ssd_conv_w, ssd_conv_b, ssd_a_log, ssd_dt_bias, ssd_d, ssd_norm_w, hgrn_lb, hgrn_norm_w, group_norm_w, w_out, ln2, w_up, ffn_conv_w, ffn_conv_b, w_down, final_norm):
    b, s, dm = x_prompt.shape
    db, ds, _ = x_sample.shape
    seq = (b * s, s, ds)
    assert s % TB_SCAN == 0 and ds % TB_SCAN == 0 and s % TB_HGRN == 0 and ds % TB_HGRN == 0
    xs = [x_prompt.reshape(b * s, dm), x_sample.reshape(db * ds, dm)]
    depth = w_in.shape[0]
    for l in range(depth):
        last_layer = l == depth - 1
        u = _inproj(xs, ln1[l], _permute_w_in(w_in[l].astype(BF16)))
        gn = group_norm_w[l]
        ya = _gdn(u, gdn_conv_w[l], gdn_a_log[l], gdn_dt_bias[l], gdn_norm_w[l], gn[0], TB_SCAN, seq)
        yb = _lru(u, lru_conv_w[l], lru_conv_b[l], lru_wa[l], lru_ba[l], lru_wi[l], lru_bi[l],
                  lru_lambda[l], gn[1], TB_SCAN, seq)
        yc = _ssd(u, ssd_conv_w[l], ssd_conv_b[l], ssd_a_log[l], ssd_dt_bias[l], ssd_d[l],
                  ssd_norm_w[l], gn[2], TB_SCAN, seq)
        yd = _hgrn(u, hgrn_lb, l, hgrn_norm_w[l], gn[3], TB_HGRN, seq)
        x = _outproj(xs, (ya, yb, yc, yd), w_out[l].astype(BF16))
        out_rows = [b * s, db * ds] if last_layer else [b * s + db * ds]
        xs = list(_ffn(x, ln2[l], w_up[l].astype(BF16), ffn_conv_w[l], ffn_conv_b[l],
                       w_down[l].astype(BF16), final_norm, last_layer, seq, out_rows))
    return (xs[0].reshape(b, s, dm), xs[1].reshape(db, ds, dm))
```

```python
import functools
import math

import jax
import jax.numpy as jnp
import numpy as np
from jax import lax
from jax.experimental import pallas as pl
from jax.experimental.pallas import tpu as pltpu

F32 = jnp.float32
BF16 = jnp.bfloat16

D_MODEL = 2048
DEPTH = 2
GROUP_W = 512
GDN_HEADS = 4
GDN_DK = 128
LRU_BLOCKS = 4
LRU_BW = 128
LRU_C = 8.0
SSD_HEADS = 8
SSD_HEADDIM = 64
SSD_GROUPS = 2
SSD_STATE = 128
HGRN_HEADS = 4
HGRN_DK = 128
D_FF = 5632
CHUNK = 64
HGRN_LEVELS = 6
HGRN_STATE_CHUNK = 1 << HGRN_LEVELS
EPS = 1e-6

COL_GDN_QKV = 0
COL_GDN_Z = 1536
COL_GDN_SMALL = 2048
COL_LRU_X = 2064
COL_LRU_GATE = 2576
COL_SSD_Z = 3088
COL_SSD_XBC = 3600
COL_SSD_SMALL = 4608
HG_SHIFT = 96
COL_HG_Q = 4640 + HG_SHIFT
COL_HG_F = 5152 + HG_SHIFT
COL_HG_I = 6176 + HG_SHIFT
COL_HG_G = 6688 + HG_SHIFT
D_IN = 7200
D_IN_P = D_IN + HG_SHIFT
SM_BETA = 0
SM_ALPHA = 8
SM_DT = 16

SUBLANES = 8
HALO = 8
FHALO = 16
VMEM_LIMIT = 56 * 1024 * 1024
TB_SCAN = 512
TB_HGRN = 256

def _sig(x):
    return 1.0 / (1.0 + jnp.exp(-x))


def _silu(x):
    return x * _sig(x)


def _softplus(x):
    return jnp.maximum(x, 0.0) + jnp.log1p(jnp.exp(-jnp.abs(x)))


def _gelu_tanh(x):
    c = math.sqrt(2.0 / math.pi)
    return 0.5 * x * (1.0 + jnp.tanh(c * (x + 0.044715 * (x * x * x))))


def _bf(x):
    return x.astype(BF16)


def _dot(a, b):
    return jnp.dot(a, b, preferred_element_type=F32)


def _dot1(a, b):
    return _dot(_bf(a), _bf(b))


def _dot1_nt(a, b):
    return lax.dot_general(_bf(a), _bf(b), (((1,), (1,)), ((), ())),
                           preferred_element_type=F32)


def _dot1_tn(a, b):
    return lax.dot_general(_bf(a), _bf(b), (((0,), (0,)), ((), ())),
                           preferred_element_type=F32)


def _dot_mask_l(m, b):
    b1 = _bf(b)
    r1 = b - b1.astype(F32)
    b2 = _bf(r1)
    b3 = _bf(r1 - b2.astype(F32))
    return _dot(m, b1) + (_dot(m, b2) + _dot(m, b3))


def _dot_mask_r(a, m):
    a1 = _bf(a)
    r1 = a - a1.astype(F32)
    a2 = _bf(r1)
    a3 = _bf(r1 - a2.astype(F32))
    return _dot(a1, m) + (_dot(a2, m) + _dot(a3, m))


def _rms(x, w):
    return x * lax.rsqrt(jnp.mean(x * x, axis=-1, keepdims=True) + EPS) * w


def _seq_start(row, seq):
    p_rows, s_len, ds_len = seq
    return jnp.where(row < p_rows, lax.rem(row, s_len) == 0,
                     lax.rem(row - p_rows, ds_len) == 0)


def _block_pos(nb, tb, rev, seq):
    i = pl.program_id(0)
    j = (nb - 1 - i) if rev else i
    row0 = j * tb
    first = _seq_start(row0, seq)
    last = _seq_start(row0 + tb, seq)
    return first, last


def _fill_pad(xp_ref, blk_ref, prev_ref, next_ref, first, last, tb):
    keep_p = jnp.where(first, 0.0, 1.0)
    keep_n = jnp.where(last, 0.0, 1.0)
    xp_ref[0:HALO, :] = prev_ref[...] * keep_p
    xp_ref[HALO:HALO + tb, :] = blk_ref[...]
    xp_ref[HALO + tb:2 * HALO + tb, :] = next_ref[...] * keep_n


def _conv4(xp_ref, w_ref, tb):
    acc = xp_ref[HALO - 2:HALO - 2 + tb, :] * w_ref[0:1, :]
    acc = acc + xp_ref[HALO - 1:HALO - 1 + tb, :] * w_ref[1:2, :]
    acc = acc + xp_ref[HALO:HALO + tb, :] * w_ref[2:3, :]
    acc = acc + xp_ref[HALO + 1:HALO + 1 + tb, :] * w_ref[3:4, :]
    return acc


def _iota2(shape):
    return (lax.broadcasted_iota(jnp.int32, shape, 0),
            lax.broadcasted_iota(jnp.int32, shape, 1))


def _bdot(a, b):
    return lax.dot_general(a, b, (((2,), (1,)), ((0,), (0,))), preferred_element_type=F32)


def _bdot1_nt(a, b):
    return lax.dot_general(_bf(a), _bf(b), (((2,), (2,)), ((0,), (0,))),
                           preferred_element_type=F32)


def _bdot1(a, b):
    return _bdot(_bf(a), _bf(b))


def _unit_tri_inverse(a, r, c, n):
    eye = jnp.where(r == c, 1.0, 0.0)[None]
    d = jnp.where(((r >> 3) == (c >> 3))[None], a, 0.0)
    d2 = _bdot1(d, d)
    x = eye - d
    x = x + _bdot1(x, d2)
    d4 = _bdot1(d2, d2)
    x = x + _bdot1(x, d4)
    b = 16
    sh = 4
    while b <= n:
        joined = ((r >> sh) == (c >> sh)) & ((r >> (sh - 1)) != (c >> (sh - 1)))
        e = jnp.where(joined[None], a, 0.0)
        x = x - _bdot1(_bdot1(x, e), x)
        b *= 2
        sh += 1
    return x


def _row_source_specs(xs, tm):
    d = xs[0].shape[1]
    starts = []
    specs = []
    off = 0
    for x in xs:
        assert x.shape[0] % tm == 0
        nblk = x.shape[0] // tm

        def imap(i, *_, off=off, nblk=nblk):
            return (jnp.clip(i - off, 0, nblk - 1), 0)

        specs.append(pl.BlockSpec((tm, d), imap))
        starts.append(off)
        off += nblk
    return specs, starts


def _select_rows(i, refs, starts, fn):
    for n, ref in enumerate(refs):
        lo = starts[n]
        hi = starts[n + 1] if n + 1 < len(starts) else None
        cond = i >= lo if hi is None else ((i >= lo) & (i < hi))
        if len(refs) == 1:
            fn(ref)
        else:
            pl.when(cond)(functools.partial(fn, ref))


def _inproj_kernel(starts, *refs):
    nsrc = len(starts)
    x_refs = refs[:nsrc]
    ln_ref, w_ref, o_ref, h_s = refs[nsrc:]

    def norm_rows(x_ref):
        h_s[...] = _bf(_rms(x_ref[...], ln_ref[...]))

    @pl.when(pl.program_id(1) == 0)
    def _():
        _select_rows(pl.program_id(0), x_refs, starts, norm_rows)

    o_ref[...] = _dot(h_s[...], w_ref[...])


def _inproj(xs, ln, w):
    d = xs[0].shape[1]
    t = sum(x.shape[0] for x in xs)
    n = w.shape[1]
    tm = 512
    tn = n // 3
    x_specs, starts = _row_source_specs(xs, tm)
    return pl.pallas_call(
        functools.partial(_inproj_kernel, starts),
        grid=(t // tm, n // tn),
        in_specs=x_specs + [pl.BlockSpec((1, d), lambda i, j: (0, 0)),
                            pl.BlockSpec((d, tn), lambda i, j: (0, j))],
        out_specs=pl.BlockSpec((tm, tn), lambda i, j: (i, j)),
        out_shape=jax.ShapeDtypeStruct((t, n), F32),
        scratch_shapes=[pltpu.VMEM((tm, d), BF16)],
        compiler_params=pltpu.CompilerParams(
            dimension_semantics=("arbitrary", "arbitrary"),
            vmem_limit_bytes=VMEM_LIMIT),
        name="inproj",
    )(*xs, ln.reshape(1, d), w)


def _bidx(nb, rev):
    if rev:
        return lambda i: nb - 1 - i
    return lambda i: i


LANES = 128


def _window_spec(rows, w, col, row_block):
    if col % w == 0:
        return pl.BlockSpec((rows, w), lambda i: (row_block(i), col // w)), 0
    if col % LANES == 0:
        return pl.BlockSpec((pl.Element(rows), pl.Element(w)), lambda i: (row_block(i) * rows, col)), 0
    wide = w + LANES
    start = min(col - col % LANES, D_IN_P - wide)
    spec = pl.BlockSpec((pl.Element(rows), pl.Element(wide)), lambda i: (row_block(i) * rows, start))
    return spec, col - start


class _LaneWindow:
    def __init__(self, ref, lane0, w):
        self.ref, self.lane0, self.w = ref, lane0, w

    def __getitem__(self, idx):
        assert idx is Ellipsis
        return self.ref[:, self.lane0:self.lane0 + self.w]


def _main_spec(tb, w, col, nb, rev):
    spec, lane0 = _window_spec(tb, w, col, _bidx(nb, rev))
    assert lane0 == 0
    return spec


def _main_window(tb, w, col, nb, rev):
    return _window_spec(tb, w, col, _bidx(nb, rev))


def _halo_specs(tb, w, col, nb, rev, t):
    f = _bidx(nb, rev)
    r = tb // HALO
    prev, _ = _window_spec(HALO, w, col, lambda i: jnp.maximum(f(i) * r - 1, 0))
    nxt, _ = _window_spec(HALO, w, col, lambda i: jnp.minimum((f(i) + 1) * r, t // HALO - 1))
    return prev, nxt


def _const_spec(shape):
    nd = len(shape)
    return pl.BlockSpec(shape, lambda i: (0,) * nd)


def _scan_params():
    return pltpu.CompilerParams(dimension_semantics=("arbitrary",),
                                vmem_limit_bytes=VMEM_LIMIT)


def _gdn_kernel(rev, final, tb, nb, seq, *refs):
    if final:
        (qkv_in_ref, small_ref, arow_ref, brow_ref,
         z_ref, ofwd_ref, nw_ref, gn_ref, out_ref,
         q3, k3, v3, bet3, gi3, gt3, gj3, w3, u3, qk3, st_ref) = refs
    else:
        (qkv_ref, prev_ref, next_ref, small_ref, cw_ref, arow_ref, brow_ref, out_ref, qkv_out_ref,
         xp_ref, q3, k3, v3, bet3, gi3, gt3, gj3, w3, u3, qk3, st_ref) = refs
    d = 1 if rev else 0
    c = CHUNK
    c2 = 2 * CHUNK
    npair = tb // c2
    nh = GDN_HEADS
    hw = GDN_DK
    kw = GDN_HEADS * GDN_DK
    first, last = _block_pos(nb, tb, rev, seq)

    if not final:
        _fill_pad(xp_ref, qkv_ref, prev_ref, next_ref, first, last, tb)
        y = _silu(_conv4(xp_ref, cw_ref, tb))
    for h in range(nh):
        if final:
            qh = qkv_in_ref[:, h * hw:(h + 1) * hw]
            kh = qkv_in_ref[:, kw + h * hw:kw + (h + 1) * hw]
            vh = qkv_in_ref[:, 2 * kw + h * hw:2 * kw + (h + 1) * hw]
        else:
            qh = y[:, h * hw:(h + 1) * hw]
            qh = qh * lax.rsqrt(jnp.sum(qh * qh, axis=-1, keepdims=True) + EPS) * (GDN_DK ** -0.5)
            kh = y[:, kw + h * hw:kw + (h + 1) * hw]
            kh = kh * lax.rsqrt(jnp.sum(kh * kh, axis=-1, keepdims=True) + EPS)
            vh = y[:, 2 * kw + h * hw:2 * kw + (h + 1) * hw]
            qkv_out_ref[:, h * hw:(h + 1) * hw] = qh
            qkv_out_ref[:, kw + h * hw:kw + (h + 1) * hw] = kh
            qkv_out_ref[:, 2 * kw + h * hw:2 * kw + (h + 1) * hw] = vh
        for pi in range(npair):
            q3[pi * nh + h] = qh[pi * c2:(pi + 1) * c2, :]
            k3[pi * nh + h] = kh[pi * c2:(pi + 1) * c2, :]
            v3[pi * nh + h] = vh[pi * c2:(pi + 1) * c2, :]

    sm = small_ref[...]
    beta_all = _sig(sm)
    g_all = -jnp.exp(arow_ref[...]) * _softplus(sm + brow_ref[...])
    rb, cb_ = _iota2((tb, tb))
    same = (rb >> 6) == (cb_ >> 6)
    incl_b = same & ((cb_ >= rb) if rev else (cb_ <= rb))
    gcum = _dot_mask_l(jnp.where(incl_b, 1.0, 0.0).astype(BF16), g_all)
    gtot = _dot_mask_l(jnp.where(same, 1.0, 0.0).astype(BF16), g_all)
    gcum_t = gcum.T
    for pi in range(npair):
        rows = slice(pi * c2, (pi + 1) * c2)
        for h in range(nh):
            cb = SM_BETA + d * nh + h
            cg = SM_ALPHA + d * nh + h
            b = pi * nh + h
            bet3[b] = jnp.broadcast_to(beta_all[rows, cb:cb + 1], (c2, hw))
            gi3[b] = jnp.broadcast_to(gcum[rows, cg:cg + 1], (c2, hw))
            gt3[b] = jnp.broadcast_to(gtot[rows, cg:cg + 1], (c2, hw))
            gj3[b] = jnp.broadcast_to(gcum_t[cg:cg + 1, rows], (c2, c2))

    @pl.when(last if rev else first)
    def _():
        st_ref[...] = jnp.zeros_like(st_ref)

    r, cc = _iota2((c2, c2))
    same_chunk = (r >> 6) == (cc >> 6)
    incl = same_chunk & ((cc >= r) if rev else (cc <= r))
    strict = same_chunk & ((cc > r) if rev else (cc < r))
    q = q3[...]
    k = k3[...]
    bet = bet3[...]
    gi = gi3[...]
    decay = jnp.where(incl[None], jnp.exp(gi - gj3[...]), 0.0)
    kb = k * bet
    a = jnp.where(strict[None], _bdot1_nt(kb, k) * decay, 0.0)
    tm = _unit_tri_inverse(a, r, cc, c)
    eg = jnp.exp(gi)
    w3[...] = _bdot1(tm, kb * eg)
    u3[...] = _bdot1(tm, v3[...] * bet)
    qk3[...] = _bdot1_nt(q, k) * decay
    q3[...] = q * eg
    k3[...] = k * jnp.exp(gt3[...] - gi)

    def chunk_pair(pi, carry):
        pj = (npair - 1 - pi) if rev else pi
        hs = range(nh)
        for half in ((1, 0) if rev else (0, 1)):
            r0 = pl.multiple_of(pj * c2 + half * c, c)
            hr = slice(half * c, (half + 1) * c)
            s = [st_ref[h] for h in hs]
            sb = [_bf(s[h]) for h in hs]
            ws = [_dot(_bf(w3[pj * nh + h, hr, :]), sb[h]) for h in hs]
            qs = [_dot(_bf(q3[pj * nh + h, hr, :]), sb[h]) for h in hs]
            v_new = [_bf(u3[pj * nh + h, hr, :] - ws[h]) for h in hs]
            kv = [_dot1_tn(k3[pj * nh + h, hr, :], v_new[h]) for h in hs]
            ov = [_dot(_bf(qk3[pj * nh + h, hr, hr]), v_new[h]) for h in hs]
            for h in hs:
                st_ref[h] = s[h] * jnp.exp(gt3[pj * nh + h, half * c:half * c + 1, :]) + kv[h]
                out_ref[pl.ds(r0, c), h * hw:(h + 1) * hw] = qs[h] + ov[h]
        return carry

    lax.fori_loop(0, npair, chunk_pair, 0)

    if final:
        o = out_ref[...] + ofwd_ref[...]
        z = z_ref[...]
        for h in range(GDN_HEADS):
            oh = _rms(o[:, h * hw:(h + 1) * hw], nw_ref[...])
            out_ref[:, h * hw:(h + 1) * hw] = oh * _silu(z[:, h * hw:(h + 1) * hw])
        out_ref[...] = _rms(out_ref[...], gn_ref[...])


def _gdn(u, conv_w, a_log, dt_bias, norm_w, gn_w, tb, seq):
    t = u.shape[0]
    nb = t // tb
    w3 = 3 * GROUP_W
    arow = jnp.zeros((1, 128), F32).at[0, SM_ALPHA:SM_ALPHA + 8].set(a_log.reshape(8))
    brow = jnp.zeros((1, 128), F32).at[0, SM_ALPHA:SM_ALPHA + 8].set(dt_bias.reshape(8))
    assert GDN_DK == 2 * CHUNK
    nbat = (tb // (2 * CHUNK)) * GDN_HEADS
    entry = pltpu.VMEM((nbat, 2 * CHUNK, GDN_DK), F32)
    batch_scratch = [entry] * 10 + [pltpu.VMEM((GDN_HEADS, GDN_DK, GDN_DK), F32)]
    o_sds = jax.ShapeDtypeStruct((t, GROUP_W), F32)

    prev, nxt = _halo_specs(tb, w3, COL_GDN_QKV, nb, False, t)
    o_fwd, qkv_act = pl.pallas_call(
        functools.partial(_gdn_kernel, False, False, tb, nb, seq),
        grid=(nb,),
        in_specs=[_main_spec(tb, w3, COL_GDN_QKV, nb, False), prev, nxt,
                  _main_spec(tb, 128, COL_GDN_SMALL, nb, False),
                  _const_spec((4, w3)), _const_spec((1, 128)), _const_spec((1, 128))],
        out_specs=(_main_spec(tb, GROUP_W, 0, nb, False), _main_spec(tb, w3, 0, nb, False)),
        out_shape=(o_sds, jax.ShapeDtypeStruct((t, w3), F32)),
        scratch_shapes=[pltpu.VMEM((tb + 2 * HALO, w3), F32)] + batch_scratch,
        compiler_params=_scan_params(),
        name="gdn_fwd",
    )(u, u, u, u, conv_w, arow, brow)
    return pl.pallas_call(
        functools.partial(_gdn_kernel, True, True, tb, nb, seq),
        grid=(nb,),
        in_specs=[_main_spec(tb, w3, 0, nb, True), _main_spec(tb, 128, COL_GDN_SMALL, nb, True),
                  _const_spec((1, 128)), _const_spec((1, 128)),
                  _main_spec(tb, GROUP_W, COL_GDN_Z, nb, True), _main_spec(tb, GROUP_W, 0, nb, True),
                  _const_spec((1, GDN_DK)), _const_spec((1, GROUP_W))],
        out_specs=_main_spec(tb, GROUP_W, 0, nb, True),
        out_shape=o_sds,
        scratch_shapes=batch_scratch,
        compiler_params=_scan_params(),
        name="gdn_bwd",
    )(qkv_act, u, arow, brow, u, o_fwd, norm_w.reshape(1, -1), gn_w.reshape(1, -1))


def _lru_kernel(rev, final, tb, nb, seq, lane_x, lane_gate, *refs):
    if final:
        (x_ref, prev_ref, next_ref, cw_ref, cb_ref, w_ref, ba_ref, bi_ref, lam_ref,
         gate_ref, ofwd_ref, gn_ref, out_ref, xp_ref, h_ref) = refs
        gate_ref = _LaneWindow(gate_ref, lane_gate, GROUP_W)
    else:
        (x_ref, prev_ref, next_ref, cw_ref, cb_ref, w_ref, ba_ref, bi_ref, lam_ref,
         out_ref, xp_ref, h_ref) = refs
    x_ref, prev_ref, next_ref = (_LaneWindow(r, lane_x, GROUP_W) for r in (x_ref, prev_ref, next_ref))
    first, last = _block_pos(nb, tb, rev, seq)
    start = last if rev else first
    bw = LRU_BW

    _fill_pad(xp_ref, x_ref, prev_ref, next_ref, first, last, tb)
    xc = _conv4(xp_ref, cw_ref, tb) + cb_ref[...]

    @pl.when(start)
    def _():
        h_ref[...] = jnp.zeros_like(h_ref)

    rows = lax.broadcasted_iota(jnp.int32, (tb, bw), 0)
    tile_rows = lax.broadcasted_iota(jnp.int32, (tb // SUBLANES, SUBLANES, bw), 1)
    start_row = (tb - 1) if rev else 0
    is_start_row = (rows == start_row) & start
    sp = _softplus(-lam_ref[...])
    for n in range(LRU_BLOCKS):
        xb = xc[:, n * bw:(n + 1) * bw]
        pre = _dot1(xb, w_ref[n])
        rg = _sig(pre[:, :bw] + ba_ref[:, n * bw:(n + 1) * bw])
        ig = _sig(pre[:, bw:] + bi_ref[:, n * bw:(n + 1) * bw])
        log_a = (-LRU_C) * rg * sp[:, n * bw:(n + 1) * bw]
        a = jnp.exp(log_a)
        mult = jnp.sqrt((1.0 + a * a) * jnp.tanh(-log_a))
        mult = jnp.where(is_start_row, 1.0, mult)
        b = mult * (ig * xb)
        ntiles = tb // SUBLANES
        a = a.reshape(ntiles, SUBLANES, bw)
        b = b.reshape(ntiles, SUBLANES, bw)
        s = 1
        while s < SUBLANES:
            sh = (SUBLANES - s) if rev else s
            a_sh = pltpu.roll(a, sh, 1)
            b_sh = pltpu.roll(b, sh, 1)
            ok = (tile_rows < SUBLANES - s) if rev else (tile_rows >= s)
            b = jnp.where(ok, a * b_sh, 0.0) + b
            a = jnp.where(ok, a * a_sh, a)
            s *= 2
        h_prev = h_ref[:, n * bw:(n + 1) * bw]
        for ti in (range(ntiles - 1, -1, -1) if rev else range(ntiles)):
            ht = a[ti] * h_prev + b[ti]
            out_ref[ti * SUBLANES:(ti + 1) * SUBLANES, n * bw:(n + 1) * bw] = ht
            h_prev = ht[0:1, :] if rev else ht[SUBLANES - 1:SUBLANES, :]
        h_ref[:, n * bw:(n + 1) * bw] = h_prev

    if final:
        o = (out_ref[...] + ofwd_ref[...]) * _gelu_tanh(gate_ref[...])
        out_ref[...] = _rms(o, gn_ref[...])


def _lru(u, conv_w, conv_b, wa, ba, wi, bi, lam, gn_w, tb, seq):
    t = u.shape[0]
    nb = t // tb
    w = GROUP_W
    wcat = jnp.concatenate([wa, wi], axis=-1).astype(BF16)
    scratch = [pltpu.VMEM((tb + 2 * HALO, w), F32), pltpu.VMEM((1, w), F32)]

    gate_spec, lane_gate = _main_window(tb, w, COL_LRU_GATE, nb, True)

    def call(rev, final, extra_in, extra_specs):
        d = 1 if rev else 0
        prev, nxt = _halo_specs(tb, w, COL_LRU_X, nb, rev, t)
        x_spec, lane_x = _main_window(tb, w, COL_LRU_X, nb, rev)
        in_specs = [x_spec, prev, nxt,
                    _const_spec((4, w)), _const_spec((1, w)),
                    _const_spec((LRU_BLOCKS, LRU_BW, 2 * LRU_BW)),
                    _const_spec((1, w)), _const_spec((1, w)), _const_spec((1, w))] + extra_specs
        return pl.pallas_call(
            functools.partial(_lru_kernel, rev, final, tb, nb, seq, lane_x, lane_gate),
            grid=(nb,),
            in_specs=in_specs,
            out_specs=_main_spec(tb, w, 0, nb, rev),
            out_shape=jax.ShapeDtypeStruct((t, w), F32),
            scratch_shapes=scratch,
            compiler_params=_scan_params(),
            name="lru_bwd" if rev else "lru_fwd",
        )(u, u, u, conv_w, conv_b.reshape(1, w), wcat[d], ba[d].reshape(1, w),
          bi[d].reshape(1, w), lam[d].reshape(1, w), *extra_in)

    o_fwd = call(False, False, [], [])
    return call(True, True, [u, o_fwd, gn_w.reshape(1, w)],
                [gate_spec, _main_spec(tb, w, 0, nb, True),
                 _const_spec((1, w))])


def _ssd_kernel(rev, final, tb, nb, seq, lane_xbc, lane_z, *refs):
    if final:
        (xbc_s, small_ref, arow_ref, brow_ref,
         z_ref, ofwd_ref, dskip_ref, nw_ref, gn_ref, out_ref, st_ref) = refs
        z_ref = _LaneWindow(z_ref, lane_z, GROUP_W)
    else:
        (xbc_ref, prev_ref, next_ref, small_ref, cw_ref, cb_ref, arow_ref, brow_ref,
         out_ref, xbc_s, xp_ref, st_ref) = refs
        wx = GROUP_W + 2 * SSD_GROUPS * SSD_STATE
        xbc_ref, prev_ref, next_ref = (_LaneWindow(r, lane_xbc, wx) for r in (xbc_ref, prev_ref, next_ref))
    d = 1 if rev else 0
    c = CHUNK
    nc = tb // c
    di = GROUP_W
    p = SSD_HEADDIM
    gw = di // SSD_GROUPS
    first, last = _block_pos(nb, tb, rev, seq)

    assert p == c
    hpg = SSD_HEADS // SSD_GROUPS

    if not final:
        _fill_pad(xp_ref, xbc_ref, prev_ref, next_ref, first, last, tb)
        xbc_s[...] = _silu(_conv4(xp_ref, cw_ref, tb) + cb_ref[...])
    dt = _softplus(small_ref[...] + brow_ref[...])
    da = dt * (-jnp.exp(arow_ref[...]))

    rb, cb_ = _iota2((tb, tb))
    same = (rb >> 6) == (cb_ >> 6)
    incl_b = same & ((cb_ >= rb) if rev else (cb_ <= rb))
    acs = _dot_mask_l(jnp.where(incl_b, 1.0, 0.0).astype(BF16), da)
    tot = _dot_mask_l(jnp.where(same, 1.0, 0.0).astype(BF16), da)
    acs_t = acs.T
    sr, sc = _iota2((128, di))
    sel = jnp.where(sr == SM_DT + d * SSD_HEADS + (sc >> 6), 1.0, 0.0).astype(BF16)
    acs_x = _dot_mask_r(acs, sel)
    tot_x = _dot_mask_r(tot, sel)
    dt_x = _dot_mask_r(dt, sel)
    xdt = xbc_s[:, 0:di] * dt_x
    xdt_e = xdt * jnp.exp(tot_x - acs_x)
    e_acs = jnp.exp(acs_x)
    e_tot = jnp.exp(tot_x)
    xdt_b = _bf(xdt)

    @pl.when(last if rev else first)
    def _():
        st_ref[...] = jnp.zeros_like(st_ref)

    rr, lj = _iota2((c, gw))
    incl_cat = ((lj & (c - 1)) >= rr) if rev else ((lj & (c - 1)) <= rr)
    rbd, cbd = _iota2((gw, gw))
    head_diag = (rbd >> 6) == (cbd >> 6)

    gss = [slice(g * gw, (g + 1) * gw) for g in range(SSD_GROUPS)]
    pairs = [(cj, g) for cj in range(nc) for g in range(SSD_GROUPS)]
    rsl = {cj: slice(cj * c, (cj + 1) * c) for cj in range(nc)}
    bm = {(cj, g): _bf(xbc_s[rsl[cj], di + g * SSD_STATE:di + (g + 1) * SSD_STATE]) for cj, g in pairs}
    cm = {(cj, g): _bf(xbc_s[rsl[cj], di + (SSD_GROUPS + g) * SSD_STATE:di + (SSD_GROUPS + g + 1) * SSD_STATE])
          for cj, g in pairs}
    cbm = {k: _dot1_nt(cm[k], bm[k]) for k in pairs}
    kx = {(cj, g): _dot1_tn(bm[(cj, g)], xdt_e[rsl[cj], gss[g]]) for cj, g in pairs}
    y_d = {}
    for cj, g in pairs:
        rs = rsl[cj]
        cb_cat = jnp.concatenate([cbm[(cj, g)]] * hpg, axis=1)
        col0 = SM_DT + d * SSD_HEADS + g * hpg
        a_row = jnp.concatenate([acs_t[col0 + hh:col0 + hh + 1, rs] for hh in range(hpg)], axis=1)
        l_cat = jnp.where(incl_cat, jnp.exp(acs_x[rs, gss[g]] - a_row), 0.0)
        x_bd = jnp.where(head_diag, jnp.concatenate([xdt_b[rs, gss[g]]] * hpg, axis=0),
                         jnp.zeros((), BF16))
        y_d[(cj, g)] = _dot1(cb_cat * l_cat, x_bd)
    s = [st_ref[g] for g in range(SSD_GROUPS)]
    for ci in range(nc):
        cj = (nc - 1 - ci) if rev else ci
        rs = rsl[cj]
        y_off = [_dot(cm[(cj, g)], _bf(s[g])) for g in range(SSD_GROUPS)]
        for g in range(SSD_GROUPS):
            out_ref[rs, gss[g]] = y_d[(cj, g)] + y_off[g] * e_acs[rs, gss[g]]
            s[g] = s[g] * e_tot[cj * c:cj * c + 1, gss[g]] + kx[(cj, g)]
    for g in range(SSD_GROUPS):
        st_ref[g] = s[g]

    if final:
        y = out_ref[...] + ofwd_ref[...] + dskip_ref[...] * xbc_s[:, 0:di]
        y = y * _silu(z_ref[...])
        for g in range(SSD_GROUPS):
            out_ref[:, g * gw:(g + 1) * gw] = _rms(y[:, g * gw:(g + 1) * gw], nw_ref[:, g * gw:(g + 1) * gw])
        out_ref[...] = _rms(out_ref[...], gn_ref[...])


def _ssd(u, conv_w, conv_b, a_log, dt_bias, d_skip, norm_w, gn_w, tb, seq):
    t = u.shape[0]
    nb = t // tb
    wx = GROUP_W + 2 * SSD_GROUPS * SSD_STATE
    arow = jnp.zeros((1, 128), F32).at[0, SM_DT:SM_DT + 16].set(a_log.reshape(16))
    brow = jnp.zeros((1, 128), F32).at[0, SM_DT:SM_DT + 16].set(dt_bias.reshape(16))
    dsk = jnp.repeat(d_skip, SSD_HEADDIM).reshape(1, GROUP_W)
    state = pltpu.VMEM((SSD_GROUPS, SSD_STATE, GROUP_W // SSD_GROUPS), F32)
    w = GROUP_W
    o_sds = jax.ShapeDtypeStruct((t, w), F32)

    prev, nxt = _halo_specs(tb, wx, COL_SSD_XBC, nb, False, t)
    xbc_spec, lane_xbc = _main_window(tb, wx, COL_SSD_XBC, nb, False)
    z_spec, lane_z = _main_window(tb, w, COL_SSD_Z, nb, True)
    o_fwd, xbc_act = pl.pallas_call(
        functools.partial(_ssd_kernel, False, False, tb, nb, seq, lane_xbc, lane_z),
        grid=(nb,),
        in_specs=[xbc_spec, prev, nxt,
                  _main_spec(tb, 128, COL_SSD_SMALL, nb, False),
                  _const_spec((4, wx)), _const_spec((1, wx)),
                  _const_spec((1, 128)), _const_spec((1, 128))],
        out_specs=(_main_spec(tb, w, 0, nb, False), _main_spec(tb, wx, 0, nb, False)),
        out_shape=(o_sds, jax.ShapeDtypeStruct((t, wx), F32)),
        scratch_shapes=[pltpu.VMEM((tb + 2 * HALO, wx), F32), state],
        compiler_params=_scan_params(),
        name="ssd_fwd",
    )(u, u, u, u, conv_w, conv_b.reshape(1, wx), arow, brow)
    return pl.pallas_call(
        functools.partial(_ssd_kernel, True, True, tb, nb, seq, lane_xbc, lane_z),
        grid=(nb,),
        in_specs=[_main_spec(tb, wx, 0, nb, True), _main_spec(tb, 128, COL_SSD_SMALL, nb, True),
                  _const_spec((1, 128)), _const_spec((1, 128)),
                  z_spec, _main_spec(tb, w, 0, nb, True),
                  _const_spec((1, w)), _const_spec((1, w)), _const_spec((1, w))],
        out_specs=_main_spec(tb, w, 0, nb, True),
        out_shape=o_sds,
        scratch_shapes=[state],
        compiler_params=_scan_params(),
        name="ssd_bwd",
    )(xbc_act, u, arow, brow, u, o_fwd, dsk, norm_w.reshape(1, w), gn_w.reshape(1, w))


def _hgrn_kernel(rev, final, layer, tb, nb, seq, lanes, *refs):
    if final:
        (q_ref, f_ref, i_ref, lb_ref, mask_ref, g_ref, ofwd_ref, nw_ref, gn_ref, out_ref,
         d_s, z_s, qd_s, kd_s, st_ref) = refs
        g_ref = _LaneWindow(g_ref, lanes[3], GROUP_W)
    else:
        (q_ref, f_ref, i_ref, lb_ref, mask_ref, out_ref,
         d_s, z_s, qd_s, kd_s, st_ref) = refs
    q_ref, f_ref, i_ref = (_LaneWindow(r, lanes[n], GROUP_W) for n, r in enumerate((q_ref, f_ref, i_ref)))
    c = HGRN_STATE_CHUNK
    nl = HGRN_LEVELS
    nc = tb // c
    w = GROUP_W
    hw = HGRN_DK
    first, last = _block_pos(nb, tb, rev, seq)

    lbp = lb_ref[...]
    mx = jnp.max(lbp, axis=0, keepdims=True)
    ex = jnp.exp(lbp - mx)
    den = jnp.sum(ex, axis=0, keepdims=True)
    lb = jnp.zeros((1, w), F32)
    for m in range(1, layer + 1):
        lb = lb + ex[m:m + 1, :] / den

    f = lb + (1.0 - lb) * _sig(f_ref[...])
    logf = jnp.log(f)
    kk = 1.0 - f
    qq = _silu(q_ref[...])
    vv = i_ref[...]

    l1 = _bf(logf)
    l2 = _bf(logf - l1.astype(F32))

    d_s[...] = _dot(mask_ref[...], l1) + _dot(mask_ref[...], l2)

    def range_sum(blk):
        return d_s[blk * tb:(blk + 1) * tb, :]

    rows = lax.broadcasted_iota(jnp.int32, (tb, 1), 0)
    z_s[0] = _bf(qq)
    z_s[1] = _bf(kk)
    for lv in range(1, nl + 1):
        bit = (rows >> (lv - 1)) & 1
        is_query = (bit == 0) if rev else (bit == 1)
        z_s[lv + 1] = _bf(jnp.exp(range_sum(lv - 1)) * jnp.where(is_query, qq, kk))
    bsc = range_sum(nl)
    btot = range_sum(nl + 1)
    qd_s[...] = _bf(qq * jnp.exp(bsc))
    kd_s[...] = _bf(kk * jnp.exp(btot - bsc))
    gl = jnp.exp(btot)
    vb = _bf(vv)

    r64, c64 = _iota2((c, c))
    pair_masks = [r64 == c64]
    for lv in range(1, nl + 1):
        rbit = (r64 >> (lv - 1)) & 1
        cbit = (c64 >> (lv - 1)) & 1
        split = ((rbit == 0) & (cbit == 1)) if rev else ((rbit == 1) & (cbit == 0))
        pair_masks.append(((r64 >> lv) == (c64 >> lv)) & split)

    @pl.when(last if rev else first)
    def _():
        st_ref[...] = jnp.zeros_like(st_ref)

    hs = range(HGRN_HEADS)
    sls = [slice(h * hw, (h + 1) * hw) for h in hs]
    s = [st_ref[h] for h in hs]
    for ci in range(nc):
        cj = (nc - 1 - ci) if rev else ci
        rs = slice(cj * c, (cj + 1) * c)
        prods = [[_dot1_nt(z_s[0, rs, sls[h]], z_s[1, rs, sls[h]])]
                 + [_dot1_nt(z_s[lv + 1, rs, sls[h]], z_s[lv + 1, rs, sls[h]]) for lv in range(1, nl + 1)]
                 for h in hs]
        o_st = [_dot1_nt(qd_s[rs, sls[h]], s[h]) for h in hs]
        kv = [_dot1_tn(vb[rs, sls[h]], kd_s[rs, sls[h]]) for h in hs]
        sc = []
        for h in hs:
            acc = jnp.where(pair_masks[0], prods[h][0], 0.0)
            for lv in range(1, nl + 1):
                acc = acc + jnp.where(pair_masks[lv], prods[h][lv], 0.0)
            sc.append(_bf(acc))
        y_in = [_dot(sc[h], vb[rs, sls[h]]) for h in hs]
        for h in hs:
            out_ref[rs, sls[h]] = y_in[h] + o_st[h]
            s[h] = s[h] * gl[cj * c:cj * c + 1, sls[h]] + kv[h]
    for h in hs:
        st_ref[h] = s[h]

    if final:
        o = out_ref[...] + ofwd_ref[...]
        gg = g_ref[...]
        for h in range(HGRN_HEADS):
            sl = slice(h * hw, (h + 1) * hw)
            out_ref[:, sl] = _rms(o[:, sl], nw_ref[...]) * _silu(gg[:, sl])
        out_ref[...] = _rms(out_ref[...], gn_ref[...])


def _hgrn_masks(tb, rev):
    t = np.arange(tb)[:, None]
    r = np.arange(tb)[None, :]
    blocks = []
    for lv in range(1, HGRN_LEVELS + 1):
        g = 1 << lv
        mid = (t // g) * g + g // 2
        if rev:
            m = np.where(t < mid, (r >= t) & (r < mid), (r >= mid) & (r < t))
        else:
            m = np.where(t >= mid, (r >= mid) & (r <= t), (r > t) & (r < mid))
        blocks.append(m)
    same = (r // HGRN_STATE_CHUNK) == (t // HGRN_STATE_CHUNK)
    blocks.append(same & ((r >= t) if rev else (r <= t)))
    blocks.append(same)
    return jnp.asarray(np.concatenate(blocks, axis=0).astype(np.float32), dtype=BF16)


def _hgrn(u, lb_param, layer, norm_w, gn_w, tb, seq):
    t = u.shape[0]
    nb = t // tb
    w = GROUP_W
    nblk = HGRN_LEVELS + 2
    scratch = [pltpu.VMEM((nblk * tb, w), F32),
               pltpu.VMEM((nblk, tb, w), BF16), pltpu.VMEM((tb, w), BF16),
               pltpu.VMEM((tb, w), BF16),
               pltpu.VMEM((HGRN_HEADS, HGRN_DK, HGRN_DK), F32)]

    g_spec, lane_g = _main_window(tb, w, COL_HG_G, nb, True)

    def call(rev, final, extra_in, extra_specs):
        d = 1 if rev else 0
        q_spec, lane_q = _main_window(tb, w, COL_HG_Q, nb, rev)
        f_spec, lane_f = _main_window(tb, w, COL_HG_F + d * w, nb, rev)
        i_spec, lane_i = _main_window(tb, w, COL_HG_I, nb, rev)
        in_specs = [q_spec, f_spec, i_spec,
                    _const_spec((DEPTH, w)), _const_spec((nblk * tb, tb))] + extra_specs
        return pl.pallas_call(
            functools.partial(_hgrn_kernel, rev, final, layer, tb, nb, seq,
                              (lane_q, lane_f, lane_i, lane_g)),
            grid=(nb,),
            in_specs=in_specs,
            out_specs=_main_spec(tb, w, 0, nb, rev),
            out_shape=jax.ShapeDtypeStruct((t, w), F32),
            scratch_shapes=scratch,
            compiler_params=_scan_params(),
            name="hgrn_bwd" if rev else "hgrn_fwd",
        )(u, u, u, lb_param[d], _hgrn_masks(tb, rev), *extra_in)

    o_fwd = call(False, False, [], [])
    return call(True, True, [u, o_fwd, norm_w.reshape(1, -1), gn_w.reshape(1, w)],
                [g_spec, _main_spec(tb, w, 0, nb, True),
                 _const_spec((1, HGRN_DK)), _const_spec((1, w))])


def _outproj_kernel(starts, *refs):
    nsrc = len(starts)
    x_refs = refs[:nsrc]
    ya_ref, yb_ref, yc_ref, yd_ref, w_ref, o_ref = refs[nsrc:]
    acc = None
    for n, y_ref in enumerate((ya_ref, yb_ref, yc_ref, yd_ref)):
        part = _dot(_bf(y_ref[...]), w_ref[n * GROUP_W:(n + 1) * GROUP_W, :])
        acc = part if acc is None else acc + part

    def add_residual(x_ref):
        o_ref[...] = x_ref[...] + acc

    _select_rows(pl.program_id(0), x_refs, starts, add_residual)


def _outproj(xs, ys, w):
    d = xs[0].shape[1]
    t = sum(x.shape[0] for x in xs)
    tm = 512
    x_specs, starts = _row_source_specs(xs, tm)
    yspec = pl.BlockSpec((tm, GROUP_W), lambda i: (i, 0))
    return pl.pallas_call(
        functools.partial(_outproj_kernel, starts),
        grid=(t // tm,),
        in_specs=x_specs + [yspec, yspec, yspec, yspec, pl.BlockSpec(w.shape, lambda i: (0, 0))],
        out_specs=pl.BlockSpec((tm, d), lambda i: (i, 0)),
        out_shape=jax.ShapeDtypeStruct((t, d), F32),
        compiler_params=pltpu.CompilerParams(dimension_semantics=("arbitrary",),
                                             vmem_limit_bytes=VMEM_LIMIT),
        name="outproj",
    )(*xs, *ys, w)


def _ffn_kernel(apply_final, tm, nrb, nf, seq, out_starts, *refs):
    (x_ref, prev_ref, next_ref, ln_ref, wg_ref, wv_ref, cwg_ref, cwv_ref, cbg_ref, cbv_ref,
     wd_ref, fn_ref) = refs[:12]
    o_refs = refs[12:12 + len(out_starts)]
    h_s, g_s, v_s, acc_s = refs[12 + len(out_starts):]
    j = pl.program_id(1)
    i = pl.program_id(0)
    row0 = i * tm
    first = _seq_start(row0, seq)
    last = _seq_start(row0 + tm, seq)

    @pl.when(j == 0)
    def _():
        keep_p = jnp.where(first, 0.0, 1.0)
        keep_n = jnp.where(last, 0.0, 1.0)
        h_s[0:FHALO, :] = _bf(_rms(prev_ref[...], ln_ref[...]) * keep_p)
        h_s[FHALO:FHALO + tm, :] = _bf(_rms(x_ref[...], ln_ref[...]))
        h_s[FHALO + tm:2 * FHALO + tm, :] = _bf(_rms(next_ref[...], ln_ref[...]) * keep_n)
        acc_s[...] = jnp.zeros_like(acc_s)

    hb = h_s[...]
    g_s[...] = _dot(hb, wg_ref[...])
    v_s[...] = _dot(hb, wv_ref[...])

    def conv3(s_ref, cw_ref, cb_ref):
        acc = s_ref[FHALO - 1:FHALO - 1 + tm, :] * cw_ref[0:1, :]
        acc = acc + s_ref[FHALO:FHALO + tm, :] * cw_ref[1:2, :]
        acc = acc + s_ref[FHALO + 1:FHALO + 1 + tm, :] * cw_ref[2:3, :]
        return acc + cb_ref[...]

    act = _silu(conv3(g_s, cwg_ref, cbg_ref)) * conv3(v_s, cwv_ref, cbv_ref)
    acc_s[...] += _dot(_bf(act), wd_ref[...])

    @pl.when(j == nf - 1)
    def _():
        y = x_ref[...] + acc_s[...]
        if apply_final:
            y = _rms(y, fn_ref[...])

        def write(o_ref):
            o_ref[...] = y

        _select_rows(i, o_refs, out_starts, write)


def _ffn(x, ln, w_up, conv_w, conv_b, w_down, final_w, apply_final, seq, out_rows):
    t, d = x.shape
    tm = 512
    tf = 512
    nf = D_FF // tf
    nrb = t // tm
    r = tm // FHALO
    cb = conv_b.reshape(1, 2 * D_FF)
    up_buf = pltpu.VMEM((tm + 2 * FHALO, tf), F32)
    assert sum(out_rows) == t
    out_specs, out_starts = _row_source_specs([jax.ShapeDtypeStruct((n, d), F32) for n in out_rows], tm)
    return pl.pallas_call(
        functools.partial(_ffn_kernel, apply_final, tm, nrb, nf, seq, out_starts),
        grid=(nrb, nf),
        in_specs=[pl.BlockSpec((tm, d), lambda i, j: (i, 0)),
                  pl.BlockSpec((FHALO, d), lambda i, j: (jnp.maximum(i * r - 1, 0), 0)),
                  pl.BlockSpec((FHALO, d), lambda i, j: (jnp.minimum((i + 1) * r, t // FHALO - 1), 0)),
                  pl.BlockSpec((1, d), lambda i, j: (0, 0)),
                  pl.BlockSpec((d, tf), lambda i, j: (0, j)),
                  pl.BlockSpec((d, tf), lambda i, j: (0, j + nf)),
                  pl.BlockSpec((3, tf), lambda i, j: (0, j)),
                  pl.BlockSpec((3, tf), lambda i, j: (0, j + nf)),
                  pl.BlockSpec((1, tf), lambda i, j: (0, j)),
                  pl.BlockSpec((1, tf), lambda i, j: (0, j + nf)),
                  pl.BlockSpec((tf, d), lambda i, j: (j, 0)),
                  pl.BlockSpec((1, d), lambda i, j: (0, 0))],
        out_specs=tuple(out_specs),
        out_shape=tuple(jax.ShapeDtypeStruct((n, d), F32) for n in out_rows),
        scratch_shapes=[pltpu.VMEM((tm + 2 * FHALO, d), BF16), up_buf, up_buf,
                        pltpu.VMEM((tm, d), F32)],
        compiler_params=pltpu.CompilerParams(dimension_semantics=("arbitrary", "arbitrary"),
                                             vmem_limit_bytes=VMEM_LIMIT),
        name="ffn",
    )(x, x, x, ln.reshape(1, d), w_up, w_up, conv_w, conv_w, cb, cb, w_down, final_w.reshape(1, d))


def _pad_w_in(w):
    assert w.shape[1] == D_IN
    split = COL_HG_Q - HG_SHIFT
    return jnp.concatenate([w[:, :split], jnp.zeros((w.shape[0], HG_SHIFT), w.dtype), w[:, split:]], axis=1)


def kernel(x_prompt, x_sample, ln1, w_in, gdn_conv_w, gdn_a_log, gdn_dt_bias, gdn_norm_w, lru_conv_w, lru_conv_b, lru_wa, lru_ba, lru_wi, lru_bi, lru_lambda, ssd_conv_w, ssd_conv_b, ssd_a_log, ssd_dt_bias, ssd_d, ssd_norm_w, hgrn_lb, hgrn_norm_w, group_norm_w, w_out, ln2, w_up, ffn_conv_w, ffn_conv_b, w_down, final_norm):
    b, s, dm = x_prompt.shape
    db, ds, _ = x_sample.shape
    seq = (b * s, s, ds)
    assert s % TB_SCAN == 0 and ds % TB_SCAN == 0 and s % TB_HGRN == 0 and ds % TB_HGRN == 0
    xs = [x_prompt.reshape(b * s, dm), x_sample.reshape(db * ds, dm)]
    depth = w_in.shape[0]
    for l in range(depth):
        last_layer = l == depth - 1
        u = _inproj(xs, ln1[l], _pad_w_in(w_in[l].astype(BF16)))
        gn = group_norm_w[l]
        ya = _gdn(u, gdn_conv_w[l], gdn_a_log[l], gdn_dt_bias[l], gdn_norm_w[l], gn[0], TB_SCAN, seq)
        yb = _lru(u, lru_conv_w[l], lru_conv_b[l], lru_wa[l], lru_ba[l], lru_wi[l], lru_bi[l],
                  lru_lambda[l], gn[1], TB_SCAN, seq)
        yc = _ssd(u, ssd_conv_w[l], ssd_conv_b[l], ssd_a_log[l], ssd_dt_bias[l], ssd_d[l],
                  ssd_norm_w[l], gn[2], TB_SCAN, seq)
        yd = _hgrn(u, hgrn_lb, l, hgrn_norm_w[l], gn[3], TB_HGRN, seq)
        x = _outproj(xs, (ya, yb, yc, yd), w_out[l].astype(BF16))
        out_rows = [b * s, db * ds] if last_layer else [b * s + db * ds]
        xs = list(_ffn(x, ln2[l], w_up[l].astype(BF16), ffn_conv_w[l], ffn_conv_b[l],
                       w_down[l].astype(BF16), final_norm, last_layer, seq, out_rows))
    return (xs[0].reshape(b, s, dm), xs[1].reshape(db, ds, dm))
```

```python
import functools
import math

import jax
import jax.numpy as jnp
import numpy as np
from jax import lax
from jax.experimental import pallas as pl
from jax.experimental.pallas import tpu as pltpu

F32 = jnp.float32
BF16 = jnp.bfloat16

D_MODEL = 2048
DEPTH = 2
GROUP_W = 512
GDN_HEADS = 4
GDN_DK = 128
LRU_BLOCKS = 4
LRU_BW = 128
LRU_C = 8.0
SSD_HEADS = 8
SSD_HEADDIM = 64
SSD_GROUPS = 2
SSD_STATE = 128
HGRN_HEADS = 4
HGRN_DK = 128
D_FF = 5632
CHUNK = 64
HGRN_LEVELS = 6
HGRN_STATE_CHUNK = 1 << HGRN_LEVELS
EPS = 1e-6

COL_GDN_QKV = 0
COL_GDN_Z = 1536
COL_GDN_SMALL = 2048
COL_LRU_X = 2064
COL_LRU_GATE = 2576
COL_SSD_Z = 3088
COL_SSD_XBC = 3600
COL_SSD_SMALL = 4608
HG_SHIFT = 96
COL_HG_Q = 4640 + HG_SHIFT
COL_HG_F = 5152 + HG_SHIFT
COL_HG_I = 6176 + HG_SHIFT
COL_HG_G = 6688 + HG_SHIFT
D_IN = 7200
D_IN_P = D_IN + HG_SHIFT
SM_BETA = 0
SM_ALPHA = 8
SM_DT = 16

SUBLANES = 8
HALO = 8
FHALO = 16
VMEM_LIMIT = 56 * 1024 * 1024
TB_SCAN = 512
TB_HGRN = 256

def _sig(x):
    return 1.0 / (1.0 + jnp.exp(-x))


def _silu(x):
    return x * _sig(x)


def _softplus(x):
    return jnp.maximum(x, 0.0) + jnp.log1p(jnp.exp(-jnp.abs(x)))


def _gelu_tanh(x):
    c = math.sqrt(2.0 / math.pi)
    return 0.5 * x * (1.0 + jnp.tanh(c * (x + 0.044715 * (x * x * x))))


def _bf(x):
    return x.astype(BF16)


def _dot(a, b):
    return jnp.dot(a, b, preferred_element_type=F32)


def _dot1(a, b):
    return _dot(_bf(a), _bf(b))


def _dot1_nt(a, b):
    return lax.dot_general(_bf(a), _bf(b), (((1,), (1,)), ((), ())),
                           preferred_element_type=F32)


def _dot1_tn(a, b):
    return lax.dot_general(_bf(a), _bf(b), (((0,), (0,)), ((), ())),
                           preferred_element_type=F32)


def _dot_mask_l(m, b):
    b1 = _bf(b)
    r1 = b - b1.astype(F32)
    b2 = _bf(r1)
    b3 = _bf(r1 - b2.astype(F32))
    return _dot(m, b1) + (_dot(m, b2) + _dot(m, b3))


def _dot_mask_r(a, m):
    a1 = _bf(a)
    r1 = a - a1.astype(F32)
    a2 = _bf(r1)
    a3 = _bf(r1 - a2.astype(F32))
    return _dot(a1, m) + (_dot(a2, m) + _dot(a3, m))


def _rms(x, w):
    return x * lax.rsqrt(jnp.mean(x * x, axis=-1, keepdims=True) + EPS) * w


def _seq_start(row, seq):
    p_rows, s_len, ds_len = seq
    return jnp.where(row < p_rows, lax.rem(row, s_len) == 0,
                     lax.rem(row - p_rows, ds_len) == 0)


def _block_pos(nb, tb, rev, seq):
    i = pl.program_id(0)
    j = (nb - 1 - i) if rev else i
    row0 = j * tb
    first = _seq_start(row0, seq)
    last = _seq_start(row0 + tb, seq)
    return first, last


def _fill_pad(xp_ref, blk_ref, prev_ref, next_ref, first, last, tb):
    keep_p = jnp.where(first, 0.0, 1.0)
    keep_n = jnp.where(last, 0.0, 1.0)
    xp_ref[0:HALO, :] = prev_ref[...] * keep_p
    xp_ref[HALO:HALO + tb, :] = blk_ref[...]
    xp_ref[HALO + tb:2 * HALO + tb, :] = next_ref[...] * keep_n


def _conv4(xp_ref, w_ref, tb):
    acc = xp_ref[HALO - 2:HALO - 2 + tb, :] * w_ref[0:1, :]
    acc = acc + xp_ref[HALO - 1:HALO - 1 + tb, :] * w_ref[1:2, :]
    acc = acc + xp_ref[HALO:HALO + tb, :] * w_ref[2:3, :]
    acc = acc + xp_ref[HALO + 1:HALO + 1 + tb, :] * w_ref[3:4, :]
    return acc


def _iota2(shape):
    return (lax.broadcasted_iota(jnp.int32, shape, 0),
            lax.broadcasted_iota(jnp.int32, shape, 1))


def _bdot(a, b):
    return lax.dot_general(a, b, (((2,), (1,)), ((0,), (0,))), preferred_element_type=F32)


def _bdot1_nt(a, b):
    return lax.dot_general(_bf(a), _bf(b), (((2,), (2,)), ((0,), (0,))),
                           preferred_element_type=F32)


def _bdot1(a, b):
    return _bdot(_bf(a), _bf(b))


def _unit_tri_inverse(a, r, c, n):
    eye = jnp.where(r == c, 1.0, 0.0)[None]
    d = jnp.where(((r >> 3) == (c >> 3))[None], a, 0.0)
    d2 = _bdot1(d, d)
    x = eye - d
    x = x + _bdot1(x, d2)
    d4 = _bdot1(d2, d2)
    x = x + _bdot1(x, d4)
    b = 16
    sh = 4
    while b <= n:
        joined = ((r >> sh) == (c >> sh)) & ((r >> (sh - 1)) != (c >> (sh - 1)))
        e = jnp.where(joined[None], a, 0.0)
        x = x - _bdot1(_bdot1(x, e), x)
        b *= 2
        sh += 1
    return x


def _row_source_specs(xs, tm):
    d = xs[0].shape[1]
    starts = []
    specs = []
    off = 0
    for x in xs:
        assert x.shape[0] % tm == 0
        nblk = x.shape[0] // tm

        def imap(i, *_, off=off, nblk=nblk):
            return (jnp.clip(i - off, 0, nblk - 1), 0)

        specs.append(pl.BlockSpec((tm, d), imap))
        starts.append(off)
        off += nblk
    return specs, starts


def _select_rows(i, refs, starts, fn):
    for n, ref in enumerate(refs):
        lo = starts[n]
        hi = starts[n + 1] if n + 1 < len(starts) else None
        cond = i >= lo if hi is None else ((i >= lo) & (i < hi))
        if len(refs) == 1:
            fn(ref)
        else:
            pl.when(cond)(functools.partial(fn, ref))


def _inproj_kernel(starts, *refs):
    nsrc = len(starts)
    x_refs = refs[:nsrc]
    ln_ref, w_ref, o_ref, h_s = refs[nsrc:]

    def norm_rows(x_ref):
        h_s[...] = _bf(_rms(x_ref[...], ln_ref[...]))

    @pl.when(pl.program_id(1) == 0)
    def _():
        _select_rows(pl.program_id(0), x_refs, starts, norm_rows)

    o_ref[...] = _dot(h_s[...], w_ref[...])


def _inproj(xs, ln, w, layer):
    d = xs[0].shape[1]
    t = sum(x.shape[0] for x in xs)
    n = w.shape[2]
    tm = 512
    tn = n // 3
    x_specs, starts = _row_source_specs(xs, tm)
    return pl.pallas_call(
        functools.partial(_inproj_kernel, starts),
        grid=(t // tm, n // tn),
        in_specs=x_specs + [pl.BlockSpec((1, d), lambda i, j: (0, 0)),
                            pl.BlockSpec((None, d, tn), lambda i, j: (layer, 0, j))],
        out_specs=pl.BlockSpec((tm, tn), lambda i, j: (i, j)),
        out_shape=jax.ShapeDtypeStruct((t, n), F32),
        scratch_shapes=[pltpu.VMEM((tm, d), BF16)],
        compiler_params=pltpu.CompilerParams(
            dimension_semantics=("arbitrary", "arbitrary"),
            vmem_limit_bytes=VMEM_LIMIT),
        name="inproj",
    )(*xs, ln.reshape(1, d), w)


def _bidx(nb, rev):
    if rev:
        return lambda i: nb - 1 - i
    return lambda i: i


LANES = 128


def _window_spec(rows, w, col, row_block):
    if col % w == 0:
        return pl.BlockSpec((rows, w), lambda i: (row_block(i), col // w)), 0
    if col % LANES == 0:
        return pl.BlockSpec((pl.Element(rows), pl.Element(w)), lambda i: (row_block(i) * rows, col)), 0
    wide = w + LANES
    start = min(col - col % LANES, D_IN_P - wide)
    spec = pl.BlockSpec((pl.Element(rows), pl.Element(wide)), lambda i: (row_block(i) * rows, start))
    return spec, col - start


class _LaneWindow:
    def __init__(self, ref, lane0, w):
        self.ref, self.lane0, self.w = ref, lane0, w

    def __getitem__(self, idx):
        assert idx is Ellipsis
        return self.ref[:, self.lane0:self.lane0 + self.w]


def _main_spec(tb, w, col, nb, rev):
    spec, lane0 = _window_spec(tb, w, col, _bidx(nb, rev))
    assert lane0 == 0
    return spec


def _main_window(tb, w, col, nb, rev):
    return _window_spec(tb, w, col, _bidx(nb, rev))


def _halo_specs(tb, w, col, nb, rev, t):
    f = _bidx(nb, rev)
    r = tb // HALO
    prev, _ = _window_spec(HALO, w, col, lambda i: jnp.maximum(f(i) * r - 1, 0))
    nxt, _ = _window_spec(HALO, w, col, lambda i: jnp.minimum((f(i) + 1) * r, t // HALO - 1))
    return prev, nxt


def _const_spec(shape):
    nd = len(shape)
    return pl.BlockSpec(shape, lambda i: (0,) * nd)


def _scan_params():
    return pltpu.CompilerParams(dimension_semantics=("arbitrary",),
                                vmem_limit_bytes=VMEM_LIMIT)


def _gdn_kernel(rev, final, tb, nb, seq, *refs):
    if final:
        (qkv_in_ref, small_ref, arow_ref, brow_ref,
         z_ref, ofwd_ref, nw_ref, gn_ref, out_ref,
         q3, k3, v3, bet3, gi3, gt3, gj3, w3, u3, qk3, st_ref) = refs
    else:
        (qkv_ref, prev_ref, next_ref, small_ref, cw_ref, arow_ref, brow_ref, out_ref, qkv_out_ref,
         xp_ref, q3, k3, v3, bet3, gi3, gt3, gj3, w3, u3, qk3, st_ref) = refs
    d = 1 if rev else 0
    c = CHUNK
    c2 = 2 * CHUNK
    npair = tb // c2
    nh = GDN_HEADS
    hw = GDN_DK
    kw = GDN_HEADS * GDN_DK
    first, last = _block_pos(nb, tb, rev, seq)

    if not final:
        _fill_pad(xp_ref, qkv_ref, prev_ref, next_ref, first, last, tb)
        y = _silu(_conv4(xp_ref, cw_ref, tb))
    for h in range(nh):
        if final:
            qh = qkv_in_ref[:, h * hw:(h + 1) * hw]
            kh = qkv_in_ref[:, kw + h * hw:kw + (h + 1) * hw]
            vh = qkv_in_ref[:, 2 * kw + h * hw:2 * kw + (h + 1) * hw]
        else:
            qh = y[:, h * hw:(h + 1) * hw]
            qh = qh * lax.rsqrt(jnp.sum(qh * qh, axis=-1, keepdims=True) + EPS) * (GDN_DK ** -0.5)
            kh = y[:, kw + h * hw:kw + (h + 1) * hw]
            kh = kh * lax.rsqrt(jnp.sum(kh * kh, axis=-1, keepdims=True) + EPS)
            vh = y[:, 2 * kw + h * hw:2 * kw + (h + 1) * hw]
            qkv_out_ref[:, h * hw:(h + 1) * hw] = qh
            qkv_out_ref[:, kw + h * hw:kw + (h + 1) * hw] = kh
            qkv_out_ref[:, 2 * kw + h * hw:2 * kw + (h + 1) * hw] = vh
        for pi in range(npair):
            q3[pi * nh + h] = qh[pi * c2:(pi + 1) * c2, :]
            k3[pi * nh + h] = kh[pi * c2:(pi + 1) * c2, :]
            v3[pi * nh + h] = vh[pi * c2:(pi + 1) * c2, :]

    sm = small_ref[...]
    beta_all = _sig(sm)
    g_all = -jnp.exp(arow_ref[...]) * _softplus(sm + brow_ref[...])
    rb, cb_ = _iota2((tb, tb))
    same = (rb >> 6) == (cb_ >> 6)
    incl_b = same & ((cb_ >= rb) if rev else (cb_ <= rb))
    gcum = _dot_mask_l(jnp.where(incl_b, 1.0, 0.0).astype(BF16), g_all)
    gtot = _dot_mask_l(jnp.where(same, 1.0, 0.0).astype(BF16), g_all)
    gcum_t = gcum.T
    for pi in range(npair):
        rows = slice(pi * c2, (pi + 1) * c2)
        for h in range(nh):
            cb = SM_BETA + d * nh + h
            cg = SM_ALPHA + d * nh + h
            b = pi * nh + h
            bet3[b] = jnp.broadcast_to(beta_all[rows, cb:cb + 1], (c2, hw))
            gi3[b] = jnp.broadcast_to(gcum[rows, cg:cg + 1], (c2, hw))
            gt3[b] = jnp.broadcast_to(gtot[rows, cg:cg + 1], (c2, hw))
            gj3[b] = jnp.broadcast_to(gcum_t[cg:cg + 1, rows], (c2, c2))

    @pl.when(last if rev else first)
    def _():
        st_ref[...] = jnp.zeros_like(st_ref)

    r, cc = _iota2((c2, c2))
    same_chunk = (r >> 6) == (cc >> 6)
    incl = same_chunk & ((cc >= r) if rev else (cc <= r))
    strict = same_chunk & ((cc > r) if rev else (cc < r))
    q = q3[...]
    k = k3[...]
    bet = bet3[...]
    gi = gi3[...]
    decay = jnp.where(incl[None], jnp.exp(gi - gj3[...]), 0.0)
    kb = k * bet
    a = jnp.where(strict[None], _bdot1_nt(kb, k) * decay, 0.0)
    tm = _unit_tri_inverse(a, r, cc, c)
    eg = jnp.exp(gi)
    w3[...] = _bdot1(tm, kb * eg)
    u3[...] = _bdot1(tm, v3[...] * bet)
    qk3[...] = _bdot1_nt(q, k) * decay
    q3[...] = q * eg
    k3[...] = k * jnp.exp(gt3[...] - gi)

    def chunk_pair(pi, carry):
        pj = (npair - 1 - pi) if rev else pi
        hs = range(nh)
        for half in ((1, 0) if rev else (0, 1)):
            r0 = pl.multiple_of(pj * c2 + half * c, c)
            hr = slice(half * c, (half + 1) * c)
            s = [st_ref[h] for h in hs]
            sb = [_bf(s[h]) for h in hs]
            ws = [_dot(_bf(w3[pj * nh + h, hr, :]), sb[h]) for h in hs]
            qs = [_dot(_bf(q3[pj * nh + h, hr, :]), sb[h]) for h in hs]
            v_new = [_bf(u3[pj * nh + h, hr, :] - ws[h]) for h in hs]
            kv = [_dot1_tn(k3[pj * nh + h, hr, :], v_new[h]) for h in hs]
            ov = [_dot(_bf(qk3[pj * nh + h, hr, hr]), v_new[h]) for h in hs]
            for h in hs:
                st_ref[h] = s[h] * jnp.exp(gt3[pj * nh + h, half * c:half * c + 1, :]) + kv[h]
                out_ref[pl.ds(r0, c), h * hw:(h + 1) * hw] = qs[h] + ov[h]
        return carry

    lax.fori_loop(0, npair, chunk_pair, 0)

    if final:
        o = out_ref[...] + ofwd_ref[...]
        z = z_ref[...]
        for h in range(GDN_HEADS):
            oh = _rms(o[:, h * hw:(h + 1) * hw], nw_ref[...])
            out_ref[:, h * hw:(h + 1) * hw] = oh * _silu(z[:, h * hw:(h + 1) * hw])
        out_ref[...] = _rms(out_ref[...], gn_ref[...])


def _gdn(u, conv_w, a_log, dt_bias, norm_w, gn_w, tb, seq):
    t = u.shape[0]
    nb = t // tb
    w3 = 3 * GROUP_W
    arow = jnp.zeros((1, 128), F32).at[0, SM_ALPHA:SM_ALPHA + 8].set(a_log.reshape(8))
    brow = jnp.zeros((1, 128), F32).at[0, SM_ALPHA:SM_ALPHA + 8].set(dt_bias.reshape(8))
    assert GDN_DK == 2 * CHUNK
    nbat = (tb // (2 * CHUNK)) * GDN_HEADS
    entry = pltpu.VMEM((nbat, 2 * CHUNK, GDN_DK), F32)
    batch_scratch = [entry] * 10 + [pltpu.VMEM((GDN_HEADS, GDN_DK, GDN_DK), F32)]
    o_sds = jax.ShapeDtypeStruct((t, GROUP_W), F32)

    prev, nxt = _halo_specs(tb, w3, COL_GDN_QKV, nb, False, t)
    o_fwd, qkv_act = pl.pallas_call(
        functools.partial(_gdn_kernel, False, False, tb, nb, seq),
        grid=(nb,),
        in_specs=[_main_spec(tb, w3, COL_GDN_QKV, nb, False), prev, nxt,
                  _main_spec(tb, 128, COL_GDN_SMALL, nb, False),
                  _const_spec((4, w3)), _const_spec((1, 128)), _const_spec((1, 128))],
        out_specs=(_main_spec(tb, GROUP_W, 0, nb, False), _main_spec(tb, w3, 0, nb, False)),
        out_shape=(o_sds, jax.ShapeDtypeStruct((t, w3), F32)),
        scratch_shapes=[pltpu.VMEM((tb + 2 * HALO, w3), F32)] + batch_scratch,
        compiler_params=_scan_params(),
        name="gdn_fwd",
    )(u, u, u, u, conv_w, arow, brow)
    return pl.pallas_call(
        functools.partial(_gdn_kernel, True, True, tb, nb, seq),
        grid=(nb,),
        in_specs=[_main_spec(tb, w3, 0, nb, True), _main_spec(tb, 128, COL_GDN_SMALL, nb, True),
                  _const_spec((1, 128)), _const_spec((1, 128)),
                  _main_spec(tb, GROUP_W, COL_GDN_Z, nb, True), _main_spec(tb, GROUP_W, 0, nb, True),
                  _const_spec((1, GDN_DK)), _const_spec((1, GROUP_W))],
        out_specs=_main_spec(tb, GROUP_W, 0, nb, True),
        out_shape=o_sds,
        scratch_shapes=batch_scratch,
        compiler_params=_scan_params(),
        name="gdn_bwd",
    )(qkv_act, u, arow, brow, u, o_fwd, norm_w.reshape(1, -1), gn_w.reshape(1, -1))


def _lru_kernel(rev, final, tb, nb, seq, lane_x, lane_gate, *refs):
    if final:
        (x_ref, prev_ref, next_ref, cw_ref, cb_ref, w_ref, ba_ref, bi_ref, lam_ref,
         gate_ref, ofwd_ref, gn_ref, out_ref, xp_ref, h_ref) = refs
        gate_ref = _LaneWindow(gate_ref, lane_gate, GROUP_W)
    else:
        (x_ref, prev_ref, next_ref, cw_ref, cb_ref, w_ref, ba_ref, bi_ref, lam_ref,
         out_ref, xp_ref, h_ref) = refs
    x_ref, prev_ref, next_ref = (_LaneWindow(r, lane_x, GROUP_W) for r in (x_ref, prev_ref, next_ref))
    first, last = _block_pos(nb, tb, rev, seq)
    start = last if rev else first
    bw = LRU_BW

    _fill_pad(xp_ref, x_ref, prev_ref, next_ref, first, last, tb)
    xc = _conv4(xp_ref, cw_ref, tb) + cb_ref[...]

    @pl.when(start)
    def _():
        h_ref[...] = jnp.zeros_like(h_ref)

    rows = lax.broadcasted_iota(jnp.int32, (tb, bw), 0)
    tile_rows = lax.broadcasted_iota(jnp.int32, (tb // SUBLANES, SUBLANES, bw), 1)
    start_row = (tb - 1) if rev else 0
    is_start_row = (rows == start_row) & start
    sp = _softplus(-lam_ref[...])
    for n in range(LRU_BLOCKS):
        xb = xc[:, n * bw:(n + 1) * bw]
        pre = _dot1(xb, w_ref[n])
        rg = _sig(pre[:, :bw] + ba_ref[:, n * bw:(n + 1) * bw])
        ig = _sig(pre[:, bw:] + bi_ref[:, n * bw:(n + 1) * bw])
        log_a = (-LRU_C) * rg * sp[:, n * bw:(n + 1) * bw]
        a = jnp.exp(log_a)
        mult = jnp.sqrt((1.0 + a * a) * jnp.tanh(-log_a))
        mult = jnp.where(is_start_row, 1.0, mult)
        b = mult * (ig * xb)
        ntiles = tb // SUBLANES
        a = a.reshape(ntiles, SUBLANES, bw)
        b = b.reshape(ntiles, SUBLANES, bw)
        s = 1
        while s < SUBLANES:
            sh = (SUBLANES - s) if rev else s
            a_sh = pltpu.roll(a, sh, 1)
            b_sh = pltpu.roll(b, sh, 1)
            ok = (tile_rows < SUBLANES - s) if rev else (tile_rows >= s)
            b = jnp.where(ok, a * b_sh, 0.0) + b
            a = jnp.where(ok, a * a_sh, a)
            s *= 2
        h_prev = h_ref[:, n * bw:(n + 1) * bw]
        for ti in (range(ntiles - 1, -1, -1) if rev else range(ntiles)):
            ht = a[ti] * h_prev + b[ti]
            out_ref[ti * SUBLANES:(ti + 1) * SUBLANES, n * bw:(n + 1) * bw] = ht
            h_prev = ht[0:1, :] if rev else ht[SUBLANES - 1:SUBLANES, :]
        h_ref[:, n * bw:(n + 1) * bw] = h_prev

    if final:
        o = (out_ref[...] + ofwd_ref[...]) * _gelu_tanh(gate_ref[...])
        out_ref[...] = _rms(o, gn_ref[...])


def _lru(u, conv_w, conv_b, wa, ba, wi, bi, lam, gn_w, tb, seq):
    t = u.shape[0]
    nb = t // tb
    w = GROUP_W
    wcat = jnp.concatenate([wa, wi], axis=-1).astype(BF16)
    scratch = [pltpu.VMEM((tb + 2 * HALO, w), F32), pltpu.VMEM((1, w), F32)]

    gate_spec, lane_gate = _main_window(tb, w, COL_LRU_GATE, nb, True)

    def call(rev, final, extra_in, extra_specs):
        d = 1 if rev else 0
        prev, nxt = _halo_specs(tb, w, COL_LRU_X, nb, rev, t)
        x_spec, lane_x = _main_window(tb, w, COL_LRU_X, nb, rev)
        in_specs = [x_spec, prev, nxt,
                    _const_spec((4, w)), _const_spec((1, w)),
                    _const_spec((LRU_BLOCKS, LRU_BW, 2 * LRU_BW)),
                    _const_spec((1, w)), _const_spec((1, w)), _const_spec((1, w))] + extra_specs
        return pl.pallas_call(
            functools.partial(_lru_kernel, rev, final, tb, nb, seq, lane_x, lane_gate),
            grid=(nb,),
            in_specs=in_specs,
            out_specs=_main_spec(tb, w, 0, nb, rev),
            out_shape=jax.ShapeDtypeStruct((t, w), F32),
            scratch_shapes=scratch,
            compiler_params=_scan_params(),
            name="lru_bwd" if rev else "lru_fwd",
        )(u, u, u, conv_w, conv_b.reshape(1, w), wcat[d], ba[d].reshape(1, w),
          bi[d].reshape(1, w), lam[d].reshape(1, w), *extra_in)

    o_fwd = call(False, False, [], [])
    return call(True, True, [u, o_fwd, gn_w.reshape(1, w)],
                [gate_spec, _main_spec(tb, w, 0, nb, True),
                 _const_spec((1, w))])


def _ssd_kernel(rev, final, tb, nb, seq, lane_xbc, lane_z, *refs):
    if final:
        (xbc_s, small_ref, arow_ref, brow_ref,
         z_ref, ofwd_ref, dskip_ref, nw_ref, gn_ref, out_ref, st_ref) = refs
        z_ref = _LaneWindow(z_ref, lane_z, GROUP_W)
    else:
        (xbc_ref, prev_ref, next_ref, small_ref, cw_ref, cb_ref, arow_ref, brow_ref,
         out_ref, xbc_s, xp_ref, st_ref) = refs
        wx = GROUP_W + 2 * SSD_GROUPS * SSD_STATE
        xbc_ref, prev_ref, next_ref = (_LaneWindow(r, lane_xbc, wx) for r in (xbc_ref, prev_ref, next_ref))
    d = 1 if rev else 0
    c = CHUNK
    nc = tb // c
    di = GROUP_W
    p = SSD_HEADDIM
    gw = di // SSD_GROUPS
    first, last = _block_pos(nb, tb, rev, seq)

    assert p == c
    hpg = SSD_HEADS // SSD_GROUPS

    if not final:
        _fill_pad(xp_ref, xbc_ref, prev_ref, next_ref, first, last, tb)
        xbc_s[...] = _silu(_conv4(xp_ref, cw_ref, tb) + cb_ref[...])
    dt = _softplus(small_ref[...] + brow_ref[...])
    da = dt * (-jnp.exp(arow_ref[...]))

    rb, cb_ = _iota2((tb, tb))
    same = (rb >> 6) == (cb_ >> 6)
    incl_b = same & ((cb_ >= rb) if rev else (cb_ <= rb))
    acs = _dot_mask_l(jnp.where(incl_b, 1.0, 0.0).astype(BF16), da)
    tot = _dot_mask_l(jnp.where(same, 1.0, 0.0).astype(BF16), da)
    acs_t = acs.T
    sr, sc = _iota2((128, di))
    sel = jnp.where(sr == SM_DT + d * SSD_HEADS + (sc >> 6), 1.0, 0.0).astype(BF16)
    acs_x = _dot_mask_r(acs, sel)
    tot_x = _dot_mask_r(tot, sel)
    dt_x = _dot_mask_r(dt, sel)
    xdt = xbc_s[:, 0:di] * dt_x
    xdt_e = xdt * jnp.exp(tot_x - acs_x)
    e_acs = jnp.exp(acs_x)
    e_tot = jnp.exp(tot_x)
    xdt_b = _bf(xdt)

    @pl.when(last if rev else first)
    def _():
        st_ref[...] = jnp.zeros_like(st_ref)

    rr, lj = _iota2((c, gw))
    incl_cat = ((lj & (c - 1)) >= rr) if rev else ((lj & (c - 1)) <= rr)
    rbd, cbd = _iota2((gw, gw))
    head_diag = (rbd >> 6) == (cbd >> 6)

    gss = [slice(g * gw, (g + 1) * gw) for g in range(SSD_GROUPS)]
    pairs = [(cj, g) for cj in range(nc) for g in range(SSD_GROUPS)]
    rsl = {cj: slice(cj * c, (cj + 1) * c) for cj in range(nc)}
    bm = {(cj, g): _bf(xbc_s[rsl[cj], di + g * SSD_STATE:di + (g + 1) * SSD_STATE]) for cj, g in pairs}
    cm = {(cj, g): _bf(xbc_s[rsl[cj], di + (SSD_GROUPS + g) * SSD_STATE:di + (SSD_GROUPS + g + 1) * SSD_STATE])
          for cj, g in pairs}
    cbm = {k: _dot1_nt(cm[k], bm[k]) for k in pairs}
    kx = {(cj, g): _dot1_tn(bm[(cj, g)], xdt_e[rsl[cj], gss[g]]) for cj, g in pairs}
    y_d = {}
    for cj, g in pairs:
        rs = rsl[cj]
        cb_cat = jnp.concatenate([cbm[(cj, g)]] * hpg, axis=1)
        col0 = SM_DT + d * SSD_HEADS + g * hpg
        a_row = jnp.concatenate([acs_t[col0 + hh:col0 + hh + 1, rs] for hh in range(hpg)], axis=1)
        l_cat = jnp.where(incl_cat, jnp.exp(acs_x[rs, gss[g]] - a_row), 0.0)
        x_bd = jnp.where(head_diag, jnp.concatenate([xdt_b[rs, gss[g]]] * hpg, axis=0),
                         jnp.zeros((), BF16))
        y_d[(cj, g)] = _dot1(cb_cat * l_cat, x_bd)
    s = [st_ref[g] for g in range(SSD_GROUPS)]
    for ci in range(nc):
        cj = (nc - 1 - ci) if rev else ci
        rs = rsl[cj]
        y_off = [_dot(cm[(cj, g)], _bf(s[g])) for g in range(SSD_GROUPS)]
        for g in range(SSD_GROUPS):
            out_ref[rs, gss[g]] = y_d[(cj, g)] + y_off[g] * e_acs[rs, gss[g]]
            s[g] = s[g] * e_tot[cj * c:cj * c + 1, gss[g]] + kx[(cj, g)]
    for g in range(SSD_GROUPS):
        st_ref[g] = s[g]

    if final:
        y = out_ref[...] + ofwd_ref[...] + dskip_ref[...] * xbc_s[:, 0:di]
        y = y * _silu(z_ref[...])
        for g in range(SSD_GROUPS):
            out_ref[:, g * gw:(g + 1) * gw] = _rms(y[:, g * gw:(g + 1) * gw], nw_ref[:, g * gw:(g + 1) * gw])
        out_ref[...] = _rms(out_ref[...], gn_ref[...])


def _ssd(u, conv_w, conv_b, a_log, dt_bias, d_skip, norm_w, gn_w, tb, seq):
    t = u.shape[0]
    nb = t // tb
    wx = GROUP_W + 2 * SSD_GROUPS * SSD_STATE
    arow = jnp.zeros((1, 128), F32).at[0, SM_DT:SM_DT + 16].set(a_log.reshape(16))
    brow = jnp.zeros((1, 128), F32).at[0, SM_DT:SM_DT + 16].set(dt_bias.reshape(16))
    dsk = jnp.repeat(d_skip, SSD_HEADDIM).reshape(1, GROUP_W)
    state = pltpu.VMEM((SSD_GROUPS, SSD_STATE, GROUP_W // SSD_GROUPS), F32)
    w = GROUP_W
    o_sds = jax.ShapeDtypeStruct((t, w), F32)

    prev, nxt = _halo_specs(tb, wx, COL_SSD_XBC, nb, False, t)
    xbc_spec, lane_xbc = _main_window(tb, wx, COL_SSD_XBC, nb, False)
    z_spec, lane_z = _main_window(tb, w, COL_SSD_Z, nb, True)
    o_fwd, xbc_act = pl.pallas_call(
        functools.partial(_ssd_kernel, False, False, tb, nb, seq, lane_xbc, lane_z),
        grid=(nb,),
        in_specs=[xbc_spec, prev, nxt,
                  _main_spec(tb, 128, COL_SSD_SMALL, nb, False),
                  _const_spec((4, wx)), _const_spec((1, wx)),
                  _const_spec((1, 128)), _const_spec((1, 128))],
        out_specs=(_main_spec(tb, w, 0, nb, False), _main_spec(tb, wx, 0, nb, False)),
        out_shape=(o_sds, jax.ShapeDtypeStruct((t, wx), F32)),
        scratch_shapes=[pltpu.VMEM((tb + 2 * HALO, wx), F32), state],
        compiler_params=_scan_params(),
        name="ssd_fwd",
    )(u, u, u, u, conv_w, conv_b.reshape(1, wx), arow, brow)
    return pl.pallas_call(
        functools.partial(_ssd_kernel, True, True, tb, nb, seq, lane_xbc, lane_z),
        grid=(nb,),
        in_specs=[_main_spec(tb, wx, 0, nb, True), _main_spec(tb, 128, COL_SSD_SMALL, nb, True),
                  _const_spec((1, 128)), _const_spec((1, 128)),
                  z_spec, _main_spec(tb, w, 0, nb, True),
                  _const_spec((1, w)), _const_spec((1, w)), _const_spec((1, w))],
        out_specs=_main_spec(tb, w, 0, nb, True),
        out_shape=o_sds,
        scratch_shapes=[state],
        compiler_params=_scan_params(),
        name="ssd_bwd",
    )(xbc_act, u, arow, brow, u, o_fwd, dsk, norm_w.reshape(1, w), gn_w.reshape(1, w))


def _hgrn_kernel(rev, final, layer, tb, nb, seq, lanes, *refs):
    if final:
        (q_ref, f_ref, i_ref, lb_ref, mask_ref, g_ref, ofwd_ref, nw_ref, gn_ref, out_ref,
         d_s, z_s, qd_s, kd_s, st_ref) = refs
        g_ref = _LaneWindow(g_ref, lanes[3], GROUP_W)
    else:
        (q_ref, f_ref, i_ref, lb_ref, mask_ref, out_ref,
         d_s, z_s, qd_s, kd_s, st_ref) = refs
    q_ref, f_ref, i_ref = (_LaneWindow(r, lanes[n], GROUP_W) for n, r in enumerate((q_ref, f_ref, i_ref)))
    c = HGRN_STATE_CHUNK
    nl = HGRN_LEVELS
    nc = tb // c
    w = GROUP_W
    hw = HGRN_DK
    first, last = _block_pos(nb, tb, rev, seq)

    lbp = lb_ref[...]
    mx = jnp.max(lbp, axis=0, keepdims=True)
    ex = jnp.exp(lbp - mx)
    den = jnp.sum(ex, axis=0, keepdims=True)
    lb = jnp.zeros((1, w), F32)
    for m in range(1, layer + 1):
        lb = lb + ex[m:m + 1, :] / den

    f = lb + (1.0 - lb) * _sig(f_ref[...])
    logf = jnp.log(f)
    kk = 1.0 - f
    qq = _silu(q_ref[...])
    vv = i_ref[...]

    l1 = _bf(logf)
    l2 = _bf(logf - l1.astype(F32))

    d_s[...] = _dot(mask_ref[...], l1) + _dot(mask_ref[...], l2)

    def range_sum(blk):
        return d_s[blk * tb:(blk + 1) * tb, :]

    rows = lax.broadcasted_iota(jnp.int32, (tb, 1), 0)
    z_s[0] = _bf(qq)
    z_s[1] = _bf(kk)
    for lv in range(1, nl + 1):
        bit = (rows >> (lv - 1)) & 1
        is_query = (bit == 0) if rev else (bit == 1)
        z_s[lv + 1] = _bf(jnp.exp(range_sum(lv - 1)) * jnp.where(is_query, qq, kk))
    bsc = range_sum(nl)
    btot = range_sum(nl + 1)
    qd_s[...] = _bf(qq * jnp.exp(bsc))
    kd_s[...] = _bf(kk * jnp.exp(btot - bsc))
    gl = jnp.exp(btot)
    vb = _bf(vv)

    r64, c64 = _iota2((c, c))
    pair_masks = [r64 == c64]
    for lv in range(1, nl + 1):
        rbit = (r64 >> (lv - 1)) & 1
        cbit = (c64 >> (lv - 1)) & 1
        split = ((rbit == 0) & (cbit == 1)) if rev else ((rbit == 1) & (cbit == 0))
        pair_masks.append(((r64 >> lv) == (c64 >> lv)) & split)

    @pl.when(last if rev else first)
    def _():
        st_ref[...] = jnp.zeros_like(st_ref)

    hs = range(HGRN_HEADS)
    sls = [slice(h * hw, (h + 1) * hw) for h in hs]
    s = [st_ref[h] for h in hs]
    for ci in range(nc):
        cj = (nc - 1 - ci) if rev else ci
        rs = slice(cj * c, (cj + 1) * c)
        prods = [[_dot1_nt(z_s[0, rs, sls[h]], z_s[1, rs, sls[h]])]
                 + [_dot1_nt(z_s[lv + 1, rs, sls[h]], z_s[lv + 1, rs, sls[h]]) for lv in range(1, nl + 1)]
                 for h in hs]
        o_st = [_dot1_nt(qd_s[rs, sls[h]], s[h]) for h in hs]
        kv = [_dot1_tn(vb[rs, sls[h]], kd_s[rs, sls[h]]) for h in hs]
        sc = []
        for h in hs:
            acc = jnp.where(pair_masks[0], prods[h][0], 0.0)
            for lv in range(1, nl + 1):
                acc = acc + jnp.where(pair_masks[lv], prods[h][lv], 0.0)
            sc.append(_bf(acc))
        y_in = [_dot(sc[h], vb[rs, sls[h]]) for h in hs]
        for h in hs:
            out_ref[rs, sls[h]] = y_in[h] + o_st[h]
            s[h] = s[h] * gl[cj * c:cj * c + 1, sls[h]] + kv[h]
    for h in hs:
        st_ref[h] = s[h]

    if final:
        o = out_ref[...] + ofwd_ref[...]
        gg = g_ref[...]
        for h in range(HGRN_HEADS):
            sl = slice(h * hw, (h + 1) * hw)
            out_ref[:, sl] = _rms(o[:, sl], nw_ref[...]) * _silu(gg[:, sl])
        out_ref[...] = _rms(out_ref[...], gn_ref[...])


def _hgrn_masks(tb, rev):
    t = np.arange(tb)[:, None]
    r = np.arange(tb)[None, :]
    blocks = []
    for lv in range(1, HGRN_LEVELS + 1):
        g = 1 << lv
        mid = (t // g) * g + g // 2
        if rev:
            m = np.where(t < mid, (r >= t) & (r < mid), (r >= mid) & (r < t))
        else:
            m = np.where(t >= mid, (r >= mid) & (r <= t), (r > t) & (r < mid))
        blocks.append(m)
    same = (r // HGRN_STATE_CHUNK) == (t // HGRN_STATE_CHUNK)
    blocks.append(same & ((r >= t) if rev else (r <= t)))
    blocks.append(same)
    return jnp.asarray(np.concatenate(blocks, axis=0).astype(np.float32), dtype=BF16)


def _hgrn(u, lb_param, layer, norm_w, gn_w, tb, seq):
    t = u.shape[0]
    nb = t // tb
    w = GROUP_W
    nblk = HGRN_LEVELS + 2
    scratch = [pltpu.VMEM((nblk * tb, w), F32),
               pltpu.VMEM((nblk, tb, w), BF16), pltpu.VMEM((tb, w), BF16),
               pltpu.VMEM((tb, w), BF16),
               pltpu.VMEM((HGRN_HEADS, HGRN_DK, HGRN_DK), F32)]

    g_spec, lane_g = _main_window(tb, w, COL_HG_G, nb, True)

    def call(rev, final, extra_in, extra_specs):
        d = 1 if rev else 0
        q_spec, lane_q = _main_window(tb, w, COL_HG_Q, nb, rev)
        f_spec, lane_f = _main_window(tb, w, COL_HG_F + d * w, nb, rev)
        i_spec, lane_i = _main_window(tb, w, COL_HG_I, nb, rev)
        in_specs = [q_spec, f_spec, i_spec,
                    _const_spec((DEPTH, w)), _const_spec((nblk * tb, tb))] + extra_specs
        return pl.pallas_call(
            functools.partial(_hgrn_kernel, rev, final, layer, tb, nb, seq,
                              (lane_q, lane_f, lane_i, lane_g)),
            grid=(nb,),
            in_specs=in_specs,
            out_specs=_main_spec(tb, w, 0, nb, rev),
            out_shape=jax.ShapeDtypeStruct((t, w), F32),
            scratch_shapes=scratch,
            compiler_params=_scan_params(),
            name="hgrn_bwd" if rev else "hgrn_fwd",
        )(u, u, u, lb_param[d], _hgrn_masks(tb, rev), *extra_in)

    o_fwd = call(False, False, [], [])
    return call(True, True, [u, o_fwd, norm_w.reshape(1, -1), gn_w.reshape(1, w)],
                [g_spec, _main_spec(tb, w, 0, nb, True),
                 _const_spec((1, HGRN_DK)), _const_spec((1, w))])


def _outproj_kernel(starts, *refs):
    nsrc = len(starts)
    x_refs = refs[:nsrc]
    ya_ref, yb_ref, yc_ref, yd_ref, w_ref, o_ref = refs[nsrc:]
    acc = None
    for n, y_ref in enumerate((ya_ref, yb_ref, yc_ref, yd_ref)):
        part = _dot(_bf(y_ref[...]), w_ref[n * GROUP_W:(n + 1) * GROUP_W, :])
        acc = part if acc is None else acc + part

    def add_residual(x_ref):
        o_ref[...] = x_ref[...] + acc

    _select_rows(pl.program_id(0), x_refs, starts, add_residual)


def _outproj(xs, ys, w, layer):
    d = xs[0].shape[1]
    t = sum(x.shape[0] for x in xs)
    tm = 512
    x_specs, starts = _row_source_specs(xs, tm)
    yspec = pl.BlockSpec((tm, GROUP_W), lambda i: (i, 0))
    return pl.pallas_call(
        functools.partial(_outproj_kernel, starts),
        grid=(t // tm,),
        in_specs=x_specs + [yspec, yspec, yspec, yspec,
                            pl.BlockSpec((None,) + w.shape[1:], lambda i: (layer, 0, 0))],
        out_specs=pl.BlockSpec((tm, d), lambda i: (i, 0)),
        out_shape=jax.ShapeDtypeStruct((t, d), F32),
        compiler_params=pltpu.CompilerParams(dimension_semantics=("arbitrary",),
                                             vmem_limit_bytes=VMEM_LIMIT),
        name="outproj",
    )(*xs, *ys, w)


def _ffn_kernel(apply_final, tm, nrb, nf, seq, out_starts, *refs):
    (x_ref, prev_ref, next_ref, ln_ref, wg_ref, wv_ref, cwg_ref, cwv_ref, cbg_ref, cbv_ref,
     wd_ref, fn_ref) = refs[:12]
    o_refs = refs[12:12 + len(out_starts)]
    h_s, g_s, v_s, acc_s = refs[12 + len(out_starts):]
    j = pl.program_id(1)
    i = pl.program_id(0)
    row0 = i * tm
    first = _seq_start(row0, seq)
    last = _seq_start(row0 + tm, seq)

    @pl.when(j == 0)
    def _():
        keep_p = jnp.where(first, 0.0, 1.0)
        keep_n = jnp.where(last, 0.0, 1.0)
        h_s[0:FHALO, :] = _bf(_rms(prev_ref[...], ln_ref[...]) * keep_p)
        h_s[FHALO:FHALO + tm, :] = _bf(_rms(x_ref[...], ln_ref[...]))
        h_s[FHALO + tm:2 * FHALO + tm, :] = _bf(_rms(next_ref[...], ln_ref[...]) * keep_n)
        acc_s[...] = jnp.zeros_like(acc_s)

    hb = h_s[...]
    g_s[...] = _dot(hb, wg_ref[...])
    v_s[...] = _dot(hb, wv_ref[...])

    def conv3(s_ref, cw_ref, cb_ref):
        acc = s_ref[FHALO - 1:FHALO - 1 + tm, :] * cw_ref[0:1, :]
        acc = acc + s_ref[FHALO:FHALO + tm, :] * cw_ref[1:2, :]
        acc = acc + s_ref[FHALO + 1:FHALO + 1 + tm, :] * cw_ref[2:3, :]
        return acc + cb_ref[...]

    act = _silu(conv3(g_s, cwg_ref, cbg_ref)) * conv3(v_s, cwv_ref, cbv_ref)
    acc_s[...] += _dot(_bf(act), wd_ref[...])

    @pl.when(j == nf - 1)
    def _():
        y = x_ref[...] + acc_s[...]
        if apply_final:
            y = _rms(y, fn_ref[...])

        def write(o_ref):
            o_ref[...] = y

        _select_rows(i, o_refs, out_starts, write)


def _ffn(x, ln, w_up, conv_w, conv_b, w_down, layer, final_w, apply_final, seq, out_rows):
    t, d = x.shape
    tm = 512
    tf = 512
    nf = D_FF // tf
    nrb = t // tm
    r = tm // FHALO
    cb = conv_b.reshape(1, 2 * D_FF)
    up_buf = pltpu.VMEM((tm + 2 * FHALO, tf), F32)
    assert sum(out_rows) == t
    out_specs, out_starts = _row_source_specs([jax.ShapeDtypeStruct((n, d), F32) for n in out_rows], tm)
    return pl.pallas_call(
        functools.partial(_ffn_kernel, apply_final, tm, nrb, nf, seq, out_starts),
        grid=(nrb, nf),
        in_specs=[pl.BlockSpec((tm, d), lambda i, j: (i, 0)),
                  pl.BlockSpec((FHALO, d), lambda i, j: (jnp.maximum(i * r - 1, 0), 0)),
                  pl.BlockSpec((FHALO, d), lambda i, j: (jnp.minimum((i + 1) * r, t // FHALO - 1), 0)),
                  pl.BlockSpec((1, d), lambda i, j: (0, 0)),
                  pl.BlockSpec((None, d, tf), lambda i, j: (layer, 0, j)),
                  pl.BlockSpec((None, d, tf), lambda i, j: (layer, 0, j + nf)),
                  pl.BlockSpec((3, tf), lambda i, j: (0, j)),
                  pl.BlockSpec((3, tf), lambda i, j: (0, j + nf)),
                  pl.BlockSpec((1, tf), lambda i, j: (0, j)),
                  pl.BlockSpec((1, tf), lambda i, j: (0, j + nf)),
                  pl.BlockSpec((None, tf, d), lambda i, j: (layer, j, 0)),
                  pl.BlockSpec((1, d), lambda i, j: (0, 0))],
        out_specs=tuple(out_specs),
        out_shape=tuple(jax.ShapeDtypeStruct((n, d), F32) for n in out_rows),
        scratch_shapes=[pltpu.VMEM((tm + 2 * FHALO, d), BF16), up_buf, up_buf,
                        pltpu.VMEM((tm, d), F32)],
        compiler_params=pltpu.CompilerParams(dimension_semantics=("arbitrary", "arbitrary"),
                                             vmem_limit_bytes=VMEM_LIMIT),
        name="ffn",
    )(x, x, x, ln.reshape(1, d), w_up, w_up, conv_w, conv_w, cb, cb, w_down, final_w.reshape(1, d))


def _pad_w_in(w):
    assert w.shape[-1] == D_IN
    split = COL_HG_Q - HG_SHIFT
    zeros = jnp.zeros(w.shape[:-1] + (HG_SHIFT,), w.dtype)
    return jnp.concatenate([w[..., :split], zeros, w[..., split:]], axis=-1)


def kernel(x_prompt, x_sample, ln1, w_in, gdn_conv_w, gdn_a_log, gdn_dt_bias, gdn_norm_w, lru_conv_w, lru_conv_b, lru_wa, lru_ba, lru_wi, lru_bi, lru_lambda, ssd_conv_w, ssd_conv_b, ssd_a_log, ssd_dt_bias, ssd_d, ssd_norm_w, hgrn_lb, hgrn_norm_w, group_norm_w, w_out, ln2, w_up, ffn_conv_w, ffn_conv_b, w_down, final_norm):
    b, s, dm = x_prompt.shape
    db, ds, _ = x_sample.shape
    seq = (b * s, s, ds)
    assert s % TB_SCAN == 0 and ds % TB_SCAN == 0 and s % TB_HGRN == 0 and ds % TB_HGRN == 0
    xs = [x_prompt.reshape(b * s, dm), x_sample.reshape(db * ds, dm)]
    depth = w_in.shape[0]
    w_in_b = _pad_w_in(w_in.astype(BF16))
    w_out_b = w_out.astype(BF16)
    w_up_b = w_up.astype(BF16)
    w_down_b = w_down.astype(BF16)
    for l in range(depth):
        last_layer = l == depth - 1
        u = _inproj(xs, ln1[l], w_in_b, l)
        gn = group_norm_w[l]
        ya = _gdn(u, gdn_conv_w[l], gdn_a_log[l], gdn_dt_bias[l], gdn_norm_w[l], gn[0], TB_SCAN, seq)
        yb = _lru(u, lru_conv_w[l], lru_conv_b[l], lru_wa[l], lru_ba[l], lru_wi[l], lru_bi[l],
                  lru_lambda[l], gn[1], TB_SCAN, seq)
        yc = _ssd(u, ssd_conv_w[l], ssd_conv_b[l], ssd_a_log[l], ssd_dt_bias[l], ssd_d[l],
                  ssd_norm_w[l], gn[2], TB_SCAN, seq)
        yd = _hgrn(u, hgrn_lb, l, hgrn_norm_w[l], gn[3], TB_HGRN, seq)
        x = _outproj(xs, (ya, yb, yc, yd), w_out_b, l)
        out_rows = [b * s, db * ds] if last_layer else [b * s + db * ds]
        xs = list(_ffn(x, ln2[l], w_up_b, ffn_conv_w[l], ffn_conv_b[l], w_down_b, l,
                       final_norm, last_layer, seq, out_rows))
    return (xs[0].reshape(b, s, dm), xs[1].reshape(db, ds, dm))
```

```python
import functools
import math

import jax
import jax.numpy as jnp
import numpy as np
from jax import lax
from jax.experimental import pallas as pl
from jax.experimental.pallas import tpu as pltpu

F32 = jnp.float32
BF16 = jnp.bfloat16

D_MODEL = 2048
DEPTH = 2
GROUP_W = 512
GDN_HEADS = 4
GDN_DK = 128
LRU_BLOCKS = 4
LRU_BW = 128
LRU_C = 8.0
SSD_HEADS = 8
SSD_HEADDIM = 64
SSD_GROUPS = 2
SSD_STATE = 128
HGRN_HEADS = 4
HGRN_DK = 128
D_FF = 5632
CHUNK = 64
HGRN_LEVELS = 6
HGRN_STATE_CHUNK = 1 << HGRN_LEVELS
EPS = 1e-6

COL_GDN_QKV = 0
COL_GDN_Z = 1536
COL_GDN_SMALL = 2048
COL_LRU_X = 2064
COL_LRU_GATE = 2576
COL_SSD_Z = 3088
COL_SSD_XBC = 3600
COL_SSD_SMALL = 4608
HG_SHIFT = 96
COL_HG_Q = 4640 + HG_SHIFT
COL_HG_F = 5152 + HG_SHIFT
COL_HG_I = 6176 + HG_SHIFT
COL_HG_G = 6688 + HG_SHIFT
D_IN = 7200
D_IN_P = D_IN + HG_SHIFT
SM_BETA = 0
SM_ALPHA = 8
SM_DT = 16

SUBLANES = 8
HALO = 8
FHALO = 16
VMEM_LIMIT = 56 * 1024 * 1024
TB_SCAN = 512
TB_HGRN = 256

def _sig(x):
    return 1.0 / (1.0 + jnp.exp(-x))


def _silu(x):
    return x * _sig(x)


def _softplus(x):
    return jnp.maximum(x, 0.0) + jnp.log1p(jnp.exp(-jnp.abs(x)))


def _gelu_tanh(x):
    c = math.sqrt(2.0 / math.pi)
    return 0.5 * x * (1.0 + jnp.tanh(c * (x + 0.044715 * (x * x * x))))


def _bf(x):
    return x.astype(BF16)


def _dot(a, b):
    return jnp.dot(a, b, preferred_element_type=F32)


def _dot1(a, b):
    return _dot(_bf(a), _bf(b))


def _dot1_nt(a, b):
    return lax.dot_general(_bf(a), _bf(b), (((1,), (1,)), ((), ())),
                           preferred_element_type=F32)


def _dot1_tn(a, b):
    return lax.dot_general(_bf(a), _bf(b), (((0,), (0,)), ((), ())),
                           preferred_element_type=F32)


def _dot_mask_l(m, b):
    b1 = _bf(b)
    r1 = b - b1.astype(F32)
    b2 = _bf(r1)
    b3 = _bf(r1 - b2.astype(F32))
    return _dot(m, b1) + (_dot(m, b2) + _dot(m, b3))


def _dot_mask_r(a, m):
    a1 = _bf(a)
    r1 = a - a1.astype(F32)
    a2 = _bf(r1)
    a3 = _bf(r1 - a2.astype(F32))
    return _dot(a1, m) + (_dot(a2, m) + _dot(a3, m))


def _rms(x, w):
    return x * lax.rsqrt(jnp.mean(x * x, axis=-1, keepdims=True) + EPS) * w


def _seq_start(row, seq):
    p_rows, s_len, ds_len = seq
    return jnp.where(row < p_rows, lax.rem(row, s_len) == 0,
                     lax.rem(row - p_rows, ds_len) == 0)


def _block_pos(nb, tb, rev, seq):
    i = pl.program_id(0)
    j = (nb - 1 - i) if rev else i
    row0 = j * tb
    first = _seq_start(row0, seq)
    last = _seq_start(row0 + tb, seq)
    return first, last


def _fill_pad(xp_ref, blk_ref, prev_ref, next_ref, first, last, tb):
    keep_p = jnp.where(first, 0.0, 1.0)
    keep_n = jnp.where(last, 0.0, 1.0)
    xp_ref[0:HALO, :] = prev_ref[...] * keep_p
    xp_ref[HALO:HALO + tb, :] = blk_ref[...]
    xp_ref[HALO + tb:2 * HALO + tb, :] = next_ref[...] * keep_n


def _conv4(xp_ref, w_ref, tb):
    acc = xp_ref[HALO - 2:HALO - 2 + tb, :] * w_ref[0:1, :]
    acc = acc + xp_ref[HALO - 1:HALO - 1 + tb, :] * w_ref[1:2, :]
    acc = acc + xp_ref[HALO:HALO + tb, :] * w_ref[2:3, :]
    acc = acc + xp_ref[HALO + 1:HALO + 1 + tb, :] * w_ref[3:4, :]
    return acc


def _iota2(shape):
    return (lax.broadcasted_iota(jnp.int32, shape, 0),
            lax.broadcasted_iota(jnp.int32, shape, 1))


def _bdot(a, b):
    return lax.dot_general(a, b, (((2,), (1,)), ((0,), (0,))), preferred_element_type=F32)


def _bdot1_nt(a, b):
    return lax.dot_general(_bf(a), _bf(b), (((2,), (2,)), ((0,), (0,))),
                           preferred_element_type=F32)


def _bdot1(a, b):
    return _bdot(_bf(a), _bf(b))


def _unit_tri_inverse(a, r, c, n):
    eye = jnp.where(r == c, 1.0, 0.0)[None]
    d = jnp.where(((r >> 3) == (c >> 3))[None], a, 0.0)
    d2 = _bdot1(d, d)
    x = eye - d
    x = x + _bdot1(x, d2)
    d4 = _bdot1(d2, d2)
    x = x + _bdot1(x, d4)
    b = 16
    sh = 4
    while b <= n:
        joined = ((r >> sh) == (c >> sh)) & ((r >> (sh - 1)) != (c >> (sh - 1)))
        e = jnp.where(joined[None], a, 0.0)
        x = x - _bdot1(_bdot1(x, e), x)
        b *= 2
        sh += 1
    return x


def _row_source_specs(xs, tm):
    d = xs[0].shape[1]
    starts = []
    specs = []
    off = 0
    for x in xs:
        assert x.shape[0] % tm == 0
        nblk = x.shape[0] // tm

        def imap(i, *_, off=off, nblk=nblk):
            return (jnp.clip(i - off, 0, nblk - 1), 0)

        specs.append(pl.BlockSpec((tm, d), imap))
        starts.append(off)
        off += nblk
    return specs, starts


def _select_rows(i, refs, starts, fn):
    for n, ref in enumerate(refs):
        lo = starts[n]
        hi = starts[n + 1] if n + 1 < len(starts) else None
        cond = i >= lo if hi is None else ((i >= lo) & (i < hi))
        if len(refs) == 1:
            fn(ref)
        else:
            pl.when(cond)(functools.partial(fn, ref))


def _inproj_kernel(starts, *refs):
    nsrc = len(starts)
    x_refs = refs[:nsrc]
    ln_ref, w_ref, o_ref, h_s = refs[nsrc:]

    def norm_rows(x_ref):
        h_s[...] = _bf(_rms(x_ref[...], ln_ref[...]))

    @pl.when(pl.program_id(1) == 0)
    def _():
        _select_rows(pl.program_id(0), x_refs, starts, norm_rows)

    o_ref[...] = _dot(h_s[...], w_ref[...])


def _inproj(xs, ln, w, layer):
    d = xs[0].shape[1]
    t = sum(x.shape[0] for x in xs)
    n = w.shape[2]
    tm = 512
    tn = n // 3
    x_specs, starts = _row_source_specs(xs, tm)
    return pl.pallas_call(
        functools.partial(_inproj_kernel, starts),
        grid=(t // tm, n // tn),
        in_specs=x_specs + [pl.BlockSpec((1, d), lambda i, j: (0, 0)),
                            pl.BlockSpec((None, d, tn), lambda i, j: (layer, 0, j))],
        out_specs=pl.BlockSpec((tm, tn), lambda i, j: (i, j)),
        out_shape=jax.ShapeDtypeStruct((t, n), F32),
        scratch_shapes=[pltpu.VMEM((tm, d), BF16)],
        compiler_params=pltpu.CompilerParams(
            dimension_semantics=("arbitrary", "arbitrary"),
            vmem_limit_bytes=VMEM_LIMIT),
        name="inproj",
    )(*xs, ln.reshape(1, d), w)


def _bidx(nb, rev):
    if rev:
        return lambda i: nb - 1 - i
    return lambda i: i


LANES = 128


def _window_spec(rows, w, col, row_block):
    if col % w == 0:
        return pl.BlockSpec((rows, w), lambda i: (row_block(i), col // w)), 0
    if col % LANES == 0:
        return pl.BlockSpec((pl.Element(rows), pl.Element(w)), lambda i: (row_block(i) * rows, col)), 0
    wide = w + LANES
    start = min(col - col % LANES, D_IN_P - wide)
    spec = pl.BlockSpec((pl.Element(rows), pl.Element(wide)), lambda i: (row_block(i) * rows, start))
    return spec, col - start


class _LaneWindow:
    def __init__(self, ref, lane0, w):
        self.ref, self.lane0, self.w = ref, lane0, w

    def __getitem__(self, idx):
        assert idx is Ellipsis
        return self.ref[:, self.lane0:self.lane0 + self.w]


def _main_spec(tb, w, col, nb, rev):
    spec, lane0 = _window_spec(tb, w, col, _bidx(nb, rev))
    assert lane0 == 0
    return spec


def _main_window(tb, w, col, nb, rev):
    return _window_spec(tb, w, col, _bidx(nb, rev))


def _halo_specs(tb, w, col, nb, rev, t):
    f = _bidx(nb, rev)
    r = tb // HALO
    prev, _ = _window_spec(HALO, w, col, lambda i: jnp.maximum(f(i) * r - 1, 0))
    nxt, _ = _window_spec(HALO, w, col, lambda i: jnp.minimum((f(i) + 1) * r, t // HALO - 1))
    return prev, nxt


def _const_spec(shape):
    nd = len(shape)
    return pl.BlockSpec(shape, lambda i: (0,) * nd)


def _scan_params():
    return pltpu.CompilerParams(dimension_semantics=("arbitrary",),
                                vmem_limit_bytes=VMEM_LIMIT)


def _gdn_kernel(rev, final, tb, nb, seq, *refs):
    if final:
        (qkv_in_ref, small_ref, arow_ref, brow_ref,
         z_ref, ofwd_ref, nw_ref, gn_ref, out_ref,
         q3, k3, v3, bet3, gi3, gt3, gj3, w3, u3, qk3, st_ref) = refs
    else:
        (qkv_ref, prev_ref, next_ref, small_ref, cw_ref, arow_ref, brow_ref, out_ref, qkv_out_ref,
         xp_ref, q3, k3, v3, bet3, gi3, gt3, gj3, w3, u3, qk3, st_ref) = refs
    d = 1 if rev else 0
    c = CHUNK
    c2 = 2 * CHUNK
    npair = tb // c2
    nh = GDN_HEADS
    hw = GDN_DK
    kw = GDN_HEADS * GDN_DK
    first, last = _block_pos(nb, tb, rev, seq)

    if not final:
        _fill_pad(xp_ref, qkv_ref, prev_ref, next_ref, first, last, tb)
        y = _silu(_conv4(xp_ref, cw_ref, tb))
    for h in range(nh):
        if final:
            qh = qkv_in_ref[:, h * hw:(h + 1) * hw]
            kh = qkv_in_ref[:, kw + h * hw:kw + (h + 1) * hw]
            vh = qkv_in_ref[:, 2 * kw + h * hw:2 * kw + (h + 1) * hw]
        else:
            qh = y[:, h * hw:(h + 1) * hw]
            qh = qh * lax.rsqrt(jnp.sum(qh * qh, axis=-1, keepdims=True) + EPS) * (GDN_DK ** -0.5)
            kh = y[:, kw + h * hw:kw + (h + 1) * hw]
            kh = kh * lax.rsqrt(jnp.sum(kh * kh, axis=-1, keepdims=True) + EPS)
            vh = y[:, 2 * kw + h * hw:2 * kw + (h + 1) * hw]
            qkv_out_ref[:, h * hw:(h + 1) * hw] = qh
            qkv_out_ref[:, kw + h * hw:kw + (h + 1) * hw] = kh
            qkv_out_ref[:, 2 * kw + h * hw:2 * kw + (h + 1) * hw] = vh
        for pi in range(npair):
            q3[pi * nh + h] = qh[pi * c2:(pi + 1) * c2, :]
            k3[pi * nh + h] = kh[pi * c2:(pi + 1) * c2, :]
            v3[pi * nh + h] = vh[pi * c2:(pi + 1) * c2, :]

    sm = small_ref[...]
    beta_all = _sig(sm)
    g_all = -jnp.exp(arow_ref[...]) * _softplus(sm + brow_ref[...])
    rb, cb_ = _iota2((tb, tb))
    same = (rb >> 6) == (cb_ >> 6)
    incl_b = same & ((cb_ >= rb) if rev else (cb_ <= rb))
    gcum = _dot_mask_l(jnp.where(incl_b, 1.0, 0.0).astype(BF16), g_all)
    gtot = _dot_mask_l(jnp.where(same, 1.0, 0.0).astype(BF16), g_all)
    gcum_t = gcum.T
    for pi in range(npair):
        rows = slice(pi * c2, (pi + 1) * c2)
        for h in range(nh):
            cb = SM_BETA + d * nh + h
            cg = SM_ALPHA + d * nh + h
            b = pi * nh + h
            bet3[b] = jnp.broadcast_to(beta_all[rows, cb:cb + 1], (c2, hw))
            gi3[b] = jnp.broadcast_to(gcum[rows, cg:cg + 1], (c2, hw))
            gt3[b] = jnp.broadcast_to(gtot[rows, cg:cg + 1], (c2, hw))
            gj3[b] = jnp.broadcast_to(gcum_t[cg:cg + 1, rows], (c2, c2))

    @pl.when(last if rev else first)
    def _():
        st_ref[...] = jnp.zeros_like(st_ref)

    r, cc = _iota2((c2, c2))
    same_chunk = (r >> 6) == (cc >> 6)
    incl = same_chunk & ((cc >= r) if rev else (cc <= r))
    strict = same_chunk & ((cc > r) if rev else (cc < r))
    q = q3[...]
    k = k3[...]
    bet = bet3[...]
    gi = gi3[...]
    decay = jnp.where(incl[None], jnp.exp(gi - gj3[...]), 0.0)
    kb = k * bet
    a = jnp.where(strict[None], _bdot1_nt(kb, k) * decay, 0.0)
    tm = _unit_tri_inverse(a, r, cc, c)
    eg = jnp.exp(gi)
    w3[...] = _bdot1(tm, kb * eg)
    u3[...] = _bdot1(tm, v3[...] * bet)
    qk3[...] = _bdot1_nt(q, k) * decay
    q3[...] = q * eg
    k3[...] = k * jnp.exp(gt3[...] - gi)

    def chunk_pair(pi, carry):
        pj = (npair - 1 - pi) if rev else pi
        hs = range(nh)
        for half in ((1, 0) if rev else (0, 1)):
            r0 = pl.multiple_of(pj * c2 + half * c, c)
            hr = slice(half * c, (half + 1) * c)
            s = [st_ref[h] for h in hs]
            sb = [_bf(s[h]) for h in hs]
            ws = [_dot(_bf(w3[pj * nh + h, hr, :]), sb[h]) for h in hs]
            qs = [_dot(_bf(q3[pj * nh + h, hr, :]), sb[h]) for h in hs]
            v_new = [_bf(u3[pj * nh + h, hr, :] - ws[h]) for h in hs]
            kv = [_dot1_tn(k3[pj * nh + h, hr, :], v_new[h]) for h in hs]
            ov = [_dot(_bf(qk3[pj * nh + h, hr, hr]), v_new[h]) for h in hs]
            for h in hs:
                st_ref[h] = s[h] * jnp.exp(gt3[pj * nh + h, half * c:half * c + 1, :]) + kv[h]
                out_ref[pl.ds(r0, c), h * hw:(h + 1) * hw] = qs[h] + ov[h]
        return carry

    lax.fori_loop(0, npair, chunk_pair, 0)

    if final:
        o = out_ref[...] + ofwd_ref[...]
        z = z_ref[...]
        for h in range(GDN_HEADS):
            oh = _rms(o[:, h * hw:(h + 1) * hw], nw_ref[...])
            out_ref[:, h * hw:(h + 1) * hw] = oh * _silu(z[:, h * hw:(h + 1) * hw])
        out_ref[...] = _rms(out_ref[...], gn_ref[...])


def _gdn(u, conv_w, a_log, dt_bias, norm_w, gn_w, tb, seq):
    t = u.shape[0]
    nb = t // tb
    w3 = 3 * GROUP_W
    arow = jnp.zeros((1, 128), F32).at[0, SM_ALPHA:SM_ALPHA + 8].set(a_log.reshape(8))
    brow = jnp.zeros((1, 128), F32).at[0, SM_ALPHA:SM_ALPHA + 8].set(dt_bias.reshape(8))
    assert GDN_DK == 2 * CHUNK
    nbat = (tb // (2 * CHUNK)) * GDN_HEADS
    entry = pltpu.VMEM((nbat, 2 * CHUNK, GDN_DK), F32)
    batch_scratch = [entry] * 10 + [pltpu.VMEM((GDN_HEADS, GDN_DK, GDN_DK), F32)]
    o_sds = jax.ShapeDtypeStruct((t, GROUP_W), F32)

    prev, nxt = _halo_specs(tb, w3, COL_GDN_QKV, nb, False, t)
    o_fwd, qkv_act = pl.pallas_call(
        functools.partial(_gdn_kernel, False, False, tb, nb, seq),
        grid=(nb,),
        in_specs=[_main_spec(tb, w3, COL_GDN_QKV, nb, False), prev, nxt,
                  _main_spec(tb, 128, COL_GDN_SMALL, nb, False),
                  _const_spec((4, w3)), _const_spec((1, 128)), _const_spec((1, 128))],
        out_specs=(_main_spec(tb, GROUP_W, 0, nb, False), _main_spec(tb, w3, 0, nb, False)),
        out_shape=(o_sds, jax.ShapeDtypeStruct((t, w3), F32)),
        scratch_shapes=[pltpu.VMEM((tb + 2 * HALO, w3), F32)] + batch_scratch,
        compiler_params=_scan_params(),
        name="gdn_fwd",
    )(u, u, u, u, conv_w, arow, brow)
    return pl.pallas_call(
        functools.partial(_gdn_kernel, True, True, tb, nb, seq),
        grid=(nb,),
        in_specs=[_main_spec(tb, w3, 0, nb, True), _main_spec(tb, 128, COL_GDN_SMALL, nb, True),
                  _const_spec((1, 128)), _const_spec((1, 128)),
                  _main_spec(tb, GROUP_W, COL_GDN_Z, nb, True), _main_spec(tb, GROUP_W, 0, nb, True),
                  _const_spec((1, GDN_DK)), _const_spec((1, GROUP_W))],
        out_specs=_main_spec(tb, GROUP_W, 0, nb, True),
        out_shape=o_sds,
        scratch_shapes=batch_scratch,
        compiler_params=_scan_params(),
        name="gdn_bwd",
    )(qkv_act, u, arow, brow, u, o_fwd, norm_w.reshape(1, -1), gn_w.reshape(1, -1))


def _lru_kernel(rev, final, tb, nb, seq, lane_x, lane_gate, *refs):
    if final:
        (x_ref, prev_ref, next_ref, cw_ref, cb_ref, w_ref, ba_ref, bi_ref, lam_ref,
         gate_ref, ofwd_ref, gn_ref, out_ref, xp_ref, h_ref) = refs
        gate_ref = _LaneWindow(gate_ref, lane_gate, GROUP_W)
    else:
        (x_ref, prev_ref, next_ref, cw_ref, cb_ref, w_ref, ba_ref, bi_ref, lam_ref,
         out_ref, xp_ref, h_ref) = refs
    x_ref, prev_ref, next_ref = (_LaneWindow(r, lane_x, GROUP_W) for r in (x_ref, prev_ref, next_ref))
    first, last = _block_pos(nb, tb, rev, seq)
    start = last if rev else first
    bw = LRU_BW

    _fill_pad(xp_ref, x_ref, prev_ref, next_ref, first, last, tb)
    xc = _conv4(xp_ref, cw_ref, tb) + cb_ref[...]

    @pl.when(start)
    def _():
        h_ref[...] = jnp.zeros_like(h_ref)

    rows = lax.broadcasted_iota(jnp.int32, (tb, bw), 0)
    tile_rows = lax.broadcasted_iota(jnp.int32, (tb // SUBLANES, SUBLANES, bw), 1)
    start_row = (tb - 1) if rev else 0
    is_start_row = (rows == start_row) & start
    sp = _softplus(-lam_ref[...])
    for n in range(LRU_BLOCKS):
        xb = xc[:, n * bw:(n + 1) * bw]
        pre = _dot1(xb, w_ref[n])
        rg = _sig(pre[:, :bw] + ba_ref[:, n * bw:(n + 1) * bw])
        ig = _sig(pre[:, bw:] + bi_ref[:, n * bw:(n + 1) * bw])
        log_a = (-LRU_C) * rg * sp[:, n * bw:(n + 1) * bw]
        a = jnp.exp(log_a)
        mult = jnp.sqrt((1.0 + a * a) * jnp.tanh(-log_a))
        mult = jnp.where(is_start_row, 1.0, mult)
        b = mult * (ig * xb)
        ntiles = tb // SUBLANES
        a = a.reshape(ntiles, SUBLANES, bw)
        b = b.reshape(ntiles, SUBLANES, bw)
        s = 1
        while s < SUBLANES:
            sh = (SUBLANES - s) if rev else s
            a_sh = pltpu.roll(a, sh, 1)
            b_sh = pltpu.roll(b, sh, 1)
            ok = (tile_rows < SUBLANES - s) if rev else (tile_rows >= s)
            b = jnp.where(ok, a * b_sh, 0.0) + b
            a = jnp.where(ok, a * a_sh, a)
            s *= 2
        h_prev = h_ref[:, n * bw:(n + 1) * bw]
        for ti in (range(ntiles - 1, -1, -1) if rev else range(ntiles)):
            ht = a[ti] * h_prev + b[ti]
            out_ref[ti * SUBLANES:(ti + 1) * SUBLANES, n * bw:(n + 1) * bw] = ht
            h_prev = ht[0:1, :] if rev else ht[SUBLANES - 1:SUBLANES, :]
        h_ref[:, n * bw:(n + 1) * bw] = h_prev

    if final:
        o = (out_ref[...] + ofwd_ref[...]) * _gelu_tanh(gate_ref[...])
        out_ref[...] = _rms(o, gn_ref[...])


def _lru(u, conv_w, conv_b, wa, ba, wi, bi, lam, gn_w, tb, seq):
    t = u.shape[0]
    nb = t // tb
    w = GROUP_W
    wcat = jnp.concatenate([wa, wi], axis=-1).astype(BF16)
    scratch = [pltpu.VMEM((tb + 2 * HALO, w), F32), pltpu.VMEM((1, w), F32)]

    gate_spec, lane_gate = _main_window(tb, w, COL_LRU_GATE, nb, True)

    def call(rev, final, extra_in, extra_specs):
        d = 1 if rev else 0
        prev, nxt = _halo_specs(tb, w, COL_LRU_X, nb, rev, t)
        x_spec, lane_x = _main_window(tb, w, COL_LRU_X, nb, rev)
        in_specs = [x_spec, prev, nxt,
                    _const_spec((4, w)), _const_spec((1, w)),
                    _const_spec((LRU_BLOCKS, LRU_BW, 2 * LRU_BW)),
                    _const_spec((1, w)), _const_spec((1, w)), _const_spec((1, w))] + extra_specs
        return pl.pallas_call(
            functools.partial(_lru_kernel, rev, final, tb, nb, seq, lane_x, lane_gate),
            grid=(nb,),
            in_specs=in_specs,
            out_specs=_main_spec(tb, w, 0, nb, rev),
            out_shape=jax.ShapeDtypeStruct((t, w), F32),
            scratch_shapes=scratch,
            compiler_params=_scan_params(),
            name="lru_bwd" if rev else "lru_fwd",
        )(u, u, u, conv_w, conv_b.reshape(1, w), wcat[d], ba[d].reshape(1, w),
          bi[d].reshape(1, w), lam[d].reshape(1, w), *extra_in)

    o_fwd = call(False, False, [], [])
    return call(True, True, [u, o_fwd, gn_w.reshape(1, w)],
                [gate_spec, _main_spec(tb, w, 0, nb, True),
                 _const_spec((1, w))])


def _ssd_kernel(rev, final, tb, nb, seq, lane_xbc, lane_z, *refs):
    if final:
        (xbc_s, small_ref, arow_ref, brow_ref,
         z_ref, ofwd_ref, dskip_ref, nw_ref, gn_ref, out_ref, st_ref) = refs
        z_ref = _LaneWindow(z_ref, lane_z, GROUP_W)
    else:
        (xbc_ref, prev_ref, next_ref, small_ref, cw_ref, cb_ref, arow_ref, brow_ref,
         out_ref, xbc_s, xp_ref, st_ref) = refs
        wx = GROUP_W + 2 * SSD_GROUPS * SSD_STATE
        xbc_ref, prev_ref, next_ref = (_LaneWindow(r, lane_xbc, wx) for r in (xbc_ref, prev_ref, next_ref))
    d = 1 if rev else 0
    c = CHUNK
    nc = tb // c
    di = GROUP_W
    p = SSD_HEADDIM
    gw = di // SSD_GROUPS
    first, last = _block_pos(nb, tb, rev, seq)

    assert p == c
    hpg = SSD_HEADS // SSD_GROUPS

    if not final:
        _fill_pad(xp_ref, xbc_ref, prev_ref, next_ref, first, last, tb)
        xbc_s[...] = _silu(_conv4(xp_ref, cw_ref, tb) + cb_ref[...])
    dt = _softplus(small_ref[...] + brow_ref[...])
    da = dt * (-jnp.exp(arow_ref[...]))

    rb, cb_ = _iota2((tb, tb))
    same = (rb >> 6) == (cb_ >> 6)
    incl_b = same & ((cb_ >= rb) if rev else (cb_ <= rb))
    acs = _dot_mask_l(jnp.where(incl_b, 1.0, 0.0).astype(BF16), da)
    tot = _dot_mask_l(jnp.where(same, 1.0, 0.0).astype(BF16), da)
    acs_t = acs.T
    sr, sc = _iota2((128, di))
    sel = jnp.where(sr == SM_DT + d * SSD_HEADS + (sc >> 6), 1.0, 0.0).astype(BF16)
    acs_x = _dot_mask_r(acs, sel)
    tot_x = _dot_mask_r(tot, sel)
    dt_x = _dot_mask_r(dt, sel)
    xdt = xbc_s[:, 0:di] * dt_x
    xdt_e = xdt * jnp.exp(tot_x - acs_x)
    e_acs = jnp.exp(acs_x)
    e_tot = jnp.exp(tot_x)
    xdt_b = _bf(xdt)

    @pl.when(last if rev else first)
    def _():
        st_ref[...] = jnp.zeros_like(st_ref)

    rr, lj = _iota2((c, gw))
    incl_cat = ((lj & (c - 1)) >= rr) if rev else ((lj & (c - 1)) <= rr)
    rbd, cbd = _iota2((gw, gw))
    head_diag = (rbd >> 6) == (cbd >> 6)

    gss = [slice(g * gw, (g + 1) * gw) for g in range(SSD_GROUPS)]
    pairs = [(cj, g) for cj in range(nc) for g in range(SSD_GROUPS)]
    rsl = {cj: slice(cj * c, (cj + 1) * c) for cj in range(nc)}
    bm = {(cj, g): _bf(xbc_s[rsl[cj], di + g * SSD_STATE:di + (g + 1) * SSD_STATE]) for cj, g in pairs}
    cm = {(cj, g): _bf(xbc_s[rsl[cj], di + (SSD_GROUPS + g) * SSD_STATE:di + (SSD_GROUPS + g + 1) * SSD_STATE])
          for cj, g in pairs}
    cbm = {k: _dot1_nt(cm[k], bm[k]) for k in pairs}
    kx = {(cj, g): _dot1_tn(bm[(cj, g)], xdt_e[rsl[cj], gss[g]]) for cj, g in pairs}
    y_d = {}
    for cj, g in pairs:
        rs = rsl[cj]
        cb_cat = jnp.concatenate([cbm[(cj, g)]] * hpg, axis=1)
        col0 = SM_DT + d * SSD_HEADS + g * hpg
        a_row = jnp.concatenate([acs_t[col0 + hh:col0 + hh + 1, rs] for hh in range(hpg)], axis=1)
        l_cat = jnp.where(incl_cat, jnp.exp(acs_x[rs, gss[g]] - a_row), 0.0)
        x_bd = jnp.where(head_diag, jnp.concatenate([xdt_b[rs, gss[g]]] * hpg, axis=0),
                         jnp.zeros((), BF16))
        y_d[(cj, g)] = _dot1(cb_cat * l_cat, x_bd)
    s = [st_ref[g] for g in range(SSD_GROUPS)]
    for ci in range(nc):
        cj = (nc - 1 - ci) if rev else ci
        rs = rsl[cj]
        y_off = [_dot(cm[(cj, g)], _bf(s[g])) for g in range(SSD_GROUPS)]
        for g in range(SSD_GROUPS):
            out_ref[rs, gss[g]] = y_d[(cj, g)] + y_off[g] * e_acs[rs, gss[g]]
            s[g] = s[g] * e_tot[cj * c:cj * c + 1, gss[g]] + kx[(cj, g)]
    for g in range(SSD_GROUPS):
        st_ref[g] = s[g]

    if final:
        y = out_ref[...] + ofwd_ref[...] + dskip_ref[...] * xbc_s[:, 0:di]
        y = y * _silu(z_ref[...])
        for g in range(SSD_GROUPS):
            out_ref[:, g * gw:(g + 1) * gw] = _rms(y[:, g * gw:(g + 1) * gw], nw_ref[:, g * gw:(g + 1) * gw])
        out_ref[...] = _rms(out_ref[...], gn_ref[...])


def _ssd(u, conv_w, conv_b, a_log, dt_bias, d_skip, norm_w, gn_w, tb, seq):
    t = u.shape[0]
    nb = t // tb
    wx = GROUP_W + 2 * SSD_GROUPS * SSD_STATE
    arow = jnp.zeros((1, 128), F32).at[0, SM_DT:SM_DT + 16].set(a_log.reshape(16))
    brow = jnp.zeros((1, 128), F32).at[0, SM_DT:SM_DT + 16].set(dt_bias.reshape(16))
    dsk = jnp.repeat(d_skip, SSD_HEADDIM).reshape(1, GROUP_W)
    state = pltpu.VMEM((SSD_GROUPS, SSD_STATE, GROUP_W // SSD_GROUPS), F32)
    w = GROUP_W
    o_sds = jax.ShapeDtypeStruct((t, w), F32)

    prev, nxt = _halo_specs(tb, wx, COL_SSD_XBC, nb, False, t)
    xbc_spec, lane_xbc = _main_window(tb, wx, COL_SSD_XBC, nb, False)
    z_spec, lane_z = _main_window(tb, w, COL_SSD_Z, nb, True)
    o_fwd, xbc_act = pl.pallas_call(
        functools.partial(_ssd_kernel, False, False, tb, nb, seq, lane_xbc, lane_z),
        grid=(nb,),
        in_specs=[xbc_spec, prev, nxt,
                  _main_spec(tb, 128, COL_SSD_SMALL, nb, False),
                  _const_spec((4, wx)), _const_spec((1, wx)),
                  _const_spec((1, 128)), _const_spec((1, 128))],
        out_specs=(_main_spec(tb, w, 0, nb, False), _main_spec(tb, wx, 0, nb, False)),
        out_shape=(o_sds, jax.ShapeDtypeStruct((t, wx), F32)),
        scratch_shapes=[pltpu.VMEM((tb + 2 * HALO, wx), F32), state],
        compiler_params=_scan_params(),
        name="ssd_fwd",
    )(u, u, u, u, conv_w, conv_b.reshape(1, wx), arow, brow)
    return pl.pallas_call(
        functools.partial(_ssd_kernel, True, True, tb, nb, seq, lane_xbc, lane_z),
        grid=(nb,),
        in_specs=[_main_spec(tb, wx, 0, nb, True), _main_spec(tb, 128, COL_SSD_SMALL, nb, True),
                  _const_spec((1, 128)), _const_spec((1, 128)),
                  z_spec, _main_spec(tb, w, 0, nb, True),
                  _const_spec((1, w)), _const_spec((1, w)), _const_spec((1, w))],
        out_specs=_main_spec(tb, w, 0, nb, True),
        out_shape=o_sds,
        scratch_shapes=[state],
        compiler_params=_scan_params(),
        name="ssd_bwd",
    )(xbc_act, u, arow, brow, u, o_fwd, dsk, norm_w.reshape(1, w), gn_w.reshape(1, w))


def _hgrn_kernel(rev, final, layer, tb, nb, seq, lanes, *refs):
    if final:
        (q_ref, f_ref, i_ref, lb_ref, mask_ref, g_ref, ofwd_ref, nw_ref, gn_ref, out_ref,
         d_s, z_s, qd_s, kd_s, st_ref) = refs
        g_ref = _LaneWindow(g_ref, lanes[3], GROUP_W)
    else:
        (q_ref, f_ref, i_ref, lb_ref, mask_ref, out_ref,
         d_s, z_s, qd_s, kd_s, st_ref) = refs
    q_ref, f_ref, i_ref = (_LaneWindow(r, lanes[n], GROUP_W) for n, r in enumerate((q_ref, f_ref, i_ref)))
    c = HGRN_STATE_CHUNK
    nl = HGRN_LEVELS
    nc = tb // c
    w = GROUP_W
    hw = HGRN_DK
    first, last = _block_pos(nb, tb, rev, seq)

    lbp = lb_ref[...]
    mx = jnp.max(lbp, axis=0, keepdims=True)
    ex = jnp.exp(lbp - mx)
    den = jnp.sum(ex, axis=0, keepdims=True)
    lb = jnp.zeros((1, w), F32)
    for m in range(1, layer + 1):
        lb = lb + ex[m:m + 1, :] / den

    f = lb + (1.0 - lb) * _sig(f_ref[...])
    logf = jnp.log(f)
    kk = 1.0 - f
    qq = _silu(q_ref[...])
    vv = i_ref[...]

    l1 = _bf(logf)
    l2 = _bf(logf - l1.astype(F32))

    d_s[...] = _dot(mask_ref[...], l1) + _dot(mask_ref[...], l2)

    def range_sum(blk):
        return d_s[blk * tb:(blk + 1) * tb, :]

    rows = lax.broadcasted_iota(jnp.int32, (tb, 1), 0)
    z_s[0] = _bf(qq)
    z_s[1] = _bf(kk)
    for lv in range(1, nl + 1):
        bit = (rows >> (lv - 1)) & 1
        is_query = (bit == 0) if rev else (bit == 1)
        z_s[lv + 1] = _bf(jnp.exp(range_sum(lv - 1)) * jnp.where(is_query, qq, kk))
    bsc = range_sum(nl)
    btot = range_sum(nl + 1)
    qd_s[...] = _bf(qq * jnp.exp(bsc))
    kd_s[...] = _bf(kk * jnp.exp(btot - bsc))
    gl = jnp.exp(btot)
    vb = _bf(vv)

    r64, c64 = _iota2((c, c))
    pair_masks = [r64 == c64]
    for lv in range(1, nl + 1):
        rbit = (r64 >> (lv - 1)) & 1
        cbit = (c64 >> (lv - 1)) & 1
        split = ((rbit == 0) & (cbit == 1)) if rev else ((rbit == 1) & (cbit == 0))
        pair_masks.append(((r64 >> lv) == (c64 >> lv)) & split)

    @pl.when(last if rev else first)
    def _():
        st_ref[...] = jnp.zeros_like(st_ref)

    hs = range(HGRN_HEADS)
    sls = [slice(h * hw, (h + 1) * hw) for h in hs]
    s = [st_ref[h] for h in hs]
    for ci in range(nc):
        cj = (nc - 1 - ci) if rev else ci
        rs = slice(cj * c, (cj + 1) * c)
        prods = [[_dot1_nt(z_s[0, rs, sls[h]], z_s[1, rs, sls[h]])]
                 + [_dot1_nt(z_s[lv + 1, rs, sls[h]], z_s[lv + 1, rs, sls[h]]) for lv in range(1, nl + 1)]
                 for h in hs]
        o_st = [_dot1_nt(qd_s[rs, sls[h]], s[h]) for h in hs]
        kv = [_dot1_tn(vb[rs, sls[h]], kd_s[rs, sls[h]]) for h in hs]
        sc = []
        for h in hs:
            acc = jnp.where(pair_masks[0], prods[h][0], 0.0)
            for lv in range(1, nl + 1):
                acc = acc + jnp.where(pair_masks[lv], prods[h][lv], 0.0)
            sc.append(_bf(acc))
        y_in = [_dot(sc[h], vb[rs, sls[h]]) for h in hs]
        for h in hs:
            out_ref[rs, sls[h]] = y_in[h] + o_st[h]
            s[h] = s[h] * gl[cj * c:cj * c + 1, sls[h]] + kv[h]
    for h in hs:
        st_ref[h] = s[h]

    if final:
        o = out_ref[...] + ofwd_ref[...]
        gg = g_ref[...]
        for h in range(HGRN_HEADS):
            sl = slice(h * hw, (h + 1) * hw)
            out_ref[:, sl] = _rms(o[:, sl], nw_ref[...]) * _silu(gg[:, sl])
        out_ref[...] = _rms(out_ref[...], gn_ref[...])


def _hgrn_masks(tb, rev):
    t = np.arange(tb)[:, None]
    r = np.arange(tb)[None, :]
    blocks = []
    for lv in range(1, HGRN_LEVELS + 1):
        g = 1 << lv
        mid = (t // g) * g + g // 2
        if rev:
            m = np.where(t < mid, (r >= t) & (r < mid), (r >= mid) & (r < t))
        else:
            m = np.where(t >= mid, (r >= mid) & (r <= t), (r > t) & (r < mid))
        blocks.append(m)
    same = (r // HGRN_STATE_CHUNK) == (t // HGRN_STATE_CHUNK)
    blocks.append(same & ((r >= t) if rev else (r <= t)))
    blocks.append(same)
    return jnp.asarray(np.concatenate(blocks, axis=0).astype(np.float32), dtype=BF16)


def _hgrn(u, lb_param, layer, norm_w, gn_w, tb, seq):
    t = u.shape[0]
    nb = t // tb
    w = GROUP_W
    nblk = HGRN_LEVELS + 2
    scratch = [pltpu.VMEM((nblk * tb, w), F32),
               pltpu.VMEM((nblk, tb, w), BF16), pltpu.VMEM((tb, w), BF16),
               pltpu.VMEM((tb, w), BF16),
               pltpu.VMEM((HGRN_HEADS, HGRN_DK, HGRN_DK), F32)]

    g_spec, lane_g = _main_window(tb, w, COL_HG_G, nb, True)

    def call(rev, final, extra_in, extra_specs):
        d = 1 if rev else 0
        q_spec, lane_q = _main_window(tb, w, COL_HG_Q, nb, rev)
        f_spec, lane_f = _main_window(tb, w, COL_HG_F + d * w, nb, rev)
        i_spec, lane_i = _main_window(tb, w, COL_HG_I, nb, rev)
        in_specs = [q_spec, f_spec, i_spec,
                    _const_spec((DEPTH, w)), _const_spec((nblk * tb, tb))] + extra_specs
        return pl.pallas_call(
            functools.partial(_hgrn_kernel, rev, final, layer, tb, nb, seq,
                              (lane_q, lane_f, lane_i, lane_g)),
            grid=(nb,),
            in_specs=in_specs,
            out_specs=_main_spec(tb, w, 0, nb, rev),
            out_shape=jax.ShapeDtypeStruct((t, w), F32),
            scratch_shapes=scratch,
            compiler_params=_scan_params(),
            name="hgrn_bwd" if rev else "hgrn_fwd",
        )(u, u, u, lb_param[d], _hgrn_masks(tb, rev), *extra_in)

    o_fwd = call(False, False, [], [])
    return call(True, True, [u, o_fwd, norm_w.reshape(1, -1), gn_w.reshape(1, w)],
                [g_spec, _main_spec(tb, w, 0, nb, True),
                 _const_spec((1, HGRN_DK)), _const_spec((1, w))])


def _outproj_kernel(starts, *refs):
    nsrc = len(starts)
    x_refs = refs[:nsrc]
    ya_ref, yb_ref, yc_ref, yd_ref, w_ref, o_ref = refs[nsrc:]
    acc = None
    for n, y_ref in enumerate((ya_ref, yb_ref, yc_ref, yd_ref)):
        part = _dot(_bf(y_ref[...]), w_ref[n * GROUP_W:(n + 1) * GROUP_W, :])
        acc = part if acc is None else acc + part

    def add_residual(x_ref):
        o_ref[...] = x_ref[...] + acc

    _select_rows(pl.program_id(0), x_refs, starts, add_residual)


def _outproj(xs, ys, w, layer):
    d = xs[0].shape[1]
    t = sum(x.shape[0] for x in xs)
    tm = 512
    x_specs, starts = _row_source_specs(xs, tm)
    yspec = pl.BlockSpec((tm, GROUP_W), lambda i: (i, 0))
    return pl.pallas_call(
        functools.partial(_outproj_kernel, starts),
        grid=(t // tm,),
        in_specs=x_specs + [yspec, yspec, yspec, yspec,
                            pl.BlockSpec((None,) + w.shape[1:], lambda i: (layer, 0, 0))],
        out_specs=pl.BlockSpec((tm, d), lambda i: (i, 0)),
        out_shape=jax.ShapeDtypeStruct((t, d), F32),
        compiler_params=pltpu.CompilerParams(dimension_semantics=("arbitrary",),
                                             vmem_limit_bytes=VMEM_LIMIT),
        name="outproj",
    )(*xs, *ys, w)


def _ffn_kernel(apply_final, tm, nrb, nf, seq, out_starts, *refs):
    (x_ref, prev_ref, next_ref, ln_ref, wg_ref, wv_ref, cwg_ref, cwv_ref, cbg_ref, cbv_ref,
     wd_ref, fn_ref) = refs[:12]
    o_refs = refs[12:12 + len(out_starts)]
    h_s, g_s, v_s, acc_s = refs[12 + len(out_starts):]
    j = pl.program_id(1)
    i = pl.program_id(0)
    row0 = i * tm
    first = _seq_start(row0, seq)
    last = _seq_start(row0 + tm, seq)

    @pl.when(j == 0)
    def _():
        keep_p = jnp.where(first, 0.0, 1.0)
        keep_n = jnp.where(last, 0.0, 1.0)
        h_s[0:FHALO, :] = _bf(_rms(prev_ref[...], ln_ref[...]) * keep_p)
        h_s[FHALO:FHALO + tm, :] = _bf(_rms(x_ref[...], ln_ref[...]))
        h_s[FHALO + tm:2 * FHALO + tm, :] = _bf(_rms(next_ref[...], ln_ref[...]) * keep_n)
        acc_s[...] = jnp.zeros_like(acc_s)

    hb = h_s[...]
    g_s[...] = _dot(hb, wg_ref[...])
    v_s[...] = _dot(hb, wv_ref[...])

    def conv3(s_ref, cw_ref, cb_ref):
        acc = s_ref[FHALO - 1:FHALO - 1 + tm, :] * cw_ref[0:1, :]
        acc = acc + s_ref[FHALO:FHALO + tm, :] * cw_ref[1:2, :]
        acc = acc + s_ref[FHALO + 1:FHALO + 1 + tm, :] * cw_ref[2:3, :]
        return acc + cb_ref[...]

    act = _silu(conv3(g_s, cwg_ref, cbg_ref)) * conv3(v_s, cwv_ref, cbv_ref)
    acc_s[...] += _dot(_bf(act), wd_ref[...])

    @pl.when(j == nf - 1)
    def _():
        y = x_ref[...] + acc_s[...]
        if apply_final:
            y = _rms(y, fn_ref[...])

        def write(o_ref):
            o_ref[...] = y

        _select_rows(i, o_refs, out_starts, write)


def _ffn(x, ln, w_up, conv_w, conv_b, w_down, layer, final_w, apply_final, seq, out_rows):
    t, d = x.shape
    tm = 512
    tf = 512
    nf = D_FF // tf
    nrb = t // tm
    r = tm // FHALO
    cb = conv_b.reshape(1, 2 * D_FF)
    up_buf = pltpu.VMEM((tm + 2 * FHALO, tf), F32)
    assert sum(out_rows) == t
    out_specs, out_starts = _row_source_specs([jax.ShapeDtypeStruct((n, d), F32) for n in out_rows], tm)
    return pl.pallas_call(
        functools.partial(_ffn_kernel, apply_final, tm, nrb, nf, seq, out_starts),
        grid=(nrb, nf),
        in_specs=[pl.BlockSpec((tm, d), lambda i, j: (i, 0)),
                  pl.BlockSpec((FHALO, d), lambda i, j: (jnp.maximum(i * r - 1, 0), 0)),
                  pl.BlockSpec((FHALO, d), lambda i, j: (jnp.minimum((i + 1) * r, t // FHALO - 1), 0)),
                  pl.BlockSpec((1, d), lambda i, j: (0, 0)),
                  pl.BlockSpec((None, d, tf), lambda i, j: (layer, 0, j)),
                  pl.BlockSpec((None, d, tf), lambda i, j: (layer, 0, j + nf)),
                  pl.BlockSpec((3, tf), lambda i, j: (0, j)),
                  pl.BlockSpec((3, tf), lambda i, j: (0, j + nf)),
                  pl.BlockSpec((1, tf), lambda i, j: (0, j)),
                  pl.BlockSpec((1, tf), lambda i, j: (0, j + nf)),
                  pl.BlockSpec((None, tf, d), lambda i, j: (layer, j, 0)),
                  pl.BlockSpec((1, d), lambda i, j: (0, 0))],
        out_specs=tuple(out_specs),
        out_shape=tuple(jax.ShapeDtypeStruct((n, d), F32) for n in out_rows),
        scratch_shapes=[pltpu.VMEM((tm + 2 * FHALO, d), BF16), up_buf, up_buf,
                        pltpu.VMEM((tm, d), F32)],
        compiler_params=pltpu.CompilerParams(dimension_semantics=("arbitrary", "arbitrary"),
                                             vmem_limit_bytes=VMEM_LIMIT),
        name="ffn",
    )(x, x, x, ln.reshape(1, d), w_up, w_up, conv_w, conv_w, cb, cb, w_down, final_w.reshape(1, d))


def _pad_w_in(w, dtype=None):
    assert w.shape[-1] == D_IN
    dtype = dtype or w.dtype
    split = COL_HG_Q - HG_SHIFT
    zeros = jnp.zeros(w.shape[:-1] + (HG_SHIFT,), dtype)
    return jnp.concatenate([w[..., :split].astype(dtype), zeros, w[..., split:].astype(dtype)], axis=-1)


def kernel(x_prompt, x_sample, ln1, w_in, gdn_conv_w, gdn_a_log, gdn_dt_bias, gdn_norm_w, lru_conv_w, lru_conv_b, lru_wa, lru_ba, lru_wi, lru_bi, lru_lambda, ssd_conv_w, ssd_conv_b, ssd_a_log, ssd_dt_bias, ssd_d, ssd_norm_w, hgrn_lb, hgrn_norm_w, group_norm_w, w_out, ln2, w_up, ffn_conv_w, ffn_conv_b, w_down, final_norm):
    b, s, dm = x_prompt.shape
    db, ds, _ = x_sample.shape
    seq = (b * s, s, ds)
    assert s % TB_SCAN == 0 and ds % TB_SCAN == 0 and s % TB_HGRN == 0 and ds % TB_HGRN == 0
    xs = [x_prompt.reshape(b * s, dm), x_sample.reshape(db * ds, dm)]
    depth = w_in.shape[0]
    w_in_b = _pad_w_in(w_in, BF16)
    w_out_b = w_out.astype(BF16)
    w_up_b = w_up.astype(BF16)
    w_down_b = w_down.astype(BF16)
    for l in range(depth):
        last_layer = l == depth - 1
        u = _inproj(xs, ln1[l], w_in_b, l)
        gn = group_norm_w[l]
        ya = _gdn(u, gdn_conv_w[l], gdn_a_log[l], gdn_dt_bias[l], gdn_norm_w[l], gn[0], TB_SCAN, seq)
        yb = _lru(u, lru_conv_w[l], lru_conv_b[l], lru_wa[l], lru_ba[l], lru_wi[l], lru_bi[l],
                  lru_lambda[l], gn[1], TB_SCAN, seq)
        yc = _ssd(u, ssd_conv_w[l], ssd_conv_b[l], ssd_a_log[l], ssd_dt_bias[l], ssd_d[l],
                  ssd_norm_w[l], gn[2], TB_SCAN, seq)
        yd = _hgrn(u, hgrn_lb, l, hgrn_norm_w[l], gn[3], TB_HGRN, seq)
        x = _outproj(xs, (ya, yb, yc, yd), w_out_b, l)
        out_rows = [b * s, db * ds] if last_layer else [b * s + db * ds]
        xs = list(_ffn(x, ln2[l], w_up_b, ffn_conv_w[l], ffn_conv_b[l], w_down_b, l,
                       final_norm, last_layer, seq, out_rows))
    return (xs[0].reshape(b, s, dm), xs[1].reshape(db, ds, dm))
```

```python
import functools
import math

import jax
import jax.numpy as jnp
import numpy as np
from jax import lax
from jax.experimental import pallas as pl
from jax.experimental.pallas import tpu as pltpu

F32 = jnp.float32
BF16 = jnp.bfloat16

D_MODEL = 2048
DEPTH = 2
GROUP_W = 512
GDN_HEADS = 4
GDN_DK = 128
LRU_BLOCKS = 4
LRU_BW = 128
LRU_C = 8.0
SSD_HEADS = 8
SSD_HEADDIM = 64
SSD_GROUPS = 2
SSD_STATE = 128
HGRN_HEADS = 4
HGRN_DK = 128
D_FF = 5632
CHUNK = 64
HGRN_LEVELS = 6
HGRN_STATE_CHUNK = 1 << HGRN_LEVELS
EPS = 1e-6

COL_GDN_QKV = 0
COL_GDN_Z = 1536
COL_GDN_SMALL = 2048
COL_LRU_X = 2064
COL_LRU_GATE = 2576
COL_SSD_Z = 3088
COL_SSD_XBC = 3600
COL_SSD_SMALL = 4608
HG_SHIFT = 96
COL_HG_Q = 4640 + HG_SHIFT
COL_HG_F = 5152 + HG_SHIFT
COL_HG_I = 6176 + HG_SHIFT
COL_HG_G = 6688 + HG_SHIFT
D_IN = 7200
D_IN_P = D_IN + HG_SHIFT
SM_BETA = 0
SM_ALPHA = 8
SM_DT = 16

SUBLANES = 8
HALO = 8
FHALO = 16
VMEM_LIMIT = 56 * 1024 * 1024
TB_SCAN = 512
TB_HGRN = 256

def _sig(x):
    return 1.0 / (1.0 + jnp.exp(-x))


def _silu(x):
    return x * _sig(x)


def _softplus(x):
    return jnp.maximum(x, 0.0) + jnp.log1p(jnp.exp(-jnp.abs(x)))


def _gelu_tanh(x):
    c = math.sqrt(2.0 / math.pi)
    return 0.5 * x * (1.0 + jnp.tanh(c * (x + 0.044715 * (x * x * x))))


def _bf(x):
    return x.astype(BF16)


def _dot(a, b):
    return jnp.dot(a, b, preferred_element_type=F32)


def _dot1(a, b):
    return _dot(_bf(a), _bf(b))


def _dot1_nt(a, b):
    return lax.dot_general(_bf(a), _bf(b), (((1,), (1,)), ((), ())),
                           preferred_element_type=F32)


def _dot1_tn(a, b):
    return lax.dot_general(_bf(a), _bf(b), (((0,), (0,)), ((), ())),
                           preferred_element_type=F32)


def _dot_mask_l(m, b):
    b1 = _bf(b)
    r1 = b - b1.astype(F32)
    b2 = _bf(r1)
    b3 = _bf(r1 - b2.astype(F32))
    return _dot(m, b1) + (_dot(m, b2) + _dot(m, b3))


def _dot_mask_r(a, m):
    a1 = _bf(a)
    r1 = a - a1.astype(F32)
    a2 = _bf(r1)
    a3 = _bf(r1 - a2.astype(F32))
    return _dot(a1, m) + (_dot(a2, m) + _dot(a3, m))


def _rms(x, w):
    return x * lax.rsqrt(jnp.mean(x * x, axis=-1, keepdims=True) + EPS) * w


def _seq_start(row, seq):
    p_rows, s_len, ds_len = seq
    return jnp.where(row < p_rows, lax.rem(row, s_len) == 0,
                     lax.rem(row - p_rows, ds_len) == 0)


def _block_pos(nb, tb, rev, seq):
    i = pl.program_id(0)
    j = (nb - 1 - i) if rev else i
    row0 = j * tb
    first = _seq_start(row0, seq)
    last = _seq_start(row0 + tb, seq)
    return first, last


def _fill_pad(xp_ref, blk_ref, prev_ref, next_ref, first, last, tb):
    keep_p = jnp.where(first, 0.0, 1.0)
    keep_n = jnp.where(last, 0.0, 1.0)
    xp_ref[0:HALO, :] = prev_ref[...] * keep_p
    xp_ref[HALO:HALO + tb, :] = blk_ref[...]
    xp_ref[HALO + tb:2 * HALO + tb, :] = next_ref[...] * keep_n


def _conv4(xp_ref, w_ref, tb):
    acc = xp_ref[HALO - 2:HALO - 2 + tb, :] * w_ref[0:1, :]
    acc = acc + xp_ref[HALO - 1:HALO - 1 + tb, :] * w_ref[1:2, :]
    acc = acc + xp_ref[HALO:HALO + tb, :] * w_ref[2:3, :]
    acc = acc + xp_ref[HALO + 1:HALO + 1 + tb, :] * w_ref[3:4, :]
    return acc


def _iota2(shape):
    return (lax.broadcasted_iota(jnp.int32, shape, 0),
            lax.broadcasted_iota(jnp.int32, shape, 1))


def _bdot(a, b):
    return lax.dot_general(a, b, (((2,), (1,)), ((0,), (0,))), preferred_element_type=F32)


def _bdot1_nt(a, b):
    return lax.dot_general(_bf(a), _bf(b), (((2,), (2,)), ((0,), (0,))),
                           preferred_element_type=F32)


def _bdot1(a, b):
    return _bdot(_bf(a), _bf(b))


def _unit_tri_inverse(a, r, c, n):
    eye = jnp.where(r == c, 1.0, 0.0)[None]
    d = jnp.where(((r >> 3) == (c >> 3))[None], a, 0.0)
    d2 = _bdot1(d, d)
    x = eye - d
    x = x + _bdot1(x, d2)
    d4 = _bdot1(d2, d2)
    x = x + _bdot1(x, d4)
    b = 16
    sh = 4
    while b <= n:
        joined = ((r >> sh) == (c >> sh)) & ((r >> (sh - 1)) != (c >> (sh - 1)))
        e = jnp.where(joined[None], a, 0.0)
        x = x - _bdot1(_bdot1(x, e), x)
        b *= 2
        sh += 1
    return x


def _row_source_specs(xs, tm):
    d = xs[0].shape[1]
    starts = []
    specs = []
    off = 0
    for x in xs:
        assert x.shape[0] % tm == 0
        nblk = x.shape[0] // tm

        def imap(i, *_, off=off, nblk=nblk):
            return (jnp.clip(i - off, 0, nblk - 1), 0)

        specs.append(pl.BlockSpec((tm, d), imap))
        starts.append(off)
        off += nblk
    return specs, starts


def _select_rows(i, refs, starts, fn):
    for n, ref in enumerate(refs):
        lo = starts[n]
        hi = starts[n + 1] if n + 1 < len(starts) else None
        cond = i >= lo if hi is None else ((i >= lo) & (i < hi))
        if len(refs) == 1:
            fn(ref)
        else:
            pl.when(cond)(functools.partial(fn, ref))


def _inproj_kernel(starts, *refs):
    nsrc = len(starts)
    x_refs = refs[:nsrc]
    ln_ref, w_ref, o_ref, h_s = refs[nsrc:]

    def norm_rows(x_ref):
        h_s[...] = _bf(_rms(x_ref[...], ln_ref[...]))

    @pl.when(pl.program_id(1) == 0)
    def _():
        _select_rows(pl.program_id(0), x_refs, starts, norm_rows)

    o_ref[...] = _dot(h_s[...], w_ref[...])


def _inproj(xs, ln, w, layer):
    d = xs[0].shape[1]
    t = sum(x.shape[0] for x in xs)
    n = w.shape[2]
    tm = 512
    tn = n // 3
    x_specs, starts = _row_source_specs(xs, tm)
    return pl.pallas_call(
        functools.partial(_inproj_kernel, starts),
        grid=(t // tm, n // tn),
        in_specs=x_specs + [pl.BlockSpec((1, d), lambda i, j: (0, 0)),
                            pl.BlockSpec((None, d, tn), lambda i, j: (layer, 0, j))],
        out_specs=pl.BlockSpec((tm, tn), lambda i, j: (i, j)),
        out_shape=jax.ShapeDtypeStruct((t, n), F32),
        scratch_shapes=[pltpu.VMEM((tm, d), BF16)],
        compiler_params=pltpu.CompilerParams(
            dimension_semantics=("arbitrary", "arbitrary"),
            vmem_limit_bytes=VMEM_LIMIT),
        name="inproj",
    )(*xs, ln.reshape(1, d), w)


def _bidx(nb, rev):
    if rev:
        return lambda i: nb - 1 - i
    return lambda i: i


LANES = 128


def _window_spec(rows, w, col, row_block):
    if col % w == 0:
        return pl.BlockSpec((rows, w), lambda i: (row_block(i), col // w)), 0
    if col % LANES == 0:
        return pl.BlockSpec((pl.Element(rows), pl.Element(w)), lambda i: (row_block(i) * rows, col)), 0
    wide = w + LANES
    start = min(col - col % LANES, D_IN_P - wide)
    spec = pl.BlockSpec((pl.Element(rows), pl.Element(wide)), lambda i: (row_block(i) * rows, start))
    return spec, col - start


class _LaneWindow:
    def __init__(self, ref, lane0, w):
        self.ref, self.lane0, self.w = ref, lane0, w

    def __getitem__(self, idx):
        assert idx is Ellipsis
        return self.ref[:, self.lane0:self.lane0 + self.w]


def _main_spec(tb, w, col, nb, rev):
    spec, lane0 = _window_spec(tb, w, col, _bidx(nb, rev))
    assert lane0 == 0
    return spec


def _main_window(tb, w, col, nb, rev):
    return _window_spec(tb, w, col, _bidx(nb, rev))


def _halo_specs(tb, w, col, nb, rev, t):
    f = _bidx(nb, rev)
    r = tb // HALO
    prev, _ = _window_spec(HALO, w, col, lambda i: jnp.maximum(f(i) * r - 1, 0))
    nxt, _ = _window_spec(HALO, w, col, lambda i: jnp.minimum((f(i) + 1) * r, t // HALO - 1))
    return prev, nxt


def _const_spec(shape):
    nd = len(shape)
    return pl.BlockSpec(shape, lambda i: (0,) * nd)


def _scan_params():
    return pltpu.CompilerParams(dimension_semantics=("arbitrary",),
                                vmem_limit_bytes=VMEM_LIMIT)


def _gdn_kernel(rev, final, tb, nb, seq, *refs):
    if final:
        (qkv_in_ref, small_ref, arow_ref, brow_ref,
         z_ref, ofwd_ref, nw_ref, gn_ref, out_ref,
         q3, k3, v3, bet3, gi3, gt3, gj3, w3, u3, qk3, st_ref) = refs
    else:
        (qkv_ref, prev_ref, next_ref, small_ref, cw_ref, arow_ref, brow_ref, out_ref, qkv_out_ref,
         xp_ref, q3, k3, v3, bet3, gi3, gt3, gj3, w3, u3, qk3, st_ref) = refs
    d = 1 if rev else 0
    c = CHUNK
    c2 = 2 * CHUNK
    npair = tb // c2
    nh = GDN_HEADS
    hw = GDN_DK
    kw = GDN_HEADS * GDN_DK
    first, last = _block_pos(nb, tb, rev, seq)

    if not final:
        _fill_pad(xp_ref, qkv_ref, prev_ref, next_ref, first, last, tb)
        y = _silu(_conv4(xp_ref, cw_ref, tb))
    for h in range(nh):
        if final:
            qh = qkv_in_ref[:, h * hw:(h + 1) * hw]
            kh = qkv_in_ref[:, kw + h * hw:kw + (h + 1) * hw]
            vh = qkv_in_ref[:, 2 * kw + h * hw:2 * kw + (h + 1) * hw]
        else:
            qh = y[:, h * hw:(h + 1) * hw]
            qh = qh * lax.rsqrt(jnp.sum(qh * qh, axis=-1, keepdims=True) + EPS) * (GDN_DK ** -0.5)
            kh = y[:, kw + h * hw:kw + (h + 1) * hw]
            kh = kh * lax.rsqrt(jnp.sum(kh * kh, axis=-1, keepdims=True) + EPS)
            vh = y[:, 2 * kw + h * hw:2 * kw + (h + 1) * hw]
            qkv_out_ref[:, h * hw:(h + 1) * hw] = qh
            qkv_out_ref[:, kw + h * hw:kw + (h + 1) * hw] = kh
            qkv_out_ref[:, 2 * kw + h * hw:2 * kw + (h + 1) * hw] = vh
        for pi in range(npair):
            q3[pi * nh + h] = qh[pi * c2:(pi + 1) * c2, :]
            k3[pi * nh + h] = kh[pi * c2:(pi + 1) * c2, :]
            v3[pi * nh + h] = vh[pi * c2:(pi + 1) * c2, :]

    sm = small_ref[...]
    beta_all = _sig(sm)
    g_all = -jnp.exp(arow_ref[...]) * _softplus(sm + brow_ref[...])
    rb, cb_ = _iota2((tb, tb))
    same = (rb >> 6) == (cb_ >> 6)
    incl_b = same & ((cb_ >= rb) if rev else (cb_ <= rb))
    gcum = _dot_mask_l(jnp.where(incl_b, 1.0, 0.0).astype(BF16), g_all)
    gtot = _dot_mask_l(jnp.where(same, 1.0, 0.0).astype(BF16), g_all)
    gcum_t = gcum.T
    for pi in range(npair):
        rows = slice(pi * c2, (pi + 1) * c2)
        for h in range(nh):
            cb = SM_BETA + d * nh + h
            cg = SM_ALPHA + d * nh + h
            b = pi * nh + h
            bet3[b] = jnp.broadcast_to(beta_all[rows, cb:cb + 1], (c2, hw))
            gi3[b] = jnp.broadcast_to(gcum[rows, cg:cg + 1], (c2, hw))
            gt3[b] = jnp.broadcast_to(gtot[rows, cg:cg + 1], (c2, hw))
            gj3[b] = jnp.broadcast_to(gcum_t[cg:cg + 1, rows], (c2, c2))

    @pl.when(last if rev else first)
    def _():
        st_ref[...] = jnp.zeros_like(st_ref)

    r, cc = _iota2((c2, c2))
    same_chunk = (r >> 6) == (cc >> 6)
    incl = same_chunk & ((cc >= r) if rev else (cc <= r))
    strict = same_chunk & ((cc > r) if rev else (cc < r))
    q = q3[...]
    k = k3[...]
    bet = bet3[...]
    gi = gi3[...]
    decay = jnp.where(incl[None], jnp.exp(gi - gj3[...]), 0.0)
    kb = k * bet
    a = jnp.where(strict[None], _bdot1_nt(kb, k) * decay, 0.0)
    tm = _unit_tri_inverse(a, r, cc, c)
    eg = jnp.exp(gi)
    w3[...] = _bdot1(tm, kb * eg)
    u3[...] = _bdot1(tm, v3[...] * bet)
    qk3[...] = _bdot1_nt(q, k) * decay
    q3[...] = q * eg
    k3[...] = k * jnp.exp(gt3[...] - gi)

    def chunk_pair(pi, carry):
        pj = (npair - 1 - pi) if rev else pi
        hs = range(nh)
        for half in ((1, 0) if rev else (0, 1)):
            r0 = pl.multiple_of(pj * c2 + half * c, c)
            hr = slice(half * c, (half + 1) * c)
            s = [st_ref[h] for h in hs]
            sb = [_bf(s[h]) for h in hs]
            ws = [_dot(_bf(w3[pj * nh + h, hr, :]), sb[h]) for h in hs]
            qs = [_dot(_bf(q3[pj * nh + h, hr, :]), sb[h]) for h in hs]
            v_new = [_bf(u3[pj * nh + h, hr, :] - ws[h]) for h in hs]
            kv = [_dot1_tn(k3[pj * nh + h, hr, :], v_new[h]) for h in hs]
            ov = [_dot(_bf(qk3[pj * nh + h, hr, hr]), v_new[h]) for h in hs]
            for h in hs:
                st_ref[h] = s[h] * jnp.exp(gt3[pj * nh + h, half * c:half * c + 1, :]) + kv[h]
                out_ref[pl.ds(r0, c), h * hw:(h + 1) * hw] = qs[h] + ov[h]
        return carry

    lax.fori_loop(0, npair, chunk_pair, 0)

    if final:
        o = out_ref[...] + ofwd_ref[...]
        z = z_ref[...]
        for h in range(GDN_HEADS):
            oh = _rms(o[:, h * hw:(h + 1) * hw], nw_ref[...])
            out_ref[:, h * hw:(h + 1) * hw] = oh * _silu(z[:, h * hw:(h + 1) * hw])
        out_ref[...] = _rms(out_ref[...], gn_ref[...])


def _gdn(u, conv_w, a_log, dt_bias, norm_w, gn_w, tb, seq):
    t = u.shape[0]
    nb = t // tb
    w3 = 3 * GROUP_W
    arow = jnp.zeros((1, 128), F32).at[0, SM_ALPHA:SM_ALPHA + 8].set(a_log.reshape(8))
    brow = jnp.zeros((1, 128), F32).at[0, SM_ALPHA:SM_ALPHA + 8].set(dt_bias.reshape(8))
    assert GDN_DK == 2 * CHUNK
    nbat = (tb // (2 * CHUNK)) * GDN_HEADS
    entry = pltpu.VMEM((nbat, 2 * CHUNK, GDN_DK), F32)
    batch_scratch = [entry] * 10 + [pltpu.VMEM((GDN_HEADS, GDN_DK, GDN_DK), F32)]
    o_sds = jax.ShapeDtypeStruct((t, GROUP_W), F32)

    prev, nxt = _halo_specs(tb, w3, COL_GDN_QKV, nb, False, t)
    o_fwd, qkv_act = pl.pallas_call(
        functools.partial(_gdn_kernel, False, False, tb, nb, seq),
        grid=(nb,),
        in_specs=[_main_spec(tb, w3, COL_GDN_QKV, nb, False), prev, nxt,
                  _main_spec(tb, 128, COL_GDN_SMALL, nb, False),
                  _const_spec((4, w3)), _const_spec((1, 128)), _const_spec((1, 128))],
        out_specs=(_main_spec(tb, GROUP_W, 0, nb, False), _main_spec(tb, w3, 0, nb, False)),
        out_shape=(o_sds, jax.ShapeDtypeStruct((t, w3), F32)),
        scratch_shapes=[pltpu.VMEM((tb + 2 * HALO, w3), F32)] + batch_scratch,
        compiler_params=_scan_params(),
        name="gdn_fwd",
    )(u, u, u, u, conv_w, arow, brow)
    return pl.pallas_call(
        functools.partial(_gdn_kernel, True, True, tb, nb, seq),
        grid=(nb,),
        in_specs=[_main_spec(tb, w3, 0, nb, True), _main_spec(tb, 128, COL_GDN_SMALL, nb, True),
                  _const_spec((1, 128)), _const_spec((1, 128)),
                  _main_spec(tb, GROUP_W, COL_GDN_Z, nb, True), _main_spec(tb, GROUP_W, 0, nb, True),
                  _const_spec((1, GDN_DK)), _const_spec((1, GROUP_W))],
        out_specs=_main_spec(tb, GROUP_W, 0, nb, True),
        out_shape=o_sds,
        scratch_shapes=batch_scratch,
        compiler_params=_scan_params(),
        name="gdn_bwd",
    )(qkv_act, u, arow, brow, u, o_fwd, norm_w.reshape(1, -1), gn_w.reshape(1, -1))


def _lru_kernel(rev, final, tb, nb, seq, lane_x, lane_gate, *refs):
    if final:
        (x_ref, prev_ref, next_ref, cw_ref, cb_ref, w_ref, ba_ref, bi_ref, lam_ref,
         gate_ref, ofwd_ref, gn_ref, out_ref, xp_ref, h_ref) = refs
        gate_ref = _LaneWindow(gate_ref, lane_gate, GROUP_W)
    else:
        (x_ref, prev_ref, next_ref, cw_ref, cb_ref, w_ref, ba_ref, bi_ref, lam_ref,
         out_ref, xp_ref, h_ref) = refs
    x_ref, prev_ref, next_ref = (_LaneWindow(r, lane_x, GROUP_W) for r in (x_ref, prev_ref, next_ref))
    first, last = _block_pos(nb, tb, rev, seq)
    start = last if rev else first
    bw = LRU_BW

    _fill_pad(xp_ref, x_ref, prev_ref, next_ref, first, last, tb)
    xc = _conv4(xp_ref, cw_ref, tb) + cb_ref[...]

    @pl.when(start)
    def _():
        h_ref[...] = jnp.zeros_like(h_ref)

    rows = lax.broadcasted_iota(jnp.int32, (tb, bw), 0)
    tile_rows = lax.broadcasted_iota(jnp.int32, (tb // SUBLANES, SUBLANES, bw), 1)
    start_row = (tb - 1) if rev else 0
    is_start_row = (rows == start_row) & start
    sp = _softplus(-lam_ref[...])
    for n in range(LRU_BLOCKS):
        xb = xc[:, n * bw:(n + 1) * bw]
        pre = _dot1(xb, w_ref[n])
        rg = _sig(pre[:, :bw] + ba_ref[:, n * bw:(n + 1) * bw])
        ig = _sig(pre[:, bw:] + bi_ref[:, n * bw:(n + 1) * bw])
        log_a = (-LRU_C) * rg * sp[:, n * bw:(n + 1) * bw]
        a = jnp.exp(log_a)
        mult = jnp.sqrt((1.0 + a * a) * jnp.tanh(-log_a))
        mult = jnp.where(is_start_row, 1.0, mult)
        b = mult * (ig * xb)
        ntiles = tb // SUBLANES
        a = a.reshape(ntiles, SUBLANES, bw)
        b = b.reshape(ntiles, SUBLANES, bw)
        s = 1
        while s < SUBLANES:
            sh = (SUBLANES - s) if rev else s
            a_sh = pltpu.roll(a, sh, 1)
            b_sh = pltpu.roll(b, sh, 1)
            ok = (tile_rows < SUBLANES - s) if rev else (tile_rows >= s)
            b = jnp.where(ok, a * b_sh, 0.0) + b
            a = jnp.where(ok, a * a_sh, a)
            s *= 2
        h_prev = h_ref[:, n * bw:(n + 1) * bw]
        for ti in (range(ntiles - 1, -1, -1) if rev else range(ntiles)):
            ht = a[ti] * h_prev + b[ti]
            out_ref[ti * SUBLANES:(ti + 1) * SUBLANES, n * bw:(n + 1) * bw] = ht
            h_prev = ht[0:1, :] if rev else ht[SUBLANES - 1:SUBLANES, :]
        h_ref[:, n * bw:(n + 1) * bw] = h_prev

    if final:
        o = (out_ref[...] + ofwd_ref[...]) * _gelu_tanh(gate_ref[...])
        out_ref[...] = _rms(o, gn_ref[...])


def _lru(u, conv_w, conv_b, wa, ba, wi, bi, lam, gn_w, tb, seq):
    t = u.shape[0]
    nb = t // tb
    w = GROUP_W
    wcat = jnp.concatenate([wa, wi], axis=-1).astype(BF16)
    scratch = [pltpu.VMEM((tb + 2 * HALO, w), F32), pltpu.VMEM((1, w), F32)]

    gate_spec, lane_gate = _main_window(tb, w, COL_LRU_GATE, nb, True)

    def call(rev, final, extra_in, extra_specs):
        d = 1 if rev else 0
        prev, nxt = _halo_specs(tb, w, COL_LRU_X, nb, rev, t)
        x_spec, lane_x = _main_window(tb, w, COL_LRU_X, nb, rev)
        in_specs = [x_spec, prev, nxt,
                    _const_spec((4, w)), _const_spec((1, w)),
                    _const_spec((LRU_BLOCKS, LRU_BW, 2 * LRU_BW)),
                    _const_spec((1, w)), _const_spec((1, w)), _const_spec((1, w))] + extra_specs
        return pl.pallas_call(
            functools.partial(_lru_kernel, rev, final, tb, nb, seq, lane_x, lane_gate),
            grid=(nb,),
            in_specs=in_specs,
            out_specs=_main_spec(tb, w, 0, nb, rev),
            out_shape=jax.ShapeDtypeStruct((t, w), F32),
            scratch_shapes=scratch,
            compiler_params=_scan_params(),
            name="lru_bwd" if rev else "lru_fwd",
        )(u, u, u, conv_w, conv_b.reshape(1, w), wcat[d], ba[d].reshape(1, w),
          bi[d].reshape(1, w), lam[d].reshape(1, w), *extra_in)

    o_fwd = call(False, False, [], [])
    return call(True, True, [u, o_fwd, gn_w.reshape(1, w)],
                [gate_spec, _main_spec(tb, w, 0, nb, True),
                 _const_spec((1, w))])


def _ssd_kernel(rev, final, tb, nb, seq, lane_xbc, lane_z, *refs):
    if final:
        (xbc_s, small_ref, arow_ref, brow_ref,
         z_ref, ofwd_ref, dskip_ref, nw_ref, gn_ref, out_ref, st_ref) = refs
        z_ref = _LaneWindow(z_ref, lane_z, GROUP_W)
    else:
        (xbc_ref, prev_ref, next_ref, small_ref, cw_ref, cb_ref, arow_ref, brow_ref,
         out_ref, xbc_s, xp_ref, st_ref) = refs
        wx = GROUP_W + 2 * SSD_GROUPS * SSD_STATE
        xbc_ref, prev_ref, next_ref = (_LaneWindow(r, lane_xbc, wx) for r in (xbc_ref, prev_ref, next_ref))
    d = 1 if rev else 0
    c = CHUNK
    nc = tb // c
    di = GROUP_W
    p = SSD_HEADDIM
    gw = di // SSD_GROUPS
    first, last = _block_pos(nb, tb, rev, seq)

    assert p == c
    hpg = SSD_HEADS // SSD_GROUPS

    if not final:
        _fill_pad(xp_ref, xbc_ref, prev_ref, next_ref, first, last, tb)
        xbc_s[...] = _silu(_conv4(xp_ref, cw_ref, tb) + cb_ref[...])
    dt = _softplus(small_ref[...] + brow_ref[...])
    da = dt * (-jnp.exp(arow_ref[...]))

    rb, cb_ = _iota2((tb, tb))
    same = (rb >> 6) == (cb_ >> 6)
    incl_b = same & ((cb_ >= rb) if rev else (cb_ <= rb))
    acs = _dot_mask_l(jnp.where(incl_b, 1.0, 0.0).astype(BF16), da)
    tot = _dot_mask_l(jnp.where(same, 1.0, 0.0).astype(BF16), da)
    acs_t = acs.T
    sr, sc = _iota2((128, di))
    sel = jnp.where(sr == SM_DT + d * SSD_HEADS + (sc >> 6), 1.0, 0.0).astype(BF16)
    acs_x = _dot_mask_r(acs, sel)
    tot_x = _dot_mask_r(tot, sel)
    dt_x = _dot_mask_r(dt, sel)
    xdt = xbc_s[:, 0:di] * dt_x
    xdt_e = xdt * jnp.exp(tot_x - acs_x)
    e_acs = jnp.exp(acs_x)
    e_tot = jnp.exp(tot_x)
    xdt_b = _bf(xdt)

    @pl.when(last if rev else first)
    def _():
        st_ref[...] = jnp.zeros_like(st_ref)

    rr, lj = _iota2((c, gw))
    incl_cat = ((lj & (c - 1)) >= rr) if rev else ((lj & (c - 1)) <= rr)
    rbd, cbd = _iota2((gw, gw))
    head_diag = (rbd >> 6) == (cbd >> 6)

    gss = [slice(g * gw, (g + 1) * gw) for g in range(SSD_GROUPS)]
    pairs = [(cj, g) for cj in range(nc) for g in range(SSD_GROUPS)]
    rsl = {cj: slice(cj * c, (cj + 1) * c) for cj in range(nc)}
    bm = {(cj, g): _bf(xbc_s[rsl[cj], di + g * SSD_STATE:di + (g + 1) * SSD_STATE]) for cj, g in pairs}
    cm = {(cj, g): _bf(xbc_s[rsl[cj], di + (SSD_GROUPS + g) * SSD_STATE:di + (SSD_GROUPS + g + 1) * SSD_STATE])
          for cj, g in pairs}
    cbm = {k: _dot1_nt(cm[k], bm[k]) for k in pairs}
    kx = {(cj, g): _dot1_tn(bm[(cj, g)], xdt_e[rsl[cj], gss[g]]) for cj, g in pairs}
    y_d = {}
    for cj, g in pairs:
        rs = rsl[cj]
        cb_cat = jnp.concatenate([cbm[(cj, g)]] * hpg, axis=1)
        col0 = SM_DT + d * SSD_HEADS + g * hpg
        a_row = jnp.concatenate([acs_t[col0 + hh:col0 + hh + 1, rs] for hh in range(hpg)], axis=1)
        l_cat = jnp.where(incl_cat, jnp.exp(acs_x[rs, gss[g]] - a_row), 0.0)
        x_bd = jnp.where(head_diag, jnp.concatenate([xdt_b[rs, gss[g]]] * hpg, axis=0),
                         jnp.zeros((), BF16))
        y_d[(cj, g)] = _dot1(cb_cat * l_cat, x_bd)
    s = [st_ref[g] for g in range(SSD_GROUPS)]
    for ci in range(nc):
        cj = (nc - 1 - ci) if rev else ci
        rs = rsl[cj]
        y_off = [_dot(cm[(cj, g)], _bf(s[g])) for g in range(SSD_GROUPS)]
        for g in range(SSD_GROUPS):
            out_ref[rs, gss[g]] = y_d[(cj, g)] + y_off[g] * e_acs[rs, gss[g]]
            s[g] = s[g] * e_tot[cj * c:cj * c + 1, gss[g]] + kx[(cj, g)]
    for g in range(SSD_GROUPS):
        st_ref[g] = s[g]

    if final:
        y = out_ref[...] + ofwd_ref[...] + dskip_ref[...] * xbc_s[:, 0:di]
        y = y * _silu(z_ref[...])
        for g in range(SSD_GROUPS):
            out_ref[:, g * gw:(g + 1) * gw] = _rms(y[:, g * gw:(g + 1) * gw], nw_ref[:, g * gw:(g + 1) * gw])
        out_ref[...] = _rms(out_ref[...], gn_ref[...])


def _ssd(u, conv_w, conv_b, a_log, dt_bias, d_skip, norm_w, gn_w, tb, seq):
    t = u.shape[0]
    nb = t // tb
    wx = GROUP_W + 2 * SSD_GROUPS * SSD_STATE
    arow = jnp.zeros((1, 128), F32).at[0, SM_DT:SM_DT + 16].set(a_log.reshape(16))
    brow = jnp.zeros((1, 128), F32).at[0, SM_DT:SM_DT + 16].set(dt_bias.reshape(16))
    dsk = jnp.repeat(d_skip, SSD_HEADDIM).reshape(1, GROUP_W)
    state = pltpu.VMEM((SSD_GROUPS, SSD_STATE, GROUP_W // SSD_GROUPS), F32)
    w = GROUP_W
    o_sds = jax.ShapeDtypeStruct((t, w), F32)

    prev, nxt = _halo_specs(tb, wx, COL_SSD_XBC, nb, False, t)
    xbc_spec, lane_xbc = _main_window(tb, wx, COL_SSD_XBC, nb, False)
    z_spec, lane_z = _main_window(tb, w, COL_SSD_Z, nb, True)
    o_fwd, xbc_act = pl.pallas_call(
        functools.partial(_ssd_kernel, False, False, tb, nb, seq, lane_xbc, lane_z),
        grid=(nb,),
        in_specs=[xbc_spec, prev, nxt,
                  _main_spec(tb, 128, COL_SSD_SMALL, nb, False),
                  _const_spec((4, wx)), _const_spec((1, wx)),
                  _const_spec((1, 128)), _const_spec((1, 128))],
        out_specs=(_main_spec(tb, w, 0, nb, False), _main_spec(tb, wx, 0, nb, False)),
        out_shape=(o_sds, jax.ShapeDtypeStruct((t, wx), F32)),
        scratch_shapes=[pltpu.VMEM((tb + 2 * HALO, wx), F32), state],
        compiler_params=_scan_params(),
        name="ssd_fwd",
    )(u, u, u, u, conv_w, conv_b.reshape(1, wx), arow, brow)
    return pl.pallas_call(
        functools.partial(_ssd_kernel, True, True, tb, nb, seq, lane_xbc, lane_z),
        grid=(nb,),
        in_specs=[_main_spec(tb, wx, 0, nb, True), _main_spec(tb, 128, COL_SSD_SMALL, nb, True),
                  _const_spec((1, 128)), _const_spec((1, 128)),
                  z_spec, _main_spec(tb, w, 0, nb, True),
                  _const_spec((1, w)), _const_spec((1, w)), _const_spec((1, w))],
        out_specs=_main_spec(tb, w, 0, nb, True),
        out_shape=o_sds,
        scratch_shapes=[state],
        compiler_params=_scan_params(),
        name="ssd_bwd",
    )(xbc_act, u, arow, brow, u, o_fwd, dsk, norm_w.reshape(1, w), gn_w.reshape(1, w))


def _hgrn_kernel(rev, final, layer, tb, nb, seq, lanes, *refs):
    if final:
        (q_ref, f_ref, i_ref, lb_ref, mask_ref, g_ref, ofwd_ref, nw_ref, gn_ref, out_ref,
         d_s, z_s, qd_s, kd_s, st_ref) = refs
        g_ref = _LaneWindow(g_ref, lanes[3], GROUP_W)
    else:
        (q_ref, f_ref, i_ref, lb_ref, mask_ref, out_ref,
         d_s, z_s, qd_s, kd_s, st_ref) = refs
    q_ref, f_ref, i_ref = (_LaneWindow(r, lanes[n], GROUP_W) for n, r in enumerate((q_ref, f_ref, i_ref)))
    c = HGRN_STATE_CHUNK
    nl = HGRN_LEVELS
    nc = tb // c
    w = GROUP_W
    hw = HGRN_DK
    first, last = _block_pos(nb, tb, rev, seq)

    lbp = lb_ref[...]
    mx = jnp.max(lbp, axis=0, keepdims=True)
    ex = jnp.exp(lbp - mx)
    den = jnp.sum(ex, axis=0, keepdims=True)
    lb = jnp.zeros((1, w), F32)
    for m in range(1, layer + 1):
        lb = lb + ex[m:m + 1, :] / den

    f = lb + (1.0 - lb) * _sig(f_ref[...])
    logf = jnp.log(f)
    kk = 1.0 - f
    qq = _silu(q_ref[...])
    vv = i_ref[...]

    l1 = _bf(logf)
    l2 = _bf(logf - l1.astype(F32))

    d_s[...] = _dot(mask_ref[...], l1) + _dot(mask_ref[...], l2)

    def range_sum(blk):
        return d_s[blk * tb:(blk + 1) * tb, :]

    rows = lax.broadcasted_iota(jnp.int32, (tb, 1), 0)
    z_s[0] = _bf(qq)
    z_s[1] = _bf(kk)
    for lv in range(1, nl + 1):
        bit = (rows >> (lv - 1)) & 1
        is_query = (bit == 0) if rev else (bit == 1)
        z_s[lv + 1] = _bf(jnp.exp(range_sum(lv - 1)) * jnp.where(is_query, qq, kk))
    bsc = range_sum(nl)
    btot = range_sum(nl + 1)
    qd_s[...] = _bf(qq * jnp.exp(bsc))
    kd_s[...] = _bf(kk * jnp.exp(btot - bsc))
    gl = jnp.exp(btot)
    vb = _bf(vv)

    r64, c64 = _iota2((c, c))
    pair_masks = [r64 == c64]
    for lv in range(1, nl + 1):
        rbit = (r64 >> (lv - 1)) & 1
        cbit = (c64 >> (lv - 1)) & 1
        split = ((rbit == 0) & (cbit == 1)) if rev else ((rbit == 1) & (cbit == 0))
        pair_masks.append(((r64 >> lv) == (c64 >> lv)) & split)

    @pl.when(last if rev else first)
    def _():
        st_ref[...] = jnp.zeros_like(st_ref)

    hs = range(HGRN_HEADS)
    sls = [slice(h * hw, (h + 1) * hw) for h in hs]
    s = [st_ref[h] for h in hs]
    for ci in range(nc):
        cj = (nc - 1 - ci) if rev else ci
        rs = slice(cj * c, (cj + 1) * c)
        prods = [[_dot1_nt(z_s[0, rs, sls[h]], z_s[1, rs, sls[h]])]
                 + [_dot1_nt(z_s[lv + 1, rs, sls[h]], z_s[lv + 1, rs, sls[h]]) for lv in range(1, nl + 1)]
                 for h in hs]
        o_st = [_dot1_nt(qd_s[rs, sls[h]], s[h]) for h in hs]
        kv = [_dot1_tn(vb[rs, sls[h]], kd_s[rs, sls[h]]) for h in hs]
        sc = []
        for h in hs:
            acc = jnp.where(pair_masks[0], prods[h][0], 0.0)
            for lv in range(1, nl + 1):
                acc = acc + jnp.where(pair_masks[lv], prods[h][lv], 0.0)
            sc.append(_bf(acc))
        y_in = [_dot(sc[h], vb[rs, sls[h]]) for h in hs]
        for h in hs:
            out_ref[rs, sls[h]] = y_in[h] + o_st[h]
            s[h] = s[h] * gl[cj * c:cj * c + 1, sls[h]] + kv[h]
    for h in hs:
        st_ref[h] = s[h]

    if final:
        o = out_ref[...] + ofwd_ref[...]
        gg = g_ref[...]
        for h in range(HGRN_HEADS):
            sl = slice(h * hw, (h + 1) * hw)
            out_ref[:, sl] = _rms(o[:, sl], nw_ref[...]) * _silu(gg[:, sl])
        out_ref[...] = _rms(out_ref[...], gn_ref[...])


def _hgrn_masks(tb, rev):
    t = np.arange(tb)[:, None]
    r = np.arange(tb)[None, :]
    blocks = []
    for lv in range(1, HGRN_LEVELS + 1):
        g = 1 << lv
        mid = (t // g) * g + g // 2
        if rev:
            m = np.where(t < mid, (r >= t) & (r < mid), (r >= mid) & (r < t))
        else:
            m = np.where(t >= mid, (r >= mid) & (r <= t), (r > t) & (r < mid))
        blocks.append(m)
    same = (r // HGRN_STATE_CHUNK) == (t // HGRN_STATE_CHUNK)
    blocks.append(same & ((r >= t) if rev else (r <= t)))
    blocks.append(same)
    return jnp.asarray(np.concatenate(blocks, axis=0).astype(np.float32), dtype=BF16)


def _hgrn(u, lb_param, layer, norm_w, gn_w, tb, seq):
    t = u.shape[0]
    nb = t // tb
    w = GROUP_W
    nblk = HGRN_LEVELS + 2
    scratch = [pltpu.VMEM((nblk * tb, w), F32),
               pltpu.VMEM((nblk, tb, w), BF16), pltpu.VMEM((tb, w), BF16),
               pltpu.VMEM((tb, w), BF16),
               pltpu.VMEM((HGRN_HEADS, HGRN_DK, HGRN_DK), F32)]

    g_spec, lane_g = _main_window(tb, w, COL_HG_G, nb, True)

    def call(rev, final, extra_in, extra_specs):
        d = 1 if rev else 0
        q_spec, lane_q = _main_window(tb, w, COL_HG_Q, nb, rev)
        f_spec, lane_f = _main_window(tb, w, COL_HG_F + d * w, nb, rev)
        i_spec, lane_i = _main_window(tb, w, COL_HG_I, nb, rev)
        in_specs = [q_spec, f_spec, i_spec,
                    _const_spec((DEPTH, w)), _const_spec((nblk * tb, tb))] + extra_specs
        return pl.pallas_call(
            functools.partial(_hgrn_kernel, rev, final, layer, tb, nb, seq,
                              (lane_q, lane_f, lane_i, lane_g)),
            grid=(nb,),
            in_specs=in_specs,
            out_specs=_main_spec(tb, w, 0, nb, rev),
            out_shape=jax.ShapeDtypeStruct((t, w), F32),
            scratch_shapes=scratch,
            compiler_params=_scan_params(),
            name="hgrn_bwd" if rev else "hgrn_fwd",
        )(u, u, u, lb_param[d], _hgrn_masks(tb, rev), *extra_in)

    o_fwd = call(False, False, [], [])
    return call(True, True, [u, o_fwd, norm_w.reshape(1, -1), gn_w.reshape(1, w)],
                [g_spec, _main_spec(tb, w, 0, nb, True),
                 _const_spec((1, HGRN_DK)), _const_spec((1, w))])


def _outproj_kernel(starts, *refs):
    nsrc = len(starts)
    x_refs = refs[:nsrc]
    ya_ref, yb_ref, yc_ref, yd_ref, w_ref, o_ref = refs[nsrc:]
    acc = None
    for n, y_ref in enumerate((ya_ref, yb_ref, yc_ref, yd_ref)):
        part = _dot(_bf(y_ref[...]), w_ref[n * GROUP_W:(n + 1) * GROUP_W, :])
        acc = part if acc is None else acc + part

    def add_residual(x_ref):
        o_ref[...] = x_ref[...] + acc

    _select_rows(pl.program_id(0), x_refs, starts, add_residual)


def _outproj(xs, ys, w, layer):
    d = xs[0].shape[1]
    t = sum(x.shape[0] for x in xs)
    tm = 512
    x_specs, starts = _row_source_specs(xs, tm)
    yspec = pl.BlockSpec((tm, GROUP_W), lambda i: (i, 0))
    return pl.pallas_call(
        functools.partial(_outproj_kernel, starts),
        grid=(t // tm,),
        in_specs=x_specs + [yspec, yspec, yspec, yspec,
                            pl.BlockSpec((None,) + w.shape[1:], lambda i: (layer, 0, 0))],
        out_specs=pl.BlockSpec((tm, d), lambda i: (i, 0)),
        out_shape=jax.ShapeDtypeStruct((t, d), F32),
        compiler_params=pltpu.CompilerParams(dimension_semantics=("arbitrary",),
                                             vmem_limit_bytes=VMEM_LIMIT),
        name="outproj",
    )(*xs, *ys, w)


def _ffn_kernel(apply_final, tm, nrb, nf, seq, out_starts, *refs):
    (x_ref, prev_ref, next_ref, ln_ref, wg_ref, wv_ref, cwg_ref, cwv_ref, cbg_ref, cbv_ref,
     wd_ref, fn_ref) = refs[:12]
    o_refs = refs[12:12 + len(out_starts)]
    h_s, g_s, v_s, acc_s = refs[12 + len(out_starts):]
    j = pl.program_id(1)
    i = pl.program_id(0)
    row0 = i * tm
    first = _seq_start(row0, seq)
    last = _seq_start(row0 + tm, seq)

    @pl.when(j == 0)
    def _():
        keep_p = jnp.where(first, 0.0, 1.0)
        keep_n = jnp.where(last, 0.0, 1.0)
        h_s[0:FHALO, :] = _bf(_rms(prev_ref[...], ln_ref[...]) * keep_p)
        h_s[FHALO:FHALO + tm, :] = _bf(_rms(x_ref[...], ln_ref[...]))
        h_s[FHALO + tm:2 * FHALO + tm, :] = _bf(_rms(next_ref[...], ln_ref[...]) * keep_n)
        acc_s[...] = jnp.zeros_like(acc_s)

    hb = h_s[...]
    g_s[...] = _dot(hb, wg_ref[...])
    v_s[...] = _dot(hb, wv_ref[...])

    def conv3(s_ref, cw_ref, cb_ref):
        acc = s_ref[FHALO - 1:FHALO - 1 + tm, :] * cw_ref[0:1, :]
        acc = acc + s_ref[FHALO:FHALO + tm, :] * cw_ref[1:2, :]
        acc = acc + s_ref[FHALO + 1:FHALO + 1 + tm, :] * cw_ref[2:3, :]
        return acc + cb_ref[...]

    act = _silu(conv3(g_s, cwg_ref, cbg_ref)) * conv3(v_s, cwv_ref, cbv_ref)
    acc_s[...] += _dot(_bf(act), wd_ref[...])

    @pl.when(j == nf - 1)
    def _():
        y = x_ref[...] + acc_s[...]
        if apply_final:
            y = _rms(y, fn_ref[...])

        def write(o_ref):
            o_ref[...] = y

        _select_rows(i, o_refs, out_starts, write)


def _ffn(x, ln, w_up, conv_w, conv_b, w_down, layer, final_w, apply_final, seq, out_rows):
    t, d = x.shape
    tm = 512
    tf = 512
    nf = D_FF // tf
    nrb = t // tm
    r = tm // FHALO
    cb = conv_b.reshape(1, 2 * D_FF)
    up_buf = pltpu.VMEM((tm + 2 * FHALO, tf), F32)
    assert sum(out_rows) == t
    out_specs, out_starts = _row_source_specs([jax.ShapeDtypeStruct((n, d), F32) for n in out_rows], tm)
    return pl.pallas_call(
        functools.partial(_ffn_kernel, apply_final, tm, nrb, nf, seq, out_starts),
        grid=(nrb, nf),
        in_specs=[pl.BlockSpec((tm, d), lambda i, j: (i, 0)),
                  pl.BlockSpec((FHALO, d), lambda i, j: (jnp.maximum(i * r - 1, 0), 0)),
                  pl.BlockSpec((FHALO, d), lambda i, j: (jnp.minimum((i + 1) * r, t // FHALO - 1), 0)),
                  pl.BlockSpec((1, d), lambda i, j: (0, 0)),
                  pl.BlockSpec((None, d, tf), lambda i, j: (layer, 0, j)),
                  pl.BlockSpec((None, d, tf), lambda i, j: (layer, 0, j + nf)),
                  pl.BlockSpec((3, tf), lambda i, j: (0, j)),
                  pl.BlockSpec((3, tf), lambda i, j: (0, j + nf)),
                  pl.BlockSpec((1, tf), lambda i, j: (0, j)),
                  pl.BlockSpec((1, tf), lambda i, j: (0, j + nf)),
                  pl.BlockSpec((None, tf, d), lambda i, j: (layer, j, 0)),
                  pl.BlockSpec((1, d), lambda i, j: (0, 0))],
        out_specs=tuple(out_specs),
        out_shape=tuple(jax.ShapeDtypeStruct((n, d), F32) for n in out_rows),
        scratch_shapes=[pltpu.VMEM((tm + 2 * FHALO, d), BF16), up_buf, up_buf,
                        pltpu.VMEM((tm, d), F32)],
        compiler_params=pltpu.CompilerParams(dimension_semantics=("arbitrary", "arbitrary"),
                                             vmem_limit_bytes=VMEM_LIMIT),
        name="ffn",
    )(x, x, x, ln.reshape(1, d), w_up, w_up, conv_w, conv_w, cb, cb, w_down, final_w.reshape(1, d))


def _pad_w_in(w, dtype=None):
    assert w.shape[-1] == D_IN
    dtype = dtype or w.dtype
    split = COL_HG_Q - HG_SHIFT
    w = w.astype(dtype)
    out = jnp.zeros(w.shape[:-1] + (D_IN_P,), dtype)
    out = out.at[..., :split].set(w[..., :split])
    return out.at[..., COL_HG_Q:].set(w[..., split:])


def kernel(x_prompt, x_sample, ln1, w_in, gdn_conv_w, gdn_a_log, gdn_dt_bias, gdn_norm_w, lru_conv_w, lru_conv_b, lru_wa, lru_ba, lru_wi, lru_bi, lru_lambda, ssd_conv_w, ssd_conv_b, ssd_a_log, ssd_dt_bias, ssd_d, ssd_norm_w, hgrn_lb, hgrn_norm_w, group_norm_w, w_out, ln2, w_up, ffn_conv_w, ffn_conv_b, w_down, final_norm):
    b, s, dm = x_prompt.shape
    db, ds, _ = x_sample.shape
    seq = (b * s, s, ds)
    assert s % TB_SCAN == 0 and ds % TB_SCAN == 0 and s % TB_HGRN == 0 and ds % TB_HGRN == 0
    xs = [x_prompt.reshape(b * s, dm), x_sample.reshape(db * ds, dm)]
    depth = w_in.shape[0]
    w_in_b = _pad_w_in(w_in, BF16)
    w_out_b = w_out.astype(BF16)
    w_up_b = w_up.astype(BF16)
    w_down_b = w_down.astype(BF16)
    for l in range(depth):
        last_layer = l == depth - 1
        u = _inproj(xs, ln1[l], w_in_b, l)
        gn = group_norm_w[l]
        ya = _gdn(u, gdn_conv_w[l], gdn_a_log[l], gdn_dt_bias[l], gdn_norm_w[l], gn[0], TB_SCAN, seq)
        yb = _lru(u, lru_conv_w[l], lru_conv_b[l], lru_wa[l], lru_ba[l], lru_wi[l], lru_bi[l],
                  lru_lambda[l], gn[1], TB_SCAN, seq)
        yc = _ssd(u, ssd_conv_w[l], ssd_conv_b[l], ssd_a_log[l], ssd_dt_bias[l], ssd_d[l],
                  ssd_norm_w[l], gn[2], TB_SCAN, seq)
        yd = _hgrn(u, hgrn_lb, l, hgrn_norm_w[l], gn[3], TB_HGRN, seq)
        x = _outproj(xs, (ya, yb, yc, yd), w_out_b, l)
        out_rows = [b * s, db * ds] if last_layer else [b * s + db * ds]
        xs = list(_ffn(x, ln2[l], w_up_b, ffn_conv_w[l], ffn_conv_b[l], w_down_b, l,
                       final_norm, last_layer, seq, out_rows))
    return (xs[0].reshape(b, s, dm), xs[1].reshape(db, ds, dm))
```
